```python
import jax, jax.numpy as jnp
from jax import lax
import numpy as np

D_MODEL = 1024
BATCH = 1
SEQ = 16384
DEPTH = 1
DEC_BATCH = 32
DEC_SEQ = 16
PAST_LEN = 4096

CHUNK = 64
POOL_WIDTH = D_MODEL // 2
POOL_WINDOWS = (2, 4, 8, 16)
POOL_GROUPS = len(POOL_WINDOWS)
POOL_GD = POOL_WIDTH // POOL_GROUPS
POOL_HIST = max(POOL_WINDOWS) - 1
GLA_HEADS = 4
GLA_DK = D_MODEL // 2
GLA_DV = D_MODEL
GLA_HK = GLA_DK // GLA_HEADS
GLA_HV = GLA_DV // GLA_HEADS
GLA_LOWRANK = 16
GLA_GATE_NORM = 16.0
D_FF = 4 * D_MODEL
EPS = 1e-6
IN_SIZES = (POOL_WIDTH, GLA_DK, GLA_DK, GLA_DV, GLA_DV, GLA_LOWRANK, D_MODEL, D_MODEL)
IN_WIDTH = sum(IN_SIZES)
IN_OFFSETS = tuple(int(o) for o in np.cumsum(IN_SIZES)[:-1])

kernel_name = "hybrid_pool_gla_adaln_stream_step"


def rmsnorm(x, g):
    xf = x.astype(jnp.float32)
    r = lax.rsqrt(jnp.mean(xf * xf, axis=-1, keepdims=True) + EPS)
    return (xf * r * g.astype(jnp.float32)).astype(x.dtype)


def modulate(h, shift, scale):
    return h * (1 + scale[:, None, :]) + shift[:, None, :]


def pool_mixer(u, hist, pos0, w_pool, pool_scale):
    B, L, _ = u.shape
    ext = jnp.concatenate([hist.astype(u.dtype), u], axis=1)
    cs = jnp.cumsum(ext.astype(jnp.float32), axis=1)
    cs = jnp.concatenate([jnp.zeros((B, 1, POOL_WIDTH), jnp.float32), cs], axis=1)
    end = cs[:, POOL_HIST + 1:]
    pos = pos0 + jnp.arange(L)
    uf = u.astype(jnp.float32)
    outs = []
    for gi, w in enumerate(POOL_WINDOWS):
        sl = slice(gi * POOL_GD, (gi + 1) * POOL_GD)
        start = cs[:, POOL_HIST + 1 - w: POOL_HIST + 1 - w + L, sl]
        cnt = jnp.minimum(pos + 1, w).astype(jnp.float32)[None, :, None]
        outs.append((end[..., sl] - start) / cnt - uf[..., sl])
    d = jnp.stack(outs, axis=2)
    mixed = jnp.einsum('blgc,gcd->blgd', d, w_pool.astype(jnp.float32)).reshape(B, L, POOL_WIDTH)
    mixed = mixed * pool_scale.astype(jnp.float32)
    return mixed.astype(u.dtype), ext[:, -POOL_HIST:]


def gla_block(S, q, k, v, a):
    C = q.shape[2]
    b = jnp.cumsum(a, axis=2)
    mask = jnp.tril(jnp.ones((C, C), dtype=bool))
    diff = b[:, :, :, None, :] - b[:, :, None, :, :]
    decay = jnp.exp(jnp.where(mask[None, None, :, :, None], diff, -jnp.inf))
    scores = jnp.einsum('bhic,bhijc,bhjc->bhij', q, decay, k)
    o = jnp.einsum('bhij,bhjv->bhiv', scores, v) + jnp.einsum('bhic,bhcv->bhiv', q * jnp.exp(b), S)
    b_last = b[:, :, -1:, :]
    S_new = jnp.exp(b_last[:, :, 0, :])[..., None] * S + jnp.einsum('bhjc,bhjv->bhcv', k * jnp.exp(b_last - b), v)
    return S_new, o


def gla_mixer(q, k, v, g, alr, S0, w_alpha, b_alpha, gla_norm_g):
    B, L, _ = q.shape
    f32 = jnp.float32
    a = jax.nn.log_sigmoid((alr @ w_alpha + b_alpha).astype(f32)) / GLA_GATE_NORM

    def heads(t, d):
        return t.astype(f32).reshape(B, L, GLA_HEADS, d).transpose(0, 2, 1, 3)

    qh = heads(q, GLA_HK) * (GLA_HK ** -0.5)
    kh = heads(k, GLA_HK)
    vh = heads(v, GLA_HV)
    ah = heads(a, GLA_HK)
    S0f = S0.astype(f32)
    C = min(L, CHUNK)
    N = L // C
    if N == 1:
        S_new, o = gla_block(S0f, qh, kh, vh, ah)
    else:
        def to_blocks(t):
            return t.reshape(B, GLA_HEADS, N, C, t.shape[-1]).transpose(2, 0, 1, 3, 4)
        S_new, o = lax.scan(lambda S, xs: gla_block(S, *xs), S0f,
                            (to_blocks(qh), to_blocks(kh), to_blocks(vh), to_blocks(ah)))
        o = o.transpose(1, 2, 0, 3, 4).reshape(B, GLA_HEADS, L, GLA_HV)
    o = rmsnorm(o.transpose(0, 2, 1, 3), gla_norm_g)
    o = o.reshape(B, L, GLA_DV) * jax.nn.silu(g.astype(f32))
    return o.astype(q.dtype), S_new.astype(S0.dtype)


def layer(x, c, pool_hist, S0, pos0, w_ada, b_ada, norm1_g, w_in, w_alpha, b_alpha, w_pool,
          pool_scale, gla_norm_g, w_pa, w_pb, w_out, norm2_g, w_ff1, w_ff2):
    mod = jax.nn.silu(c) @ w_ada + b_ada
    sh1, sc1, gt1, sh2, sc2, gt2 = jnp.split(mod, 6, axis=-1)
    h = modulate(rmsnorm(x, norm1_g), sh1, sc1)
    proj = h @ w_in
    u_pool, q, k, v, g, alr, ga, gb = jnp.split(proj, IN_OFFSETS, axis=-1)
    a_out, new_hist = pool_mixer(u_pool, pool_hist, pos0, w_pool, pool_scale)
    b_out, S_new = gla_mixer(q, k, v, g, alr, S0, w_alpha, b_alpha, gla_norm_g)
    merged = jax.nn.sigmoid(ga) * (a_out @ w_pa) + jax.nn.sigmoid(gb) * (b_out @ w_pb)
    x = x + gt1[:, None, :] * (merged @ w_out)
    h2 = modulate(rmsnorm(x, norm2_g), sh2, sc2)
    ff = jnp.square(jax.nn.relu(h2 @ w_ff1)) @ w_ff2
    x = x + gt2[:, None, :] * ff
    return x, new_hist, S_new


def setup_inputs(seed: int = 0) -> dict:
    key = jax.random.key(seed)
    ks = jax.random.split(key, 24)
    f32 = jnp.float32
    nrm = lambda k, shape, s: (jax.random.normal(k, shape, f32) * s)
    L_ = DEPTH
    return {
        "x_prompt": nrm(ks[0], (BATCH, SEQ, D_MODEL), 1.0),
        "x_sample": nrm(ks[1], (DEC_BATCH, DEC_SEQ, D_MODEL), 1.0),
        "c_prompt": nrm(ks[2], (BATCH, D_MODEL), 1.0),
        "c_sample": nrm(ks[3], (DEC_BATCH, D_MODEL), 1.0),
        "state_gla": nrm(ks[4], (L_, DEC_BATCH, GLA_HEADS, GLA_HK, GLA_HV), 1.0),
        "cache_pool": nrm(ks[5], (L_, DEC_BATCH, POOL_HIST, POOL_WIDTH), 1.0),
        "w_ada": nrm(ks[6], (L_, D_MODEL, 6 * D_MODEL), 0.5 * D_MODEL ** -0.5),
        "b_ada": nrm(ks[7], (L_, 6 * D_MODEL), 0.02),
        "norm1_g": 1.0 + nrm(ks[8], (L_, D_MODEL), 0.05),
        "w_in": nrm(ks[9], (L_, D_MODEL, IN_WIDTH), D_MODEL ** -0.5),
        "w_alpha": nrm(ks[10], (L_, GLA_LOWRANK, GLA_DK), GLA_LOWRANK ** -0.5),
        "b_alpha": nrm(ks[11], (L_, GLA_DK), 0.1),
        "w_pool": nrm(ks[12], (L_, POOL_GROUPS, POOL_GD, POOL_GD), POOL_GD ** -0.5),
        "pool_scale": 1.0 + nrm(ks[13], (L_, POOL_WIDTH), 0.1),
        "gla_norm_g": 1.0 + nrm(ks[14], (L_, GLA_HV), 0.05),
        "w_pa": nrm(ks[15], (L_, POOL_WIDTH, D_MODEL), POOL_WIDTH ** -0.5),
        "w_pb": nrm(ks[16], (L_, GLA_DV, D_MODEL), GLA_DV ** -0.5),
        "w_out": nrm(ks[17], (L_, D_MODEL, D_MODEL), D_MODEL ** -0.5),
        "norm2_g": 1.0 + nrm(ks[18], (L_, D_MODEL), 0.05),
        "w_ff1": nrm(ks[19], (L_, D_MODEL, D_FF), D_MODEL ** -0.5),
        "w_ff2": nrm(ks[20], (L_, D_FF, D_MODEL), D_FF ** -0.5),
        "final_g": 1.0 + nrm(ks[21], (D_MODEL,), 0.05),
    }


def reference(x_prompt, x_sample, c_prompt, c_sample, state_gla, cache_pool, w_ada, b_ada,
              norm1_g, w_in, w_alpha, b_alpha, w_pool, pool_scale, gla_norm_g, w_pa, w_pb,
              w_out, norm2_g, w_ff1, w_ff2, final_g):
    yp, ys = x_prompt, x_sample
    sp_list, hp_list, ss_list, hs_list = [], [], [], []
    for l in range(DEPTH):
        params = (w_ada[l], b_ada[l], norm1_g[l], w_in[l], w_alpha[l], b_alpha[l], w_pool[l],
                  pool_scale[l], gla_norm_g[l], w_pa[l], w_pb[l], w_out[l], norm2_g[l],
                  w_ff1[l], w_ff2[l])
        hist0 = jnp.zeros((BATCH, POOL_HIST, POOL_WIDTH), x_prompt.dtype)
        S0 = jnp.zeros((BATCH, GLA_HEADS, GLA_HK, GLA_HV), state_gla.dtype)
        yp, hp, sp = layer(yp, c_prompt, hist0, S0, 0, *params)
        ys, hs, ss = layer(ys, c_sample, cache_pool[l], state_gla[l], PAST_LEN, *params)
        sp_list.append(sp)
        hp_list.append(hp)
        ss_list.append(ss)
        hs_list.append(hs)
    y_prompt = rmsnorm(yp, final_g)
    y_sample = rmsnorm(ys, final_g)
    state_gla_prompt = jnp.stack(sp_list, axis=0)
    cache_pool_prompt = jnp.stack(hp_list, axis=0)
    state_gla_sample = jnp.stack(ss_list, axis=0)
    cache_pool_sample = jnp.stack(hs_list, axis=0)
    return (y_prompt, y_sample, state_gla_prompt, cache_pool_prompt, state_gla_sample, cache_pool_sample)
```

```python
import functools

import jax
import jax.numpy as jnp
from jax import lax
from jax.experimental import pallas as pl
from jax.experimental.pallas import tpu as pltpu

D_MODEL = 1024
PAST_LEN = 4096
POOL_WIDTH = 512
POOL_WINDOWS = (2, 4, 8, 16)
POOL_GD = 128
POOL_HIST = 15
GLA_HEADS = 4
GLA_DK = 512
GLA_DV = 1024
GLA_HK = 128
GLA_HV = 256
GLA_LOWRANK = 16
GLA_GATE_NORM = 16.0
D_FF = 4096
EPS = 1e-6
IN_SIZES = (POOL_WIDTH, GLA_DK, GLA_DK, GLA_DV, GLA_DV, GLA_LOWRANK, D_MODEL, D_MODEL)

SUB = 16
CHUNK = 64
HIST_BASE = 32
TL_MIX = 256
TL_FFN = 512
FF_CHUNK = 1024
ADA_BLOCK = 1536
VMEM_LIMIT = 48 * 1024 * 1024

F32 = jnp.float32
BF16 = jnp.bfloat16


def _bdot(a, b):
    return jnp.dot(a, b, preferred_element_type=F32)


def _rms(xf):
    return xf * lax.rsqrt(jnp.mean(xf * xf, axis=-1, keepdims=True) + EPS)


def _sigmoid(x):
    return 1.0 / (1.0 + jnp.exp(-x))


def _log_sigmoid(z):
    return jnp.minimum(z, 0.0) - jnp.log1p(jnp.exp(-jnp.abs(z)))


def _cumsum_rows(a, period):
    n = a.shape[0]
    ri = lax.broadcasted_iota(jnp.int32, (n, n), 0)
    ci = lax.broadcasted_iota(jnp.int32, (n, n), 1)
    shift = period.bit_length() - 1
    same = jnp.right_shift(ri, shift) == jnp.right_shift(ci, shift)
    tri = jnp.where((ci <= ri) & same, 1.0, 0.0).astype(BF16)
    hi = a.astype(BF16)
    r1 = a - hi.astype(F32)
    mid = r1.astype(BF16)
    lo = (r1 - mid.astype(F32)).astype(BF16)
    return _bdot(tri, hi) + _bdot(tri, mid) + _bdot(tri, lo)


def _gla_block(q, k, b, v, s, nsub):
    L = SUB * nsub
    blast = b[L - 1:L]
    qt = (q * jnp.exp(b)).astype(BF16)
    kt = (k * jnp.exp(blast - b)).astype(BF16)

    lane = lax.broadcasted_iota(jnp.int32, (SUB, L), 1)
    row = lax.broadcasted_iota(jnp.int32, (SUB, L), 0)
    qs = [q[SUB * i:SUB * (i + 1)] for i in range(nsub)]
    ks = [k[SUB * i:SUB * (i + 1)] for i in range(nsub)]
    bs = [b[SUB * i:SUB * (i + 1)] for i in range(nsub)]

    diag = []
    for i in range(nsub):
        acc = jnp.zeros((SUB, L), F32)
        for j in range(SUB):
            dec = jnp.exp(jnp.minimum(bs[i] - bs[i][j:j + 1], 0.0))
            col = jnp.sum(qs[i] * dec * ks[i][j:j + 1], axis=1, keepdims=True)
            acc = jnp.where(lane == SUB * i + j, col, acc)
        diag.append(jnp.where(lane <= row + SUB * i, acc, 0.0))
    p = diag[0] if nsub == 1 else jnp.concatenate(diag, axis=0)

    if nsub > 1:
        zero = jnp.zeros((SUB, GLA_HK), F32)
        lhs, rhs = [], []
        for j in range(nsub - 1):
            bend = bs[j][SUB - 1:SUB]
            lrows = [zero if i <= j else qs[i] * jnp.exp(bs[i] - bend) for i in range(nsub)]
            rrows = [ks[j] * jnp.exp(bend - bs[j]) if i == j else zero for i in range(nsub)]
            lhs.append(jnp.concatenate(lrows, axis=0).astype(BF16))
            rhs.append(jnp.concatenate(rrows, axis=0).astype(BF16))
        lhs = jnp.concatenate(lhs, axis=1)
        rhs = jnp.concatenate(rhs, axis=1)
        p = p + lax.dot_general(lhs, rhs, (((1,), (1,)), ((), ())), preferred_element_type=F32)

    o = _bdot(qt, s.astype(BF16)) + _bdot(p.astype(BF16), v)

    ri = lax.broadcasted_iota(jnp.int32, (GLA_HK, GLA_HK), 0)
    ci = lax.broadcasted_iota(jnp.int32, (GLA_HK, GLA_HK), 1)
    erow = jnp.broadcast_to(jnp.exp(blast), (GLA_HK, GLA_HK))
    ecol = jnp.sum(jnp.where(ri == ci, erow, 0.0), axis=1, keepdims=True)
    s_new = s * ecol + lax.dot_general(kt, v, (((0,), (0,)), ((), ())), preferred_element_type=F32)
    return o, s_new


def _gla_heads(r0, L, nsub, q_s, k_s, b_s, v_s, sg_s, gng, bo_s, get_state, put_state):
    for h in range(GLA_HEADS):
        ksl = slice(h * GLA_HK, (h + 1) * GLA_HK)
        vsl = slice(h * GLA_HV, (h + 1) * GLA_HV)
        rows = pl.ds(r0, L)
        o, s_new = _gla_block(q_s[rows, ksl], k_s[rows, ksl], b_s[rows, ksl], v_s[rows, vsl],
                              get_state(h), nsub)
        put_state(h, s_new)
        o = _rms(o) * gng
        bo_s[rows, vsl] = (o * sg_s[rows, vsl]).astype(BF16)


def _pool_delta(u_b, s2_b, s4_b, s8_b, n, pos0):
    r = HIST_BASE + n
    s2_b[16:r, :] = u_b[16:r, :] + u_b[15:r - 1, :]
    s4_b[16:r, :] = s2_b[16:r, 128:512] + s2_b[14:r - 2, 128:512]
    s8_b[24:r, :] = s4_b[24:r, 128:384] + s4_b[20:r - 4, 128:384]
    s16 = s8_b[HIST_BASE:r, 128:256] + s8_b[HIST_BASE - 8:r - 8, 128:256]
    sums = (s2_b[HIST_BASE:r, 0:128], s4_b[HIST_BASE:r, 0:128], s8_b[HIST_BASE:r, 0:128], s16)
    pos1 = pos0 + lax.broadcasted_iota(jnp.int32, (n, 1), 0) + 1
    out = []
    for gi, w in enumerate(POOL_WINDOWS):
        cnt = jnp.minimum(pos1, w).astype(F32)
        out.append(sums[gi] / cnt - u_b[HIST_BASE:r, gi * POOL_GD:(gi + 1) * POOL_GD])
    return out


def _pool_mix(d, wpool, pscale):
    mixed = [_bdot(d[gi].astype(BF16), wpool[gi]) for gi in range(len(POOL_WINDOWS))]
    return (jnp.concatenate(mixed, axis=1) * pscale).astype(BF16)


def _project(x, mod_ref, g1, wu, wq, wk, wv, wg, walr, walpha, balpha,
             h_s, q_s, k_s, v_s, sg_s, a_s):
    h = (_rms(x) * g1[...] * (1.0 + mod_ref[1]) + mod_ref[0]).astype(BF16)
    h_s[...] = h
    q_s[...] = _bdot(h, wq[...]) * (GLA_HK ** -0.5)
    k_s[...] = _bdot(h, wk[...])
    v_s[...] = _bdot(h, wv[...]).astype(BF16)
    g = _bdot(h, wg[...])
    sg_s[...] = g * _sigmoid(g)
    alr = _bdot(h, walr[...])
    z = _bdot(alr.astype(BF16), walpha[...]) + balpha[...]
    a_s[...] = _log_sigmoid(z) / GLA_GATE_NORM
    return _bdot(h, wu[...])


def _merge(x, mod_ref, h_s, aout, bo_s, wga, wgb, wpa, wpb, wout):
    ga = _bdot(h_s[...], wga[...])
    merged = _sigmoid(ga) * _bdot(aout, wpa[...])
    gb = _bdot(h_s[...], wgb[...])
    merged = merged + _sigmoid(gb) * _bdot(bo_s[...], wpb[...])
    y = _bdot(merged.astype(BF16), wout[...])
    return x + mod_ref[2] * y


def _adaln_kernel(c_ref, w_ref, b_ref, o_ref):
    c = c_ref[...]
    sc = (c * _sigmoid(c)).astype(BF16)
    o_ref[...] = _bdot(sc, w_ref[...].astype(BF16)) + b_ref[...]


def _prompt_mixer_kernel(x_ref, mod_ref, g1, wu, wq, wk, wv, wg, walr, wga, wgb, walpha, balpha,
                         wpool, pscale, gng, wpa, wpb, wout,
                         x1_ref, st_ref, hist_ref,
                         u_b, s2_b, s4_b, s8_b, h_s, q_s, k_s, v_s, sg_s, b_s, bo_s):
    t = pl.program_id(0)
    tl = x_ref.shape[0]

    @pl.when(t == 0)
    def _():
        st_ref[...] = jnp.zeros_like(st_ref)
        u_b[0:HIST_BASE, :] = jnp.zeros((HIST_BASE, POOL_WIDTH), F32)
        s2_b[0:16, :] = jnp.zeros((16, POOL_WIDTH), F32)

    x = x_ref[...]
    u = _project(x, mod_ref, g1, wu, wq, wk, wv, wg, walr, walpha, balpha,
                 h_s, q_s, k_s, v_s, sg_s, b_s)
    u_b[HIST_BASE:HIST_BASE + tl, :] = u
    for c in range(tl // CHUNK):
        rows = slice(c * CHUNK, (c + 1) * CHUNK)
        b_s[rows, :] = _cumsum_rows(b_s[rows, :], CHUNK)

    d = _pool_delta(u_b, s2_b, s4_b, s8_b, tl, t * tl)
    aout = _pool_mix(d, wpool, pscale[...])
    tail = u_b[16 + tl:HIST_BASE + tl, :]
    u_b[16:HIST_BASE, :] = tail
    hist_ref[...] = tail

    def get_state(h):
        return st_ref[h]

    def put_state(h, s_new):
        st_ref[h] = s_new

    def chunk_body(c, carry):
        r0 = pl.multiple_of(c * CHUNK, CHUNK)
        _gla_heads(r0, CHUNK, CHUNK // SUB, q_s, k_s, b_s, v_s, sg_s, gng[...], bo_s,
                   get_state, put_state)
        return carry

    lax.fori_loop(0, tl // CHUNK, chunk_body, 0)

    x1_ref[...] = _merge(x, mod_ref, h_s, aout, bo_s, wga, wgb, wpa, wpb, wout)


def _sample_mixer_kernel(seq, x_ref, mod_ref, s0_ref, cache_ref, g1, wu, wq, wk, wv, wg, walr, wga, wgb,
                         walpha, balpha, wpool, pscale, gng, wpa, wpb, wout,
                         x1_ref, st_ref, hist_ref,
                         u_b, s2_b, s4_b, s8_b, h_s, q_s, k_s, v_s, sg_s, b_s, bo_s, u_s, ao_s):
    s = pl.program_id(0)
    n_tok = x_ref.shape[0]

    @pl.when(s == 0)
    def _():
        u_b[0:HIST_BASE, :] = jnp.zeros((HIST_BASE, POOL_WIDTH), F32)
        s2_b[0:16, :] = jnp.zeros((16, POOL_WIDTH), F32)
        u_s[...] = _project(x_ref[...], mod_ref, g1, wu, wq, wk, wv, wg, walr, walpha, balpha,
                            h_s, q_s, k_s, v_s, sg_s, b_s)
        for c in range(n_tok // CHUNK):
            rows = slice(c * CHUNK, (c + 1) * CHUNK)
            b_s[rows, :] = _cumsum_rows(b_s[rows, :], seq)

    r0 = pl.multiple_of(s * seq, seq)
    u_b[HIST_BASE - POOL_HIST:HIST_BASE, :] = cache_ref[0]
    u_b[HIST_BASE:HIST_BASE + seq, :] = u_s[pl.ds(r0, seq), :]
    d = _pool_delta(u_b, s2_b, s4_b, s8_b, seq, PAST_LEN)
    ao_s[pl.ds(r0, seq), :] = _pool_mix(d, wpool, pscale[...])
    hist_ref[0] = u_b[HIST_BASE + seq - POOL_HIST:HIST_BASE + seq, :]

    def get_state(h):
        return s0_ref[0, h]

    def put_state(h, s_new):
        st_ref[0, h] = s_new

    _gla_heads(r0, seq, seq // SUB, q_s, k_s, b_s, v_s, sg_s, gng[...], bo_s, get_state, put_state)

    @pl.when(s == pl.num_programs(0) - 1)
    def _():
        x1_ref[...] = _merge(x_ref[...], mod_ref, h_s, ao_s[...], bo_s, wga, wgb, wpa, wpb, wout)


def _ffn_kernel(x_ref, mod_ref, g2, w1, w2, gf, o_ref):
    x = x_ref[...]
    h2 = (_rms(x) * g2[...] * (1.0 + mod_ref[1]) + mod_ref[0]).astype(BF16)
    acc = jnp.zeros(x.shape, F32)
    for c in range(D_FF // FF_CHUNK):
        sl = slice(c * FF_CHUNK, (c + 1) * FF_CHUNK)
        y = jnp.maximum(_bdot(h2, w1[:, sl]), 0.0)
        acc = acc + _bdot((y * y).astype(BF16), w2[sl, :])
    x2 = x + mod_ref[2] * acc
    o_ref[...] = _rms(x2) * gf[...]


def _const_spec(shape):
    nd = len(shape)
    return pl.BlockSpec(shape, lambda *_: (0,) * nd, pipeline_mode=pl.Buffered(1))


def _params():
    return pltpu.CompilerParams(dimension_semantics=("arbitrary",), vmem_limit_bytes=VMEM_LIMIT)


def _mixer_scratch(rows, hist_rows):
    return [
        pltpu.VMEM((hist_rows, POOL_WIDTH), F32),
        pltpu.VMEM((hist_rows, POOL_WIDTH), F32),
        pltpu.VMEM((hist_rows, POOL_WIDTH - 128), F32),
        pltpu.VMEM((hist_rows, POOL_WIDTH - 256), F32),
        pltpu.VMEM((rows, D_MODEL), BF16),
        pltpu.VMEM((rows, GLA_DK), F32),
        pltpu.VMEM((rows, GLA_DK), F32),
        pltpu.VMEM((rows, GLA_DV), BF16),
        pltpu.VMEM((rows, GLA_DV), F32),
        pltpu.VMEM((rows, GLA_DK), F32),
        pltpu.VMEM((rows, GLA_DV), BF16),
    ]


def _adaln(c_all, w_ada, b_ada):
    rows = c_all.shape[0]
    n = w_ada.shape[1]
    return pl.pallas_call(
        _adaln_kernel,
        grid=(n // ADA_BLOCK,),
        in_specs=[pl.BlockSpec((rows, D_MODEL), lambda j: (0, 0)),
                  pl.BlockSpec((D_MODEL, ADA_BLOCK), lambda j: (0, j)),
                  pl.BlockSpec((1, ADA_BLOCK), lambda j: (0, j))],
        out_specs=pl.BlockSpec((rows, ADA_BLOCK), lambda j: (0, j)),
        out_shape=jax.ShapeDtypeStruct((rows, n), F32),
        compiler_params=_params(),
        name="adaln_mod",
    )(c_all, w_ada, b_ada)


def _prompt_mixer(x, mod, weights):
    n_tok = x.shape[0]
    tl = TL_MIX
    w_specs = [_const_spec(w.shape) for w in weights]
    return pl.pallas_call(
        _prompt_mixer_kernel,
        grid=(n_tok // tl,),
        in_specs=[pl.BlockSpec((tl, D_MODEL), lambda t: (t, 0)), _const_spec(mod.shape)] + w_specs,
        out_specs=[pl.BlockSpec((tl, D_MODEL), lambda t: (t, 0)),
                   pl.BlockSpec((GLA_HEADS, GLA_HK, GLA_HV), lambda t: (0, 0, 0)),
                   pl.BlockSpec((16, POOL_WIDTH), lambda t: (0, 0))],
        out_shape=[jax.ShapeDtypeStruct((n_tok, D_MODEL), F32),
                   jax.ShapeDtypeStruct((GLA_HEADS, GLA_HK, GLA_HV), F32),
                   jax.ShapeDtypeStruct((16, POOL_WIDTH), F32)],
        scratch_shapes=_mixer_scratch(tl, HIST_BASE + tl),
        compiler_params=_params(),
        name="prompt_mixer",
    )(x, mod, *weights)


def _sample_mixer(x, mod, s0, cache, weights):
    n_tok = x.shape[0]
    n_seq = s0.shape[0]
    seq = n_tok // n_seq
    w_specs = [_const_spec(w.shape) for w in weights]
    st_spec = pl.BlockSpec((1, GLA_HEADS, GLA_HK, GLA_HV), lambda s: (s, 0, 0, 0))
    hist_spec = pl.BlockSpec((1, POOL_HIST, POOL_WIDTH), lambda s: (s, 0, 0))
    return pl.pallas_call(
        functools.partial(_sample_mixer_kernel, seq),
        grid=(n_seq,),
        in_specs=[_const_spec(x.shape), _const_spec(mod.shape), st_spec, hist_spec] + w_specs,
        out_specs=[pl.BlockSpec((n_tok, D_MODEL), lambda s: (0, 0)), st_spec, hist_spec],
        out_shape=[jax.ShapeDtypeStruct((n_tok, D_MODEL), F32),
                   jax.ShapeDtypeStruct(s0.shape, F32),
                   jax.ShapeDtypeStruct(cache.shape, F32)],
        scratch_shapes=_mixer_scratch(n_tok, HIST_BASE + seq) + [
            pltpu.VMEM((n_tok, POOL_WIDTH), F32),
            pltpu.VMEM((n_tok, POOL_WIDTH), BF16),
        ],
        compiler_params=_params(),
        name="sample_mixer",
    )(x, mod, s0, cache, *weights)


def _ffn_final(x, mod, g2, w1, w2, gf):
    n_tok = x.shape[0]
    tl = min(TL_FFN, n_tok)
    mod_rows = mod.shape[1]
    if mod_rows == 1:
        mod_spec = _const_spec(mod.shape)
    else:
        mod_spec = pl.BlockSpec((3, tl, D_MODEL), lambda t: (0, t, 0))
    return pl.pallas_call(
        _ffn_kernel,
        grid=(n_tok // tl,),
        in_specs=[pl.BlockSpec((tl, D_MODEL), lambda t: (t, 0)), mod_spec,
                  _const_spec(g2.shape), _const_spec(w1.shape), _const_spec(w2.shape),
                  _const_spec(gf.shape)],
        out_specs=pl.BlockSpec((tl, D_MODEL), lambda t: (t, 0)),
        out_shape=jax.ShapeDtypeStruct((n_tok, D_MODEL), F32),
        compiler_params=_params(),
        name="ffn_final",
    )(x, mod, g2, w1, w2, gf)


def kernel(x_prompt, x_sample, c_prompt, c_sample, state_gla, cache_pool, w_ada, b_ada, norm1_g,
           w_in, w_alpha, b_alpha, w_pool, pool_scale, gla_norm_g, w_pa, w_pb, w_out, norm2_g,
           w_ff1, w_ff2, final_g):
    n_batch, n_seq_p, _ = x_prompt.shape
    n_dec, n_seq_s, _ = x_sample.shape
    assert n_batch == 1 and w_ada.shape[0] == 1

    n_c = n_batch + n_dec
    pad = (-n_c) % 8
    c_all = jnp.concatenate([c_prompt, c_sample, jnp.zeros((pad, D_MODEL), F32)], axis=0)
    mod = _adaln(c_all, w_ada[0], b_ada)
    mod = mod.reshape(n_c + pad, 6, D_MODEL).transpose(1, 0, 2)
    mod_p = mod[:, 0:1]
    mod_s = jnp.repeat(mod[:, n_batch:n_c], n_seq_s, axis=1)

    offs = [0]
    for sz in IN_SIZES:
        offs.append(offs[-1] + sz)
    w_in_b = w_in[0].astype(BF16)
    wu, wq, wk, wv, wg, walr, wga, wgb = [w_in_b[:, offs[i]:offs[i + 1]] for i in range(8)]
    weights = (norm1_g, wu, wq, wk, wv, wg, walr, wga, wgb,
               w_alpha[0].astype(BF16), b_alpha, w_pool[0].astype(BF16), pool_scale, gla_norm_g,
               w_pa[0].astype(BF16), w_pb[0].astype(BF16), w_out[0].astype(BF16))

    x1_p, st_p, hist_p = _prompt_mixer(x_prompt[0], mod_p[0:3], weights)
    x1_s, st_s, hist_s = _sample_mixer(x_sample.reshape(n_dec * n_seq_s, D_MODEL), mod_s[0:3],
                                       state_gla[0], cache_pool[0], weights)

    w1 = w_ff1[0].astype(BF16)
    w2 = w_ff2[0].astype(BF16)
    gf = final_g.reshape(1, D_MODEL)
    y_p = _ffn_final(x1_p, mod_p[3:6], norm2_g, w1, w2, gf)
    y_s = _ffn_final(x1_s, mod_s[3:6], norm2_g, w1, w2, gf)

    return (y_p[None], y_s.reshape(n_dec, n_seq_s, D_MODEL), st_p[None, None],
            hist_p[None, None, 1:], st_s[None], hist_s[None])
```

```python
import functools

import jax
import jax.numpy as jnp
from jax import lax
from jax.experimental import pallas as pl
from jax.experimental.pallas import tpu as pltpu

D_MODEL = 1024
PAST_LEN = 4096
POOL_WIDTH = 512
POOL_WINDOWS = (2, 4, 8, 16)
POOL_GD = 128
POOL_HIST = 15
GLA_HEADS = 4
GLA_DK = 512
GLA_DV = 1024
GLA_HK = 128
GLA_HV = 256
GLA_LOWRANK = 16
GLA_GATE_NORM = 16.0
D_FF = 4096
EPS = 1e-6
LOG2_E = 1.4426950408889634
LANES = 128
IN_SIZES = (POOL_WIDTH, GLA_DK, GLA_DK, GLA_DV, GLA_DV, GLA_LOWRANK, D_MODEL, D_MODEL)

SUB = 8
CHUNK = 64
HIST_BASE = 32
TL_MIX = 256
TL_FFN = 512
FF_CHUNK = 1024
ADA_BLOCK = 1536
VMEM_LIMIT = 48 * 1024 * 1024

F32 = jnp.float32
BF16 = jnp.bfloat16


def _bdot(a, b):
    return jnp.dot(a, b, preferred_element_type=F32)


def _rms(xf):
    return xf * lax.rsqrt(jnp.mean(xf * xf, axis=-1, keepdims=True) + EPS)


def _sigmoid(x):
    return 0.5 * jnp.tanh(0.5 * x) + 0.5


def _log_sigmoid(z):
    return jnp.minimum(z, 0.0) - jnp.log(1.0 + jnp.exp(-jnp.abs(z)))


def _cumsum_rows(a, period):
    n = a.shape[0]
    ri = lax.broadcasted_iota(jnp.int32, (n, n), 0)
    ci = lax.broadcasted_iota(jnp.int32, (n, n), 1)
    shift = period.bit_length() - 1
    same = jnp.right_shift(ri, shift) == jnp.right_shift(ci, shift)
    tri = jnp.where((ci <= ri) & same, 1.0, 0.0).astype(BF16)
    hi = a.astype(BF16)
    r1 = a - hi.astype(F32)
    mid = r1.astype(BF16)
    lo = (r1 - mid.astype(F32)).astype(BF16)
    return _bdot(tri, hi) + _bdot(tri, mid) + _bdot(tri, lo)


def _gla_block(q, k, b, k_row, b_row, v, s, nsub):
    L = SUB * nsub
    blast = b_row(L - 1)
    qt = (q * jnp.exp2(b)).astype(BF16)
    kt = (k * jnp.exp2(blast - b)).astype(BF16)

    lane = lax.broadcasted_iota(jnp.int32, (SUB, L), 1)
    row = lax.broadcasted_iota(jnp.int32, (SUB, L), 0)
    qs = [q[SUB * i:SUB * (i + 1)] for i in range(nsub)]
    ks = [k[SUB * i:SUB * (i + 1)] for i in range(nsub)]
    bs = [b[SUB * i:SUB * (i + 1)] for i in range(nsub)]

    diag = []
    for i in range(nsub):
        acc = jnp.zeros((SUB, L), F32)
        for j in range(SUB):
            r = SUB * i + j
            dec = jnp.exp2(bs[i] - b_row(r))
            col = jnp.sum(qs[i] * dec * k_row(r), axis=1, keepdims=True)
            acc = jnp.where(lane == r, col, acc)
        diag.append(jnp.where(lane <= row + SUB * i, acc, 0.0))
    p = diag[0] if nsub == 1 else jnp.concatenate(diag, axis=0)

    if nsub > 1:
        zero = jnp.zeros((SUB, GLA_HK), F32)
        lhs, rhs = [], []
        for j in range(nsub - 1):
            bend = b_row(SUB * j + SUB - 1)
            lrows = [zero if i <= j else qs[i] * jnp.exp2(bs[i] - bend) for i in range(nsub)]
            rrows = [ks[j] * jnp.exp2(bend - bs[j]) if i == j else zero for i in range(nsub)]
            lhs.append(jnp.concatenate(lrows, axis=0).astype(BF16))
            rhs.append(jnp.concatenate(rrows, axis=0).astype(BF16))
        lhs = jnp.concatenate(lhs, axis=1)
        rhs = jnp.concatenate(rhs, axis=1)
        p = p + lax.dot_general(lhs, rhs, (((1,), (1,)), ((), ())), preferred_element_type=F32)

    o = _bdot(qt, s.astype(BF16)) + _bdot(p.astype(BF16), v)

    ri = lax.broadcasted_iota(jnp.int32, (GLA_HK, GLA_HK), 0)
    ci = lax.broadcasted_iota(jnp.int32, (GLA_HK, GLA_HK), 1)
    erow = jnp.broadcast_to(jnp.exp2(blast), (GLA_HK, GLA_HK))
    ecol = jnp.sum(jnp.where(ri == ci, erow, 0.0), axis=1, keepdims=True)
    s_new = s * ecol + lax.dot_general(kt, v, (((0,), (0,)), ((), ())), preferred_element_type=F32)
    return o, s_new


def _gla_heads(r0, L, nsub, q_s, k_s, b_s, v_s, sg_s, gng, bo_s, get_state, put_state):
    for h in range(GLA_HEADS):
        ksl = slice(h * GLA_HK, (h + 1) * GLA_HK)
        vsl = slice(h * GLA_HV, (h + 1) * GLA_HV)
        rows = pl.ds(r0, L)
        k = k_s[rows, ksl]
        b = b_s[rows, ksl]
        if isinstance(r0, int):
            k_row = lambda r, ksl=ksl: k_s[r0 + r:r0 + r + 1, ksl]
            b_row = lambda r, ksl=ksl: b_s[r0 + r:r0 + r + 1, ksl]
        else:
            k_row = lambda r, k=k: k[r:r + 1]
            b_row = lambda r, b=b: b[r:r + 1]
        o, s_new = _gla_block(q_s[rows, ksl], k, b, k_row, b_row, v_s[rows, vsl].astype(BF16),
                              get_state(h), nsub)
        put_state(h, s_new)
        o = _rms(o) * gng
        bo_s[rows, vsl] = (o * sg_s[rows, vsl]).astype(BF16)


def _pool_delta(u_b, s2_b, s4_b, s8_b, n, pos0):
    r = HIST_BASE + n
    s2_b[16:r, :] = u_b[16:r, :] + u_b[15:r - 1, :]
    s4_b[16:r, :] = s2_b[16:r, 128:512] + s2_b[14:r - 2, 128:512]
    s8_b[24:r, :] = s4_b[24:r, 128:384] + s4_b[20:r - 4, 128:384]
    s16 = s8_b[HIST_BASE:r, 128:256] + s8_b[HIST_BASE - 8:r - 8, 128:256]
    sums = (s2_b[HIST_BASE:r, 0:128], s4_b[HIST_BASE:r, 0:128], s8_b[HIST_BASE:r, 0:128], s16)
    pos1 = pos0 + lax.broadcasted_iota(jnp.int32, (n, 1), 0) + 1
    out = []
    for gi, w in enumerate(POOL_WINDOWS):
        cnt = jnp.minimum(pos1, w).astype(F32)
        out.append(sums[gi] / cnt - u_b[HIST_BASE:r, gi * POOL_GD:(gi + 1) * POOL_GD])
    return out


def _pool_mix(d, wpool, pscale):
    mixed = [_bdot(d[gi].astype(BF16), wpool[gi]) for gi in range(len(POOL_WINDOWS))]
    return (jnp.concatenate(mixed, axis=1) * pscale).astype(BF16)


def _project_gla(x, mod_ref, g1, wq, wk, wv, wg, walr, walpha, balpha,
                 h_s, q_s, k_s, v_s, sg_s, a_s):
    h = (_rms(x) * g1[...] * (1.0 + mod_ref[1]) + mod_ref[0]).astype(BF16)
    h_s[...] = h
    q_s[...] = _bdot(h, wq[...]) * (GLA_HK ** -0.5)
    k_s[...] = _bdot(h, wk[...])
    v_s[...] = _bdot(h, wv[...])
    g = _bdot(h, wg[...])
    sg_s[...] = g * _sigmoid(g)
    alr = _bdot(h, walr[...])
    z = _bdot(alr.astype(BF16), walpha[...]) + balpha[...]
    a_s[...] = _log_sigmoid(z) * (LOG2_E / GLA_GATE_NORM)


def _gated_pool(h_s, aout, wga, wpa):
    return _sigmoid(_bdot(h_s[...], wga[...])) * _bdot(aout, wpa[...])


def _gate_b(h_s, wgb):
    return _sigmoid(_bdot(h_s[...], wgb[...]))


def _merge_out(x, mod_ref, merged_a, gate_b, bo_s, wpb, wout):
    merged = merged_a + gate_b * _bdot(bo_s[...], wpb[...])
    y = _bdot(merged.astype(BF16), wout[...])
    return x + mod_ref[2] * y


def _adaln_kernel(c_ref, w_ref, b_ref, o_ref):
    c = c_ref[...]
    sc = (c * _sigmoid(c)).astype(BF16)
    o_ref[...] = _bdot(sc, w_ref[...].astype(BF16)) + b_ref[...]


def _prompt_mixer_kernel(x_ref, mod_ref, g1, wu, wq, wk, wv, wg, walr, wga, wgb, walpha, balpha,
                         wpool, pscale, gng, wpa, wpb, wout,
                         x1_ref, st_ref, hist_ref,
                         u_b, s2_b, s4_b, s8_b, h_s, q_s, k_s, v_s, sg_s, b_s, bo_s, m_s, gb_s):
    t = pl.program_id(0)
    tl = x_ref.shape[0]

    @pl.when(t == 0)
    def _():
        st_ref[...] = jnp.zeros_like(st_ref)
        u_b[0:HIST_BASE, :] = jnp.zeros((HIST_BASE, POOL_WIDTH), F32)
        s2_b[0:16, :] = jnp.zeros((16, POOL_WIDTH), F32)

    x = x_ref[...]
    _project_gla(x, mod_ref, g1, wq, wk, wv, wg, walr, walpha, balpha,
                 h_s, q_s, k_s, v_s, sg_s, b_s)
    for c in range(tl // CHUNK):
        rows = slice(c * CHUNK, (c + 1) * CHUNK)
        b_s[rows, :] = _cumsum_rows(b_s[rows, :], CHUNK)

    def get_state(h):
        return st_ref[h]

    def put_state(h, s_new):
        st_ref[h] = s_new

    def gla_chunk(c):
        _gla_heads(c * CHUNK, CHUNK, CHUNK // SUB, q_s, k_s, b_s, v_s, sg_s, gng[...], bo_s,
                   get_state, put_state)

    assert tl // CHUNK == 4
    gla_chunk(0)
    u_b[HIST_BASE:HIST_BASE + tl, :] = _bdot(h_s[...], wu[...])
    d = _pool_delta(u_b, s2_b, s4_b, s8_b, tl, t * tl)
    aout = _pool_mix(d, wpool, pscale[...])
    tail = u_b[16 + tl:HIST_BASE + tl, :]
    u_b[16:HIST_BASE, :] = tail
    hist_ref[...] = tail
    gla_chunk(1)
    m_s[...] = _gated_pool(h_s, aout, wga, wpa)
    gla_chunk(2)
    gb_s[...] = _gate_b(h_s, wgb)
    gla_chunk(3)
    x1_ref[...] = _merge_out(x, mod_ref, m_s[...], gb_s[...], bo_s, wpb, wout)


def _sample_mixer_kernel(seq, x_ref, mod_ref, s0_ref, cache_ref, g1, wu, wq, wk, wv, wg, walr, wga, wgb,
                         walpha, balpha, wpool, pscale, gng, wpa, wpb, wout,
                         x1_ref, st_ref, hist_ref,
                         u_b, s2_b, s4_b, s8_b, h_s, q_s, k_s, v_s, sg_s, b_s, bo_s, u_s, ao_s):
    s = pl.program_id(0)
    n_tok = x_ref.shape[0]

    @pl.when(s == 0)
    def _():
        u_b[0:HIST_BASE, :] = jnp.zeros((HIST_BASE, POOL_WIDTH), F32)
        s2_b[0:16, :] = jnp.zeros((16, POOL_WIDTH), F32)
        _project_gla(x_ref[...], mod_ref, g1, wq, wk, wv, wg, walr, walpha, balpha,
                     h_s, q_s, k_s, v_s, sg_s, b_s)
        u_s[...] = _bdot(h_s[...], wu[...])
        for c in range(n_tok // CHUNK):
            rows = slice(c * CHUNK, (c + 1) * CHUNK)
            b_s[rows, :] = _cumsum_rows(b_s[rows, :], seq)

    r0 = pl.multiple_of(s * seq, seq)
    u_b[HIST_BASE - POOL_HIST:HIST_BASE, :] = cache_ref[0]
    u_b[HIST_BASE:HIST_BASE + seq, :] = u_s[pl.ds(r0, seq), :]
    d = _pool_delta(u_b, s2_b, s4_b, s8_b, seq, PAST_LEN)
    ao_s[pl.ds(r0, seq), :] = _pool_mix(d, wpool, pscale[...])
    hist_ref[0] = u_b[HIST_BASE + seq - POOL_HIST:HIST_BASE + seq, :]

    def get_state(h):
        return s0_ref[0, h]

    def put_state(h, s_new):
        st_ref[0, h] = s_new

    _gla_heads(r0, seq, seq // SUB, q_s, k_s, b_s, v_s, sg_s, gng[...], bo_s, get_state, put_state)

    @pl.when(s == pl.num_programs(0) - 1)
    def _():
        merged_a = _gated_pool(h_s, ao_s[...], wga, wpa)
        x1_ref[...] = _merge_out(x_ref[...], mod_ref, merged_a, _gate_b(h_s, wgb), bo_s, wpb, wout)


def _ffn_kernel(x_ref, mod_ref, g2, w1, w2, gf, o_ref):
    x = x_ref[...]
    h2 = (_rms(x) * g2[...] * (1.0 + mod_ref[1]) + mod_ref[0]).astype(BF16)
    acc = jnp.zeros(x.shape, F32)
    for c in range(D_FF // FF_CHUNK):
        sl = slice(c * FF_CHUNK, (c + 1) * FF_CHUNK)
        y = jnp.maximum(_bdot(h2, w1[:, sl]), 0.0)
        acc = acc + _bdot((y * y).astype(BF16), w2[sl, :])
    x2 = x + mod_ref[2] * acc
    o_ref[...] = _rms(x2) * gf[...]


def _const_spec(shape):
    nd = len(shape)
    return pl.BlockSpec(shape, lambda *_: (0,) * nd, pipeline_mode=pl.Buffered(1))


def _params():
    return pltpu.CompilerParams(dimension_semantics=("arbitrary",), vmem_limit_bytes=VMEM_LIMIT)


def _mixer_scratch(rows, hist_rows):
    return [
        pltpu.VMEM((hist_rows, POOL_WIDTH), F32),
        pltpu.VMEM((hist_rows, POOL_WIDTH), F32),
        pltpu.VMEM((hist_rows, POOL_WIDTH - 128), F32),
        pltpu.VMEM((hist_rows, POOL_WIDTH - 256), F32),
        pltpu.VMEM((rows, D_MODEL), BF16),
        pltpu.VMEM((rows, GLA_DK), F32),
        pltpu.VMEM((rows, GLA_DK), F32),
        pltpu.VMEM((rows, GLA_DV), F32),
        pltpu.VMEM((rows, GLA_DV), F32),
        pltpu.VMEM((rows, GLA_DK), F32),
        pltpu.VMEM((rows, GLA_DV), BF16),
    ]


def _adaln(c_all, w_ada, b_ada):
    rows = c_all.shape[0]
    n = w_ada.shape[1]
    return pl.pallas_call(
        _adaln_kernel,
        grid=(n // ADA_BLOCK,),
        in_specs=[pl.BlockSpec((rows, D_MODEL), lambda j: (0, 0)),
                  pl.BlockSpec((D_MODEL, ADA_BLOCK), lambda j: (0, j)),
                  pl.BlockSpec((1, ADA_BLOCK), lambda j: (0, j))],
        out_specs=pl.BlockSpec((rows, ADA_BLOCK), lambda j: (0, j)),
        out_shape=jax.ShapeDtypeStruct((rows, n), F32),
        compiler_params=_params(),
        name="adaln_mod",
    )(c_all, w_ada, b_ada)


def _prompt_mixer(x, mod, weights):
    n_tok = x.shape[0]
    tl = TL_MIX
    w_specs = [_const_spec(w.shape) for w in weights]
    return pl.pallas_call(
        _prompt_mixer_kernel,
        grid=(n_tok // tl,),
        in_specs=[pl.BlockSpec((tl, D_MODEL), lambda t: (t, 0)), _const_spec(mod.shape)] + w_specs,
        out_specs=[pl.BlockSpec((tl, D_MODEL), lambda t: (t, 0)),
                   pl.BlockSpec((GLA_HEADS, GLA_HK, GLA_HV), lambda t: (0, 0, 0)),
                   pl.BlockSpec((16, POOL_WIDTH), lambda t: (0, 0))],
        out_shape=[jax.ShapeDtypeStruct((n_tok, D_MODEL), F32),
                   jax.ShapeDtypeStruct((GLA_HEADS, GLA_HK, GLA_HV), F32),
                   jax.ShapeDtypeStruct((16, POOL_WIDTH), F32)],
        scratch_shapes=_mixer_scratch(tl, HIST_BASE + tl) + [
            pltpu.VMEM((tl, D_MODEL), F32),
            pltpu.VMEM((tl, D_MODEL), F32),
        ],
        compiler_params=_params(),
        name="prompt_mixer",
    )(x, mod, *weights)


def _sample_mixer(x, mod, s0, cache, weights):
    n_tok = x.shape[0]
    n_seq = s0.shape[0]
    seq = n_tok // n_seq
    w_specs = [_const_spec(w.shape) for w in weights]
    st_spec = pl.BlockSpec((1, GLA_HEADS, GLA_HK, GLA_HV), lambda s: (s, 0, 0, 0))
    hist_spec = pl.BlockSpec((1, POOL_HIST, POOL_WIDTH), lambda s: (s, 0, 0))
    return pl.pallas_call(
        functools.partial(_sample_mixer_kernel, seq),
        grid=(n_seq,),
        in_specs=[_const_spec(x.shape), _const_spec(mod.shape), st_spec, hist_spec] + w_specs,
        out_specs=[pl.BlockSpec((n_tok, D_MODEL), lambda s: (0, 0)), st_spec, hist_spec],
        out_shape=[jax.ShapeDtypeStruct((n_tok, D_MODEL), F32),
                   jax.ShapeDtypeStruct(s0.shape, F32),
                   jax.ShapeDtypeStruct(cache.shape, F32)],
        scratch_shapes=_mixer_scratch(n_tok, HIST_BASE + seq) + [
            pltpu.VMEM((n_tok, POOL_WIDTH), F32),
            pltpu.VMEM((n_tok, POOL_WIDTH), BF16),
        ],
        compiler_params=_params(),
        name="sample_mixer",
    )(x, mod, s0, cache, *weights)


def _ffn_final(x, mod, g2, w1, w2, gf):
    n_tok = x.shape[0]
    tl = min(TL_FFN, n_tok)
    mod_rows = mod.shape[1]
    if mod_rows == 1:
        mod_spec = _const_spec(mod.shape)
    else:
        mod_spec = pl.BlockSpec((3, tl, D_MODEL), lambda t: (0, t, 0))
    return pl.pallas_call(
        _ffn_kernel,
        grid=(n_tok // tl,),
        in_specs=[pl.BlockSpec((tl, D_MODEL), lambda t: (t, 0)), mod_spec,
                  _const_spec(g2.shape), _const_spec(w1.shape), _const_spec(w2.shape),
                  _const_spec(gf.shape)],
        out_specs=pl.BlockSpec((tl, D_MODEL), lambda t: (t, 0)),
        out_shape=jax.ShapeDtypeStruct((n_tok, D_MODEL), F32),
        compiler_params=_params(),
        name="ffn_final",
    )(x, mod, g2, w1, w2, gf)


def kernel(x_prompt, x_sample, c_prompt, c_sample, state_gla, cache_pool, w_ada, b_ada, norm1_g,
           w_in, w_alpha, b_alpha, w_pool, pool_scale, gla_norm_g, w_pa, w_pb, w_out, norm2_g,
           w_ff1, w_ff2, final_g):
    n_batch, n_seq_p, _ = x_prompt.shape
    n_dec, n_seq_s, _ = x_sample.shape
    assert n_batch == 1 and w_ada.shape[0] == 1

    n_c = n_batch + n_dec
    pad = (-n_c) % 8
    c_all = jnp.concatenate([c_prompt, c_sample, jnp.zeros((pad, D_MODEL), F32)], axis=0)
    mod = _adaln(c_all, w_ada[0], b_ada)
    mod = mod.reshape(n_c + pad, 6, D_MODEL).transpose(1, 0, 2)
    mod_p = mod[:, 0:1]
    mod_s = jnp.repeat(mod[:, n_batch:n_c], n_seq_s, axis=1)

    offs = [0]
    for sz in IN_SIZES:
        offs.append(offs[-1] + sz)
    w_in_b = w_in[0].astype(BF16)
    wu, wq, wk, wv, wg, walr, wga, wgb = [w_in_b[:, offs[i]:offs[i + 1]] for i in range(8)]
    walr = jnp.pad(walr, ((0, 0), (0, LANES - GLA_LOWRANK)))
    walpha = jnp.pad(w_alpha[0].astype(BF16), ((0, LANES - GLA_LOWRANK), (0, 0)))
    weights = (norm1_g, wu, wq, wk, wv, wg, walr, wga, wgb,
               walpha, b_alpha, w_pool[0].astype(BF16), pool_scale, gla_norm_g,
               w_pa[0].astype(BF16), w_pb[0].astype(BF16), w_out[0].astype(BF16))

    x1_p, st_p, hist_p = _prompt_mixer(x_prompt[0], mod_p[0:3], weights)
    x1_s, st_s, hist_s = _sample_mixer(x_sample.reshape(n_dec * n_seq_s, D_MODEL), mod_s[0:3],
                                       state_gla[0], cache_pool[0], weights)

    w1 = w_ff1[0].astype(BF16)
    w2 = w_ff2[0].astype(BF16)
    gf = final_g.reshape(1, D_MODEL)
    y_p = _ffn_final(x1_p, mod_p[3:6], norm2_g, w1, w2, gf)
    y_s = _ffn_final(x1_s, mod_s[3:6], norm2_g, w1, w2, gf)

    return (y_p[None], y_s.reshape(n_dec, n_seq_s, D_MODEL), st_p[None, None],
            hist_p[None, None, 1:], st_s[None], hist_s[None])
```

```python
import jax
import jax.numpy as jnp
from jax import lax
from jax.experimental import pallas as pl
from jax.experimental.pallas import tpu as pltpu

D_MODEL = 1024
PAST_LEN = 4096
POOL_WIDTH = 512
POOL_WINDOWS = (2, 4, 8, 16)
POOL_GD = 128
POOL_HIST = 15
GLA_HEADS = 4
GLA_DK = 512
GLA_DV = 1024
GLA_HK = 128
GLA_HV = 256
GLA_LOWRANK = 16
GLA_GATE_NORM = 16.0
D_FF = 4096
EPS = 1e-6
LOG2_E = 1.4426950408889634
LANES = 128
IN_SIZES = (POOL_WIDTH, GLA_DK, GLA_DK, GLA_DV, GLA_DV, GLA_LOWRANK, D_MODEL, D_MODEL)

SUB = 8
CHUNK = 64
SLAB = 64
HIST_BASE = 32
TL_MIX = 256
TL_FFN = 512
FF_CHUNK = 1024
ADA_BLOCK = 1536
SEQ_GROUP = 4
VMEM_LIMIT = 48 * 1024 * 1024

F32 = jnp.float32
BF16 = jnp.bfloat16


def _bdot(a, b):
    return jnp.dot(a, b, preferred_element_type=F32)


def _rms(xf):
    return xf * lax.rsqrt(jnp.mean(xf * xf, axis=-1, keepdims=True) + EPS)


def _sigmoid(x):
    return 0.5 * jnp.tanh(0.5 * x) + 0.5


def _log_sigmoid(z):
    return jnp.minimum(z, 0.0) - jnp.log(1.0 + jnp.exp(-jnp.abs(z)))


def _rows_per_seq(m, n_rows):
    n_seq = m.shape[0]
    if n_seq == 1:
        return m
    rep = n_rows // n_seq
    return jnp.concatenate([jnp.broadcast_to(m[i:i + 1], (rep, m.shape[1])) for i in range(n_seq)],
                           axis=0)


def _norm_mod(x, g, shift, scale):
    n = x.shape[0]
    return (_rms(x) * g * (1.0 + _rows_per_seq(scale, n)) + _rows_per_seq(shift, n)).astype(BF16)


def _store_blocks(ref, val):
    nblk, rows = ref.shape[0], ref.shape[1]
    for i in range(nblk):
        ref[i] = val[i * rows:(i + 1) * rows].astype(ref.dtype)


def _load_blocks(ref):
    return jnp.concatenate([ref[i] for i in range(ref.shape[0])], axis=0)


def _cumsum_rows(a, period):
    n = a.shape[0]
    ri = lax.broadcasted_iota(jnp.int32, (n, n), 0)
    ci = lax.broadcasted_iota(jnp.int32, (n, n), 1)
    shift = period.bit_length() - 1
    same = jnp.right_shift(ri, shift) == jnp.right_shift(ci, shift)
    tri = jnp.where((ci <= ri) & same, 1.0, 0.0).astype(BF16)
    hi = a.astype(BF16)
    r1 = a - hi.astype(F32)
    mid = r1.astype(BF16)
    lo = (r1 - mid.astype(F32)).astype(BF16)
    return _bdot(tri, hi) + _bdot(tri, mid) + _bdot(tri, lo)


def _gla_block(q, k, b, k_row, b_row, v, s, nsub):
    L = SUB * nsub
    blast = b_row(L - 1)
    qt = (q * jnp.exp2(b)).astype(BF16)
    kt = (k * jnp.exp2(blast - b)).astype(BF16)

    lane = lax.broadcasted_iota(jnp.int32, (SUB, L), 1)
    row = lax.broadcasted_iota(jnp.int32, (SUB, L), 0)
    qs = [q[SUB * i:SUB * (i + 1)] for i in range(nsub)]
    ks = [k[SUB * i:SUB * (i + 1)] for i in range(nsub)]
    bs = [b[SUB * i:SUB * (i + 1)] for i in range(nsub)]

    diag = []
    for i in range(nsub):
        acc = jnp.zeros((SUB, L), F32)
        for j in range(SUB):
            r = SUB * i + j
            dec = jnp.exp2(bs[i] - b_row(r))
            col = jnp.sum(qs[i] * dec * k_row(r), axis=1, keepdims=True)
            acc = jnp.where(lane == r, col, acc)
        diag.append(jnp.where(lane <= row + SUB * i, acc, 0.0))
    p = diag[0] if nsub == 1 else jnp.concatenate(diag, axis=0)

    if nsub > 1:
        zero = jnp.zeros((SUB, GLA_HK), F32)
        lhs, rhs = [], []
        for j in range(nsub - 1):
            bend = b_row(SUB * j + SUB - 1)
            lrows = [zero if i <= j else qs[i] * jnp.exp2(bs[i] - bend) for i in range(nsub)]
            rrows = [ks[j] * jnp.exp2(bend - bs[j]) if i == j else zero for i in range(nsub)]
            lhs.append(jnp.concatenate(lrows, axis=0).astype(BF16))
            rhs.append(jnp.concatenate(rrows, axis=0).astype(BF16))
        lhs = jnp.concatenate(lhs, axis=1)
        rhs = jnp.concatenate(rhs, axis=1)
        p = p + lax.dot_general(lhs, rhs, (((1,), (1,)), ((), ())), preferred_element_type=F32)

    o = _bdot(qt, s.astype(BF16)) + _bdot(p.astype(BF16), v)

    ri = lax.broadcasted_iota(jnp.int32, (GLA_HK, GLA_HK), 0)
    ci = lax.broadcasted_iota(jnp.int32, (GLA_HK, GLA_HK), 1)
    erow = jnp.broadcast_to(jnp.exp2(blast), (GLA_HK, GLA_HK))
    ecol = jnp.sum(jnp.where(ri == ci, erow, 0.0), axis=1, keepdims=True)
    s_new = s * ecol + lax.dot_general(kt, v, (((0,), (0,)), ((), ())), preferred_element_type=F32)
    return o, s_new


def _gla_heads(blk, q_s, k_s, b_s, v_s, sg_s, gng, bo_s, get_state, put_state):
    nsub = q_s.shape[1] // SUB
    for h in range(GLA_HEADS):
        ksl = slice(h * GLA_HK, (h + 1) * GLA_HK)
        vsl = slice(h * GLA_HV, (h + 1) * GLA_HV)
        k_row = lambda r, ksl=ksl: k_s[blk, r:r + 1, ksl]
        b_row = lambda r, ksl=ksl: b_s[blk, r:r + 1, ksl]
        o, s_new = _gla_block(q_s[blk, :, ksl], k_s[blk, :, ksl], b_s[blk, :, ksl], k_row, b_row,
                              v_s[blk, :, vsl].astype(BF16), get_state(h), nsub)
        put_state(h, s_new)
        o = _rms(o) * gng
        bo_s[blk, :, vsl] = (o * sg_s[blk, :, vsl]).astype(BF16)


def _pool_delta(u_b, s2_b, s4_b, s8_b, n, pos0):
    r = HIST_BASE + n
    s2_b[16:r, :] = u_b[16:r, :] + u_b[15:r - 1, :]
    s4_b[16:r, :] = s2_b[16:r, 128:512] + s2_b[14:r - 2, 128:512]
    s8_b[24:r, :] = s4_b[24:r, 128:384] + s4_b[20:r - 4, 128:384]
    s16 = s8_b[HIST_BASE:r, 128:256] + s8_b[HIST_BASE - 8:r - 8, 128:256]
    sums = (s2_b[HIST_BASE:r, 0:128], s4_b[HIST_BASE:r, 0:128], s8_b[HIST_BASE:r, 0:128], s16)
    pos1 = pos0 + lax.broadcasted_iota(jnp.int32, (n, 1), 0) + 1
    out = []
    for gi, w in enumerate(POOL_WINDOWS):
        cnt = jnp.minimum(pos1, w).astype(F32)
        out.append(sums[gi] / cnt - u_b[HIST_BASE:r, gi * POOL_GD:(gi + 1) * POOL_GD])
    return out


def _pool_mix(d, wpool, pscale):
    mixed = [_bdot(d[gi].astype(BF16), wpool[gi]) for gi in range(len(POOL_WINDOWS))]
    return (jnp.concatenate(mixed, axis=1) * pscale).astype(BF16)


def _project_gla(x, mod_ref, g1, wq, wk, wv, wg, walr, walpha, balpha, period,
                 h_s, q_s, k_s, v_s, sg_s, b_s):
    h = _norm_mod(x, g1[...], mod_ref[0], mod_ref[1])
    h_s[...] = h
    _store_blocks(q_s, _bdot(h, wq[...]) * (GLA_HK ** -0.5))
    _store_blocks(k_s, _bdot(h, wk[...]))
    _store_blocks(v_s, _bdot(h, wv[...]))
    g = _bdot(h, wg[...])
    _store_blocks(sg_s, g * _sigmoid(g))
    alr = _bdot(h, walr[...])
    z = _bdot(alr.astype(BF16), walpha[...]) + balpha[...]
    a = _log_sigmoid(z) * (LOG2_E / GLA_GATE_NORM)
    b = [_cumsum_rows(a[r:r + SLAB], period) for r in range(0, a.shape[0], SLAB)]
    _store_blocks(b_s, jnp.concatenate(b, axis=0))


def _gated_pool(h_s, aout, wga, wpa):
    return _sigmoid(_bdot(h_s[...], wga[...])) * _bdot(aout, wpa[...])


def _gate_b(h_s, wgb):
    return _sigmoid(_bdot(h_s[...], wgb[...]))


def _merge_out(x, gate1, merged_a, gate_b, bo, wpb, wout):
    merged = merged_a + gate_b * _bdot(bo, wpb[...])
    y = _bdot(merged.astype(BF16), wout[...])
    return x + _rows_per_seq(gate1, x.shape[0]) * y


def _adaln_kernel(c_ref, w_ref, b_ref, o_ref):
    c = c_ref[...]
    sc = (c * _sigmoid(c)).astype(BF16)
    o_ref[...] = _bdot(sc, w_ref[...].astype(BF16)) + b_ref[...]


def _prompt_mixer_kernel(x_ref, mod_ref, g1, wu, wq, wk, wv, wg, walr, wga, wgb, walpha, balpha,
                         wpool, pscale, gng, wpa, wpb, wout,
                         x1_ref, st_ref, hist_ref,
                         u_b, s2_b, s4_b, s8_b, h_s, q_s, k_s, v_s, sg_s, b_s, bo_s, m_s, gb_s):
    t = pl.program_id(0)
    tl = x_ref.shape[0]
    n_chunks = tl // CHUNK

    @pl.when(t == 0)
    def _():
        st_ref[...] = jnp.zeros_like(st_ref)
        u_b[0:HIST_BASE, :] = jnp.zeros((HIST_BASE, POOL_WIDTH), F32)
        s2_b[0:16, :] = jnp.zeros((16, POOL_WIDTH), F32)

    x = x_ref[...]
    _project_gla(x, mod_ref, g1, wq, wk, wv, wg, walr, walpha, balpha, CHUNK,
                 h_s, q_s, k_s, v_s, sg_s, b_s)

    def get_state(h):
        return st_ref[h]

    def put_state(h, s_new):
        st_ref[h] = s_new

    def gla_chunks(lo, hi):
        for c in range(lo, hi):
            _gla_heads(c, q_s, k_s, b_s, v_s, sg_s, gng[...], bo_s, get_state, put_state)

    q1, q2, q3 = n_chunks // 4, n_chunks // 2, (3 * n_chunks) // 4
    gla_chunks(0, q1)
    u_b[HIST_BASE:HIST_BASE + tl, :] = _bdot(h_s[...], wu[...])
    d = _pool_delta(u_b, s2_b, s4_b, s8_b, tl, t * tl)
    aout = _pool_mix(d, wpool, pscale[...])
    tail = u_b[16 + tl:HIST_BASE + tl, :]
    u_b[16:HIST_BASE, :] = tail
    hist_ref[...] = tail
    gla_chunks(q1, q2)
    m_s[...] = _gated_pool(h_s, aout, wga, wpa)
    gla_chunks(q2, q3)
    gb_s[...] = _gate_b(h_s, wgb)
    gla_chunks(q3, n_chunks)
    x1_ref[...] = _merge_out(x, mod_ref[2], m_s[...], gb_s[...], _load_blocks(bo_s), wpb, wout)


def _sample_mixer_kernel(x_ref, mod_ref, s0_ref, cache_ref, g1, wu, wq, wk, wv, wg, walr, wga, wgb,
                         walpha, balpha, wpool, pscale, gng, wpa, wpb, wout,
                         x1_ref, st_ref, hist_ref,
                         u_b, s2_b, s4_b, s8_b, h_s, q_s, k_s, v_s, sg_s, b_s, bo_s, u_s, ao_s):
    s = pl.program_id(0)
    seq = q_s.shape[1]
    group = s0_ref.shape[0]

    @pl.when(s == 0)
    def _():
        u_b[0:HIST_BASE, :] = jnp.zeros((HIST_BASE, POOL_WIDTH), F32)
        s2_b[0:16, :] = jnp.zeros((16, POOL_WIDTH), F32)
        _project_gla(x_ref[...], mod_ref, g1, wq, wk, wv, wg, walr, walpha, balpha, seq,
                     h_s, q_s, k_s, v_s, sg_s, b_s)
        _store_blocks(u_s, _bdot(h_s[...], wu[...]))

    for i in range(group):
        blk = s * group + i
        u_b[HIST_BASE - POOL_HIST:HIST_BASE, :] = cache_ref[i]
        u_b[HIST_BASE:HIST_BASE + seq, :] = u_s[blk]
        d = _pool_delta(u_b, s2_b, s4_b, s8_b, seq, PAST_LEN)
        ao_s[blk] = _pool_mix(d, wpool, pscale[...])
        hist_ref[i] = u_b[HIST_BASE + seq - POOL_HIST:HIST_BASE + seq, :]

        def get_state(h, i=i):
            return s0_ref[i, h]

        def put_state(h, s_new, i=i):
            st_ref[i, h] = s_new

        _gla_heads(blk, q_s, k_s, b_s, v_s, sg_s, gng[...], bo_s, get_state, put_state)

    @pl.when(s == pl.num_programs(0) - 1)
    def _():
        merged_a = _gated_pool(h_s, _load_blocks(ao_s), wga, wpa)
        x1_ref[...] = _merge_out(x_ref[...], mod_ref[2], merged_a, _gate_b(h_s, wgb),
                                 _load_blocks(bo_s), wpb, wout)


def _ffn_kernel(x_ref, mod_ref, g2, w1, w2, gf, o_ref):
    x = x_ref[...]
    h2 = _norm_mod(x, g2[...], mod_ref[0], mod_ref[1])
    acc = jnp.zeros(x.shape, F32)
    for c in range(D_FF // FF_CHUNK):
        sl = slice(c * FF_CHUNK, (c + 1) * FF_CHUNK)
        y = jnp.maximum(_bdot(h2, w1[:, sl]), 0.0)
        acc = acc + _bdot((y * y).astype(BF16), w2[sl, :])
    x2 = x + _rows_per_seq(mod_ref[2], x.shape[0]) * acc
    o_ref[...] = _rms(x2) * gf[...]


def _const_spec(shape):
    nd = len(shape)
    return pl.BlockSpec(shape, lambda *_: (0,) * nd, pipeline_mode=pl.Buffered(1))


def _params():
    return pltpu.CompilerParams(dimension_semantics=("arbitrary",), vmem_limit_bytes=VMEM_LIMIT)


def _mixer_scratch(nblk, rows, hist_rows):
    return [
        pltpu.VMEM((hist_rows, POOL_WIDTH), F32),
        pltpu.VMEM((hist_rows, POOL_WIDTH), F32),
        pltpu.VMEM((hist_rows, POOL_WIDTH - 128), F32),
        pltpu.VMEM((hist_rows, POOL_WIDTH - 256), F32),
        pltpu.VMEM((nblk * rows, D_MODEL), BF16),
        pltpu.VMEM((nblk, rows, GLA_DK), F32),
        pltpu.VMEM((nblk, rows, GLA_DK), F32),
        pltpu.VMEM((nblk, rows, GLA_DV), F32),
        pltpu.VMEM((nblk, rows, GLA_DV), F32),
        pltpu.VMEM((nblk, rows, GLA_DK), F32),
        pltpu.VMEM((nblk, rows, GLA_DV), BF16),
    ]


def _adaln(c_all, w_ada, b_ada):
    rows = c_all.shape[0]
    n = w_ada.shape[1]
    return pl.pallas_call(
        _adaln_kernel,
        grid=(n // ADA_BLOCK,),
        in_specs=[pl.BlockSpec((rows, D_MODEL), lambda j: (0, 0)),
                  pl.BlockSpec((D_MODEL, ADA_BLOCK), lambda j: (0, j)),
                  pl.BlockSpec((1, ADA_BLOCK), lambda j: (0, j))],
        out_specs=pl.BlockSpec((rows, ADA_BLOCK), lambda j: (0, j)),
        out_shape=jax.ShapeDtypeStruct((rows, n), F32),
        compiler_params=_params(),
        name="adaln_mod",
    )(c_all, w_ada, b_ada)


def _prompt_mixer(x, mod, weights):
    n_tok = x.shape[0]
    tl = TL_MIX
    w_specs = [_const_spec(w.shape) for w in weights]
    return pl.pallas_call(
        _prompt_mixer_kernel,
        grid=(n_tok // tl,),
        in_specs=[pl.BlockSpec((tl, D_MODEL), lambda t: (t, 0)), _const_spec(mod.shape)] + w_specs,
        out_specs=[pl.BlockSpec((tl, D_MODEL), lambda t: (t, 0)),
                   pl.BlockSpec((GLA_HEADS, GLA_HK, GLA_HV), lambda t: (0, 0, 0)),
                   pl.BlockSpec((16, POOL_WIDTH), lambda t: (0, 0))],
        out_shape=[jax.ShapeDtypeStruct((n_tok, D_MODEL), F32),
                   jax.ShapeDtypeStruct((GLA_HEADS, GLA_HK, GLA_HV), F32),
                   jax.ShapeDtypeStruct((16, POOL_WIDTH), F32)],
        scratch_shapes=_mixer_scratch(tl // CHUNK, CHUNK, HIST_BASE + tl) + [
            pltpu.VMEM((tl, D_MODEL), F32),
            pltpu.VMEM((tl, D_MODEL), F32),
        ],
        compiler_params=_params(),
        name="prompt_mixer",
    )(x, mod, *weights)


def _sample_mixer(x, mod, s0, cache, weights):
    n_tok = x.shape[0]
    n_seq = s0.shape[0]
    seq = n_tok // n_seq
    group = SEQ_GROUP
    w_specs = [_const_spec(w.shape) for w in weights]
    st_spec = pl.BlockSpec((group, GLA_HEADS, GLA_HK, GLA_HV), lambda s: (s, 0, 0, 0))
    hist_spec = pl.BlockSpec((group, POOL_HIST, POOL_WIDTH), lambda s: (s, 0, 0))
    return pl.pallas_call(
        _sample_mixer_kernel,
        grid=(n_seq // group,),
        in_specs=[_const_spec(x.shape), _const_spec(mod.shape), st_spec, hist_spec] + w_specs,
        out_specs=[pl.BlockSpec((n_tok, D_MODEL), lambda s: (0, 0)), st_spec, hist_spec],
        out_shape=[jax.ShapeDtypeStruct((n_tok, D_MODEL), F32),
                   jax.ShapeDtypeStruct(s0.shape, F32),
                   jax.ShapeDtypeStruct(cache.shape, F32)],
        scratch_shapes=_mixer_scratch(n_seq, seq, HIST_BASE + seq) + [
            pltpu.VMEM((n_seq, seq, POOL_WIDTH), F32),
            pltpu.VMEM((n_seq, seq, POOL_WIDTH), BF16),
        ],
        compiler_params=_params(),
        name="sample_mixer",
    )(x, mod, s0, cache, *weights)


def _ffn_final(x, mod, g2, w1, w2, gf):
    n_tok = x.shape[0]
    tl = min(TL_FFN, n_tok)
    assert mod.shape[1] == 1 or n_tok == tl
    return pl.pallas_call(
        _ffn_kernel,
        grid=(n_tok // tl,),
        in_specs=[pl.BlockSpec((tl, D_MODEL), lambda t: (t, 0)), _const_spec(mod.shape),
                  _const_spec(g2.shape), _const_spec(w1.shape), _const_spec(w2.shape),
                  _const_spec(gf.shape)],
        out_specs=pl.BlockSpec((tl, D_MODEL), lambda t: (t, 0)),
        out_shape=jax.ShapeDtypeStruct((n_tok, D_MODEL), F32),
        compiler_params=_params(),
        name="ffn_final",
    )(x, mod, g2, w1, w2, gf)


def kernel(x_prompt, x_sample, c_prompt, c_sample, state_gla, cache_pool, w_ada, b_ada, norm1_g,
           w_in, w_alpha, b_alpha, w_pool, pool_scale, gla_norm_g, w_pa, w_pb, w_out, norm2_g,
           w_ff1, w_ff2, final_g):
    n_batch, n_seq_p, _ = x_prompt.shape
    n_dec, n_seq_s, _ = x_sample.shape
    assert n_batch == 1 and w_ada.shape[0] == 1
    assert n_seq_p % TL_MIX == 0 and SLAB % n_seq_s == 0 and n_seq_s % SUB == 0
    assert n_dec % SEQ_GROUP == 0

    n_c = n_batch + n_dec
    pad = (-n_c) % 8
    c_all = jnp.concatenate([c_prompt, c_sample, jnp.zeros((pad, D_MODEL), F32)], axis=0)
    mod = _adaln(c_all, w_ada[0], b_ada)
    mod = mod.reshape(n_c + pad, 6, D_MODEL).transpose(1, 0, 2)
    mod_p = mod[:, 0:1]
    mod_s = mod[:, n_batch:n_c]

    offs = [0]
    for sz in IN_SIZES:
        offs.append(offs[-1] + sz)
    w_in_b = w_in[0].astype(BF16)
    wu, wq, wk, wv, wg, walr, wga, wgb = [w_in_b[:, offs[i]:offs[i + 1]] for i in range(8)]
    walr = jnp.pad(walr, ((0, 0), (0, LANES - GLA_LOWRANK)))
    walpha = jnp.pad(w_alpha[0].astype(BF16), ((0, LANES - GLA_LOWRANK), (0, 0)))
    weights = (norm1_g, wu, wq, wk, wv, wg, walr, wga, wgb,
               walpha, b_alpha, w_pool[0].astype(BF16), pool_scale, gla_norm_g,
               w_pa[0].astype(BF16), w_pb[0].astype(BF16), w_out[0].astype(BF16))

    x1_p, st_p, hist_p = _prompt_mixer(x_prompt[0], mod_p[0:3], weights)
    x1_s, st_s, hist_s = _sample_mixer(x_sample.reshape(n_dec * n_seq_s, D_MODEL), mod_s[0:3],
                                       state_gla[0], cache_pool[0], weights)

    w1 = w_ff1[0].astype(BF16)
    w2 = w_ff2[0].astype(BF16)
    gf = final_g.reshape(1, D_MODEL)
    y_p = _ffn_final(x1_p, mod_p[3:6], norm2_g, w1, w2, gf)
    y_s = _ffn_final(x1_s, mod_s[3:6], norm2_g, w1, w2, gf)

    return (y_p[None], y_s.reshape(n_dec, n_seq_s, D_MODEL), st_p[None, None],
            hist_p[None, None, 1:], st_s[None], hist_s[None])
```

```python
import jax
import jax.numpy as jnp
from jax import lax
from jax.experimental import pallas as pl
from jax.experimental.pallas import tpu as pltpu

D_MODEL = 1024
PAST_LEN = 4096
POOL_WIDTH = 512
POOL_WINDOWS = (2, 4, 8, 16)
POOL_GD = 128
POOL_HIST = 15
GLA_HEADS = 4
GLA_DK = 512
GLA_DV = 1024
GLA_HK = 128
GLA_HV = 256
GLA_LOWRANK = 16
GLA_GATE_NORM = 16.0
D_FF = 4096
EPS = 1e-6
LOG2_E = 1.4426950408889634
LANES = 128
IN_SIZES = (POOL_WIDTH, GLA_DK, GLA_DK, GLA_DV, GLA_DV, GLA_LOWRANK, D_MODEL, D_MODEL)

SUB = 8
CHUNK = 64
SLAB = 64
HIST_BASE = 32
TL_MIX = 256
TL_FFN = 512
FF_CHUNK = 1024
ADA_BLOCK = 1536
SEQ_GROUP = 4
VMEM_LIMIT = 48 * 1024 * 1024
VMEM_LIMIT_LAYER = 60 * 1024 * 1024

F32 = jnp.float32
BF16 = jnp.bfloat16


def _bdot(a, b):
    return jnp.dot(a, b, preferred_element_type=F32)


def _rms(xf):
    return xf * lax.rsqrt(jnp.mean(xf * xf, axis=-1, keepdims=True) + EPS)


def _sigmoid(x):
    return 0.5 * jnp.tanh(0.5 * x) + 0.5


def _log_sigmoid(z):
    return jnp.minimum(z, 0.0) - jnp.log(1.0 + jnp.exp(-jnp.abs(z)))


def _rows_per_seq(m, n_rows):
    n_seq = m.shape[0]
    if n_seq == 1:
        return m
    rep = n_rows // n_seq
    return jnp.concatenate([jnp.broadcast_to(m[i:i + 1], (rep, m.shape[1])) for i in range(n_seq)],
                           axis=0)


def _norm_mod(x, g, shift, scale):
    n = x.shape[0]
    return (_rms(x) * g * (1.0 + _rows_per_seq(scale, n)) + _rows_per_seq(shift, n)).astype(BF16)


def _store_blocks(ref, val):
    nblk, rows = ref.shape[0], ref.shape[1]
    for i in range(nblk):
        ref[i] = val[i * rows:(i + 1) * rows].astype(ref.dtype)


def _load_blocks(ref):
    return jnp.concatenate([ref[i] for i in range(ref.shape[0])], axis=0)


def _cumsum_rows(a, period):
    n = a.shape[0]
    ri = lax.broadcasted_iota(jnp.int32, (n, n), 0)
    ci = lax.broadcasted_iota(jnp.int32, (n, n), 1)
    shift = period.bit_length() - 1
    same = jnp.right_shift(ri, shift) == jnp.right_shift(ci, shift)
    tri = jnp.where((ci <= ri) & same, 1.0, 0.0).astype(BF16)
    hi = a.astype(BF16)
    r1 = a - hi.astype(F32)
    mid = r1.astype(BF16)
    lo = (r1 - mid.astype(F32)).astype(BF16)
    return _bdot(tri, hi) + _bdot(tri, mid) + _bdot(tri, lo)


def _gla_scores(q, k, b, k_row, b_row, nsub):
    L = SUB * nsub
    blast = b_row(L - 1)
    qt = (q * jnp.exp2(b)).astype(BF16)
    kt = (k * jnp.exp2(blast - b)).astype(BF16)

    lane = lax.broadcasted_iota(jnp.int32, (SUB, L), 1)
    row = lax.broadcasted_iota(jnp.int32, (SUB, L), 0)
    qs = [q[SUB * i:SUB * (i + 1)] for i in range(nsub)]
    ks = [k[SUB * i:SUB * (i + 1)] for i in range(nsub)]
    bs = [b[SUB * i:SUB * (i + 1)] for i in range(nsub)]

    diag = []
    for i in range(nsub):
        acc = jnp.zeros((SUB, L), F32)
        for j in range(SUB):
            r = SUB * i + j
            dec = jnp.exp2(bs[i] - b_row(r))
            col = jnp.sum(qs[i] * dec * k_row(r), axis=1, keepdims=True)
            acc = jnp.where(lane == r, col, acc)
        diag.append(jnp.where(lane <= row + SUB * i, acc, 0.0))
    p = diag[0] if nsub == 1 else jnp.concatenate(diag, axis=0)

    p_off = None
    if nsub > 1:
        zero = jnp.zeros((SUB, GLA_HK), F32)
        lhs, rhs = [], []
        for j in range(nsub - 1):
            bend = b_row(SUB * j + SUB - 1)
            lrows = [zero if i <= j else qs[i] * jnp.exp2(bs[i] - bend) for i in range(nsub)]
            rrows = [ks[j] * jnp.exp2(bend - bs[j]) if i == j else zero for i in range(nsub)]
            lhs.append(jnp.concatenate(lrows, axis=0).astype(BF16))
            rhs.append(jnp.concatenate(rrows, axis=0).astype(BF16))
        lhs = jnp.concatenate(lhs, axis=1)
        rhs = jnp.concatenate(rhs, axis=1)
        p_off = lax.dot_general(lhs, rhs, (((1,), (1,)), ((), ())), preferred_element_type=F32)
    return qt, kt, p, p_off, blast


def _gla_apply(scores, v, s):
    qt, kt, p, p_off, blast = scores
    if p_off is not None:
        p = p + p_off
    o = _bdot(qt, s.astype(BF16)) + _bdot(p.astype(BF16), v)

    ri = lax.broadcasted_iota(jnp.int32, (GLA_HK, GLA_HK), 0)
    ci = lax.broadcasted_iota(jnp.int32, (GLA_HK, GLA_HK), 1)
    erow = jnp.broadcast_to(jnp.exp2(blast), (GLA_HK, GLA_HK))
    ecol = jnp.sum(jnp.where(ri == ci, erow, 0.0), axis=1, keepdims=True)
    s_new = s * ecol + lax.dot_general(kt, v, (((0,), (0,)), ((), ())), preferred_element_type=F32)
    return o, s_new


def _gla_head_scores(blk, h, q_s, k_s, b_s):
    ksl = slice(h * GLA_HK, (h + 1) * GLA_HK)
    k_row = lambda r: k_s[blk, r:r + 1, ksl]
    b_row = lambda r: b_s[blk, r:r + 1, ksl]
    return _gla_scores(q_s[blk, :, ksl], k_s[blk, :, ksl], b_s[blk, :, ksl], k_row, b_row,
                       q_s.shape[1] // SUB)


def _gla_head_apply(blk, h, scores, v_s, sg_s, gng, bo_s, get_state, put_state):
    vsl = slice(h * GLA_HV, (h + 1) * GLA_HV)
    o, s_new = _gla_apply(scores, v_s[blk, :, vsl].astype(BF16), get_state(h))
    put_state(h, s_new)
    o = _rms(o) * gng
    bo_s[blk, :, vsl] = (o * sg_s[blk, :, vsl]).astype(BF16)


def _gla_heads(blk, q_s, k_s, b_s, v_s, sg_s, gng, bo_s, get_state, put_state):
    for h in range(GLA_HEADS):
        scores = _gla_head_scores(blk, h, q_s, k_s, b_s)
        _gla_head_apply(blk, h, scores, v_s, sg_s, gng, bo_s, get_state, put_state)


def _pool_delta(u_b, s2_b, s4_b, s8_b, n, pos0):
    r = HIST_BASE + n
    s2_b[16:r, :] = u_b[16:r, :] + u_b[15:r - 1, :]
    s4_b[16:r, :] = s2_b[16:r, 128:512] + s2_b[14:r - 2, 128:512]
    s8_b[24:r, :] = s4_b[24:r, 128:384] + s4_b[20:r - 4, 128:384]
    s16 = s8_b[HIST_BASE:r, 128:256] + s8_b[HIST_BASE - 8:r - 8, 128:256]
    sums = (s2_b[HIST_BASE:r, 0:128], s4_b[HIST_BASE:r, 0:128], s8_b[HIST_BASE:r, 0:128], s16)
    pos1 = pos0 + lax.broadcasted_iota(jnp.int32, (n, 1), 0) + 1
    out = []
    for gi, w in enumerate(POOL_WINDOWS):
        cnt = jnp.minimum(pos1, w).astype(F32)
        out.append(sums[gi] / cnt - u_b[HIST_BASE:r, gi * POOL_GD:(gi + 1) * POOL_GD])
    return out


def _pool_mix(d, wpool, pscale):
    mixed = [_bdot(d[gi].astype(BF16), wpool[gi]) for gi in range(len(POOL_WINDOWS))]
    return (jnp.concatenate(mixed, axis=1) * pscale).astype(BF16)


def _project_gla(x, mod_ref, g1, wq, wk, wv, wg, walr, walpha, balpha, period,
                 h_s, q_s, k_s, v_s, sg_s, b_s):
    h = _norm_mod(x, g1[...], mod_ref[0], mod_ref[1])
    h_s[...] = h
    _store_blocks(q_s, _bdot(h, wq[...]) * (GLA_HK ** -0.5))
    _store_blocks(k_s, _bdot(h, wk[...]))
    _store_blocks(v_s, _bdot(h, wv[...]))
    g = _bdot(h, wg[...])
    _store_blocks(sg_s, g * _sigmoid(g))
    alr = _bdot(h, walr[...])
    z = _bdot(alr.astype(BF16), walpha[...]) + balpha[...]
    a = _log_sigmoid(z) * (LOG2_E / GLA_GATE_NORM)
    b = [_cumsum_rows(a[r:r + SLAB], period) for r in range(0, a.shape[0], SLAB)]
    _store_blocks(b_s, jnp.concatenate(b, axis=0))


def _gated_pool(h_s, aout, wga, wpa):
    return _sigmoid(_bdot(h_s[...], wga[...])) * _bdot(aout, wpa[...])


def _gate_b(h_s, wgb):
    return _sigmoid(_bdot(h_s[...], wgb[...]))


def _merge_out(x, gate1, merged_a, gate_b, bo, wpb, wout):
    merged = merged_a + gate_b * _bdot(bo, wpb[...])
    y = _bdot(merged.astype(BF16), wout[...])
    return x + _rows_per_seq(gate1, x.shape[0]) * y


def _adaln_kernel(c_ref, w_ref, b_ref, o_ref):
    c = c_ref[...]
    sc = (c * _sigmoid(c)).astype(BF16)
    o_ref[...] = _bdot(sc, w_ref[...].astype(BF16)) + b_ref[...]


def _prompt_layer_kernel(x_ref, xn_ref, mod_ref, g1, wu, wq, wk, wv, wg, walr, wga, wgb, walpha, balpha,
                         wpool, pscale, gng, wpa, wpb, wout, g2, w1, w2, gf,
                         y_ref, st_ref, hist_ref,
                         u_b, s2_b, s4_b, s8_b, h_s, q_s, k_s, v_s, sg_s, b_s, bo_s,
                         st_s, x1_s, h2_s, y_s, acc_s):
    t = pl.program_id(0)
    n_tiles = pl.num_programs(0) - 1
    tl = x_ref.shape[0]
    n_chunks = tl // CHUNK
    n_piece = GLA_HEADS
    piece = D_MODEL // n_piece
    assert n_chunks == w1.shape[0] and w1.shape[2] == D_MODEL

    @pl.when(t == 0)
    def _():
        st_s[...] = jnp.zeros_like(st_s)
        u_b[0:HIST_BASE, :] = jnp.zeros((HIST_BASE, POOL_WIDTH), F32)
        s2_b[0:16, :] = jnp.zeros((16, POOL_WIDTH), F32)
        x1_s[...] = jnp.zeros_like(x1_s)
        h2_s[...] = jnp.zeros_like(h2_s)
        h_s[...] = _norm_mod(x_ref[...], g1[...], mod_ref[0], mod_ref[1])

    h = h_s[...]
    alr = _bdot(h, walr[...])
    _store_blocks(k_s, _bdot(h, wk[...]))
    z = _bdot(alr.astype(BF16), walpha[...]) + balpha[...]
    a = _log_sigmoid(z) * (LOG2_E / GLA_GATE_NORM)
    _store_blocks(q_s, _bdot(h, wq[...]) * (GLA_HK ** -0.5))
    b = [_cumsum_rows(a[r:r + SLAB], CHUNK) for r in range(0, tl, SLAB)]
    _store_blocks(b_s, jnp.concatenate(b, axis=0))
    _store_blocks(v_s, _bdot(h, wv[...]))
    g = _bdot(h, wg[...])
    _store_blocks(sg_s, g * _sigmoid(g))

    def get_state(h):
        return st_s[h]

    def put_state(h, s_new):
        st_s[h] = s_new

    acc_s[...] = jnp.zeros_like(acc_s)

    def chunk_body(c, carry):
        def scores(h):
            return _gla_head_scores(c, h, q_s, k_s, b_s)

        def apply(h, sc):
            _gla_head_apply(c, h, sc, v_s, sg_s, gng[...], bo_s, get_state, put_state)

        def ffn_up(j):
            cols = slice(j * piece, (j + 1) * piece)
            y = jnp.maximum(_bdot(h2_s[...], w1[c, :, cols]), 0.0)
            y_s[:, cols] = (y * y).astype(BF16)

        def ffn_down(j):
            cols = slice(j * piece, (j + 1) * piece)
            acc_s[:, cols] += _bdot(y_s[...], w2[c, :, cols])

        sc0 = scores(0)
        ffn_up(0)
        sc1 = scores(1)
        ffn_up(1)
        apply(0, sc0)
        ffn_up(2)
        sc2 = scores(2)
        ffn_up(3)
        apply(1, sc1)
        ffn_down(0)
        sc3 = scores(3)
        ffn_down(1)
        apply(2, sc2)
        ffn_down(2)
        apply(3, sc3)
        ffn_down(3)
        return carry

    for c in range(n_chunks):
        chunk_body(c, 0)

    x2 = x1_s[...] + mod_ref[5] * acc_s[...]
    y_ref[...] = _rms(x2) * gf[...]

    u_b[HIST_BASE:HIST_BASE + tl, :] = _bdot(h, wu[...])
    gate_a = _sigmoid(_bdot(h, wga[...]))
    gate_b = _sigmoid(_bdot(h, wgb[...]))
    d = _pool_delta(u_b, s2_b, s4_b, s8_b, tl, t * tl)
    aout = _pool_mix(d, wpool, pscale[...])
    u_b[16:HIST_BASE, :] = u_b[16 + tl:HIST_BASE + tl, :]
    merged = gate_a * _bdot(aout, wpa[...]) + gate_b * _bdot(_load_blocks(bo_s), wpb[...])
    x1 = x_ref[...] + mod_ref[2] * _bdot(merged.astype(BF16), wout[...])
    x1_s[...] = x1
    h2_s[...] = _norm_mod(x1, g2[...], mod_ref[3], mod_ref[4])
    h_s[...] = _norm_mod(xn_ref[...], g1[...], mod_ref[0], mod_ref[1])

    @pl.when(t == n_tiles - 1)
    def _():
        st_ref[...] = st_s[...]
        hist_ref[...] = u_b[16:HIST_BASE, :]


def _sample_mixer_kernel(x_ref, mod_ref, s0_ref, cache_ref, g1, wu, wq, wk, wv, wg, walr, wga, wgb,
                         walpha, balpha, wpool, pscale, gng, wpa, wpb, wout,
                         x1_ref, st_ref, hist_ref,
                         u_b, s2_b, s4_b, s8_b, h_s, q_s, k_s, v_s, sg_s, b_s, bo_s, u_s, ao_s):
    s = pl.program_id(0)
    seq = q_s.shape[1]
    group = s0_ref.shape[0]

    @pl.when(s == 0)
    def _():
        u_b[0:HIST_BASE, :] = jnp.zeros((HIST_BASE, POOL_WIDTH), F32)
        s2_b[0:16, :] = jnp.zeros((16, POOL_WIDTH), F32)
        _project_gla(x_ref[...], mod_ref, g1, wq, wk, wv, wg, walr, walpha, balpha, seq,
                     h_s, q_s, k_s, v_s, sg_s, b_s)
        _store_blocks(u_s, _bdot(h_s[...], wu[...]))

    for i in range(group):
        blk = s * group + i
        u_b[HIST_BASE - POOL_HIST:HIST_BASE, :] = cache_ref[i]
        u_b[HIST_BASE:HIST_BASE + seq, :] = u_s[blk]
        d = _pool_delta(u_b, s2_b, s4_b, s8_b, seq, PAST_LEN)
        ao_s[blk] = _pool_mix(d, wpool, pscale[...])
        hist_ref[i] = u_b[HIST_BASE + seq - POOL_HIST:HIST_BASE + seq, :]

        def get_state(h, i=i):
            return s0_ref[i, h]

        def put_state(h, s_new, i=i):
            st_ref[i, h] = s_new

        _gla_heads(blk, q_s, k_s, b_s, v_s, sg_s, gng[...], bo_s, get_state, put_state)

    @pl.when(s == pl.num_programs(0) - 1)
    def _():
        merged_a = _gated_pool(h_s, _load_blocks(ao_s), wga, wpa)
        x1_ref[...] = _merge_out(x_ref[...], mod_ref[2], merged_a, _gate_b(h_s, wgb),
                                 _load_blocks(bo_s), wpb, wout)


def _ffn_kernel(x_ref, mod_ref, g2, w1, w2, gf, o_ref):
    x = x_ref[...]
    h2 = _norm_mod(x, g2[...], mod_ref[0], mod_ref[1])
    acc = jnp.zeros(x.shape, F32)
    for c in range(w1.shape[0]):
        y = jnp.maximum(_bdot(h2, w1[c]), 0.0)
        acc = acc + _bdot((y * y).astype(BF16), w2[c])
    x2 = x + _rows_per_seq(mod_ref[2], x.shape[0]) * acc
    o_ref[...] = _rms(x2) * gf[...]


def _const_spec(shape):
    nd = len(shape)
    return pl.BlockSpec(shape, lambda *_: (0,) * nd, pipeline_mode=pl.Buffered(1))


def _params(vmem_limit=None):
    return pltpu.CompilerParams(dimension_semantics=("arbitrary",),
                                vmem_limit_bytes=vmem_limit or VMEM_LIMIT)


def _mixer_scratch(nblk, rows, hist_rows):
    return [
        pltpu.VMEM((hist_rows, POOL_WIDTH), F32),
        pltpu.VMEM((hist_rows, POOL_WIDTH), F32),
        pltpu.VMEM((hist_rows, POOL_WIDTH - 128), F32),
        pltpu.VMEM((hist_rows, POOL_WIDTH - 256), F32),
        pltpu.VMEM((nblk * rows, D_MODEL), BF16),
        pltpu.VMEM((nblk, rows, GLA_DK), F32),
        pltpu.VMEM((nblk, rows, GLA_DK), F32),
        pltpu.VMEM((nblk, rows, GLA_DV), F32),
        pltpu.VMEM((nblk, rows, GLA_DV), F32),
        pltpu.VMEM((nblk, rows, GLA_DK), F32),
        pltpu.VMEM((nblk, rows, GLA_DV), BF16),
    ]


def _adaln(c_all, w_ada, b_ada):
    rows = c_all.shape[0]
    n = w_ada.shape[1]
    return pl.pallas_call(
        _adaln_kernel,
        grid=(n // ADA_BLOCK,),
        in_specs=[pl.BlockSpec((rows, D_MODEL), lambda j: (0, 0)),
                  pl.BlockSpec((D_MODEL, ADA_BLOCK), lambda j: (0, j)),
                  pl.BlockSpec((1, ADA_BLOCK), lambda j: (0, j))],
        out_specs=pl.BlockSpec((rows, ADA_BLOCK), lambda j: (0, j)),
        out_shape=jax.ShapeDtypeStruct((rows, n), F32),
        compiler_params=_params(),
        name="adaln_mod",
    )(c_all, w_ada, b_ada)


def _prompt_layer(x, mod, weights, ffn_weights):
    n_tok = x.shape[0]
    tl = TL_MIX
    n_tiles = n_tok // tl
    consts = (mod,) + tuple(weights) + tuple(ffn_weights)
    return pl.pallas_call(
        _prompt_layer_kernel,
        grid=(n_tiles + 1,),
        in_specs=[pl.BlockSpec((tl, D_MODEL), lambda t: (jnp.minimum(t, n_tiles - 1), 0)),
                  pl.BlockSpec((tl, D_MODEL), lambda t: (jnp.minimum(t + 1, n_tiles - 1), 0))]
        + [_const_spec(w.shape) for w in consts],
        out_specs=[pl.BlockSpec((tl, D_MODEL), lambda t: (jnp.maximum(t - 1, 0), 0)),
                   pl.BlockSpec((GLA_HEADS, GLA_HK, GLA_HV), lambda t: (0, 0, 0)),
                   pl.BlockSpec((16, POOL_WIDTH), lambda t: (0, 0))],
        out_shape=[jax.ShapeDtypeStruct((n_tok, D_MODEL), F32),
                   jax.ShapeDtypeStruct((GLA_HEADS, GLA_HK, GLA_HV), F32),
                   jax.ShapeDtypeStruct((16, POOL_WIDTH), F32)],
        scratch_shapes=_mixer_scratch(tl // CHUNK, CHUNK, HIST_BASE + tl) + [
            pltpu.VMEM((GLA_HEADS, GLA_HK, GLA_HV), F32),
            pltpu.VMEM((tl, D_MODEL), F32),
            pltpu.VMEM((tl, D_MODEL), BF16),
            pltpu.VMEM((tl, FF_CHUNK), BF16),
            pltpu.VMEM((tl, D_MODEL), F32),
        ],
        compiler_params=_params(VMEM_LIMIT_LAYER),
        name="prompt_layer",
    )(x, x, *consts)


def _sample_mixer(x, mod, s0, cache, weights):
    n_tok = x.shape[0]
    n_seq = s0.shape[0]
    seq = n_tok // n_seq
    group = SEQ_GROUP
    w_specs = [_const_spec(w.shape) for w in weights]
    st_spec = pl.BlockSpec((group, GLA_HEADS, GLA_HK, GLA_HV), lambda s: (s, 0, 0, 0))
    hist_spec = pl.BlockSpec((group, POOL_HIST, POOL_WIDTH), lambda s: (s, 0, 0))
    return pl.pallas_call(
        _sample_mixer_kernel,
        grid=(n_seq // group,),
        in_specs=[_const_spec(x.shape), _const_spec(mod.shape), st_spec, hist_spec] + w_specs,
        out_specs=[pl.BlockSpec((n_tok, D_MODEL), lambda s: (0, 0)), st_spec, hist_spec],
        out_shape=[jax.ShapeDtypeStruct((n_tok, D_MODEL), F32),
                   jax.ShapeDtypeStruct(s0.shape, F32),
                   jax.ShapeDtypeStruct(cache.shape, F32)],
        scratch_shapes=_mixer_scratch(n_seq, seq, HIST_BASE + seq) + [
            pltpu.VMEM((n_seq, seq, POOL_WIDTH), F32),
            pltpu.VMEM((n_seq, seq, POOL_WIDTH), BF16),
        ],
        compiler_params=_params(),
        name="sample_mixer",
    )(x, mod, s0, cache, *weights)


def _ffn_final(x, mod, g2, w1, w2, gf):
    n_tok = x.shape[0]
    tl = min(TL_FFN, n_tok)
    assert mod.shape[1] == 1 or n_tok == tl
    return pl.pallas_call(
        _ffn_kernel,
        grid=(n_tok // tl,),
        in_specs=[pl.BlockSpec((tl, D_MODEL), lambda t: (t, 0)), _const_spec(mod.shape),
                  _const_spec(g2.shape), _const_spec(w1.shape), _const_spec(w2.shape),
                  _const_spec(gf.shape)],
        out_specs=pl.BlockSpec((tl, D_MODEL), lambda t: (t, 0)),
        out_shape=jax.ShapeDtypeStruct((n_tok, D_MODEL), F32),
        compiler_params=_params(),
        name="ffn_final",
    )(x, mod, g2, w1, w2, gf)


def kernel(x_prompt, x_sample, c_prompt, c_sample, state_gla, cache_pool, w_ada, b_ada, norm1_g,
           w_in, w_alpha, b_alpha, w_pool, pool_scale, gla_norm_g, w_pa, w_pb, w_out, norm2_g,
           w_ff1, w_ff2, final_g):
    n_batch, n_seq_p, _ = x_prompt.shape
    n_dec, n_seq_s, _ = x_sample.shape
    assert n_batch == 1 and w_ada.shape[0] == 1
    assert n_seq_p % TL_MIX == 0 and SLAB % n_seq_s == 0 and n_seq_s % SUB == 0
    assert n_dec % SEQ_GROUP == 0

    n_c = n_batch + n_dec
    pad = (-n_c) % 8
    c_all = jnp.concatenate([c_prompt, c_sample, jnp.zeros((pad, D_MODEL), F32)], axis=0)
    mod = _adaln(c_all, w_ada[0], b_ada)
    mod = mod.reshape(n_c + pad, 6, D_MODEL).transpose(1, 0, 2)
    mod_p = mod[:, 0:1]
    mod_s = mod[:, n_batch:n_c]

    offs = [0]
    for sz in IN_SIZES:
        offs.append(offs[-1] + sz)
    w_in_b = w_in[0].astype(BF16)
    wu, wq, wk, wv, wg, walr, wga, wgb = [w_in_b[:, offs[i]:offs[i + 1]] for i in range(8)]
    walr = jnp.pad(walr, ((0, 0), (0, LANES - GLA_LOWRANK)))
    walpha = jnp.pad(w_alpha[0].astype(BF16), ((0, LANES - GLA_LOWRANK), (0, 0)))
    weights = (norm1_g, wu, wq, wk, wv, wg, walr, wga, wgb,
               walpha, b_alpha, w_pool[0].astype(BF16), pool_scale, gla_norm_g,
               w_pa[0].astype(BF16), w_pb[0].astype(BF16), w_out[0].astype(BF16))

    n_slab = D_FF // FF_CHUNK
    w1 = w_ff1[0].astype(BF16).reshape(D_MODEL, n_slab, FF_CHUNK).transpose(1, 0, 2)
    w2 = w_ff2[0].astype(BF16).reshape(n_slab, FF_CHUNK, D_MODEL)
    gf = final_g.reshape(1, D_MODEL)

    y_p, st_p, hist_p = _prompt_layer(x_prompt[0], mod_p, weights, (norm2_g, w1, w2, gf))
    x1_s, st_s, hist_s = _sample_mixer(x_sample.reshape(n_dec * n_seq_s, D_MODEL), mod_s[0:3],
                                       state_gla[0], cache_pool[0], weights)
    y_s = _ffn_final(x1_s, mod_s[3:6], norm2_g, w1, w2, gf)

    return (y_p[None], y_s.reshape(n_dec, n_seq_s, D_MODEL), st_p[None, None],
            hist_p[None, None, 1:], st_s[None], hist_s[None])
```

```python
import jax
import jax.numpy as jnp
from jax import lax
from jax.experimental import pallas as pl
from jax.experimental.pallas import tpu as pltpu

D_MODEL = 1024
PAST_LEN = 4096
POOL_WIDTH = 512
POOL_WINDOWS = (2, 4, 8, 16)
POOL_GD = 128
POOL_HIST = 15
GLA_HEADS = 4
GLA_DK = 512
GLA_DV = 1024
GLA_HK = 128
GLA_HV = 256
GLA_LOWRANK = 16
GLA_GATE_NORM = 16.0
D_FF = 4096
EPS = 1e-6
LOG2_E = 1.4426950408889634
LANES = 128
IN_SIZES = (POOL_WIDTH, GLA_DK, GLA_DK, GLA_DV, GLA_DV, GLA_LOWRANK, D_MODEL, D_MODEL)

SUB = 8
CHUNK = 64
SLAB = 64
HIST_BASE = 32
TL_MIX = 256
TL_FFN = 512
FF_CHUNK = 1024
ADA_BLOCK = 1536
SEQ_GROUP = 4
VMEM_LIMIT = 48 * 1024 * 1024
VMEM_LIMIT_LAYER = 60 * 1024 * 1024

F32 = jnp.float32
BF16 = jnp.bfloat16


def _bdot(a, b):
    return jnp.dot(a, b, preferred_element_type=F32)


def _proj(h, wt_ref):
    return lax.dot_general(h, wt_ref[...], (((1,), (1,)), ((), ())), preferred_element_type=F32)


def _rms(xf):
    return xf * lax.rsqrt(jnp.mean(xf * xf, axis=-1, keepdims=True) + EPS)


def _sigmoid(x):
    return 0.5 * jnp.tanh(0.5 * x) + 0.5


def _log_sigmoid(z):
    return jnp.minimum(z, 0.0) - jnp.log(1.0 + jnp.exp(-jnp.abs(z)))


def _rows_per_seq(m, n_rows):
    n_seq = m.shape[0]
    if n_seq == 1:
        return m
    rep = n_rows // n_seq
    return jnp.concatenate([jnp.broadcast_to(m[i:i + 1], (rep, m.shape[1])) for i in range(n_seq)],
                           axis=0)


def _norm_mod(x, g, shift, scale):
    n = x.shape[0]
    return (_rms(x) * g * (1.0 + _rows_per_seq(scale, n)) + _rows_per_seq(shift, n)).astype(BF16)


def _store_blocks(ref, val):
    nblk, rows = ref.shape[0], ref.shape[1]
    for i in range(nblk):
        ref[i] = val[i * rows:(i + 1) * rows].astype(ref.dtype)


def _load_blocks(ref):
    return jnp.concatenate([ref[i] for i in range(ref.shape[0])], axis=0)


def _cumsum_rows(a, period):
    n = a.shape[0]
    ri = lax.broadcasted_iota(jnp.int32, (n, n), 0)
    ci = lax.broadcasted_iota(jnp.int32, (n, n), 1)
    shift = period.bit_length() - 1
    same = jnp.right_shift(ri, shift) == jnp.right_shift(ci, shift)
    tri = jnp.where((ci <= ri) & same, 1.0, 0.0).astype(BF16)
    hi = a.astype(BF16)
    r1 = a - hi.astype(F32)
    mid = r1.astype(BF16)
    lo = (r1 - mid.astype(F32)).astype(BF16)
    return _bdot(tri, hi) + _bdot(tri, mid) + _bdot(tri, lo)


def _gla_scores(q, k, b, k_row, b_row, nsub):
    L = SUB * nsub
    blast = b_row(L - 1)
    qt = (q * jnp.exp2(b)).astype(BF16)
    kt = (k * jnp.exp2(blast - b)).astype(BF16)

    lane = lax.broadcasted_iota(jnp.int32, (SUB, L), 1)
    row = lax.broadcasted_iota(jnp.int32, (SUB, L), 0)
    qs = [q[SUB * i:SUB * (i + 1)] for i in range(nsub)]
    ks = [k[SUB * i:SUB * (i + 1)] for i in range(nsub)]
    bs = [b[SUB * i:SUB * (i + 1)] for i in range(nsub)]

    diag = []
    for i in range(nsub):
        acc = jnp.zeros((SUB, L), F32)
        for j in range(SUB):
            r = SUB * i + j
            dec = jnp.exp2(bs[i] - b_row(r))
            col = jnp.sum(qs[i] * dec * k_row(r), axis=1, keepdims=True)
            acc = jnp.where(lane == r, col, acc)
        diag.append(jnp.where(lane <= row + SUB * i, acc, 0.0))
    p = diag[0] if nsub == 1 else jnp.concatenate(diag, axis=0)

    p_off = None
    if nsub > 1:
        zero = jnp.zeros((SUB, GLA_HK), F32)
        lhs, rhs = [], []
        for j in range(nsub - 1):
            bend = b_row(SUB * j + SUB - 1)
            lrows = [zero if i <= j else qs[i] * jnp.exp2(bs[i] - bend) for i in range(nsub)]
            rrows = [ks[j] * jnp.exp2(bend - bs[j]) if i == j else zero for i in range(nsub)]
            lhs.append(jnp.concatenate(lrows, axis=0).astype(BF16))
            rhs.append(jnp.concatenate(rrows, axis=0).astype(BF16))
        lhs = jnp.concatenate(lhs, axis=1)
        rhs = jnp.concatenate(rhs, axis=1)
        p_off = lax.dot_general(lhs, rhs, (((1,), (1,)), ((), ())), preferred_element_type=F32)
    return qt, kt, p, p_off, blast


def _gla_apply(scores, v, s):
    qt, kt, p, p_off, blast = scores
    if p_off is not None:
        p = p + p_off
    o = _bdot(qt, s.astype(BF16)) + _bdot(p.astype(BF16), v)

    ri = lax.broadcasted_iota(jnp.int32, (GLA_HK, GLA_HK), 0)
    ci = lax.broadcasted_iota(jnp.int32, (GLA_HK, GLA_HK), 1)
    erow = jnp.broadcast_to(jnp.exp2(blast), (GLA_HK, GLA_HK))
    ecol = jnp.sum(jnp.where(ri == ci, erow, 0.0), axis=1, keepdims=True)
    s_new = s * ecol + lax.dot_general(kt, v, (((0,), (0,)), ((), ())), preferred_element_type=F32)
    return o, s_new


def _gla_head_scores(blk, h, q_s, k_s, b_s):
    ksl = slice(h * GLA_HK, (h + 1) * GLA_HK)
    k_row = lambda r: k_s[blk, r:r + 1, ksl]
    b_row = lambda r: b_s[blk, r:r + 1, ksl]
    return _gla_scores(q_s[blk, :, ksl], k_s[blk, :, ksl], b_s[blk, :, ksl], k_row, b_row,
                       q_s.shape[1] // SUB)


def _gla_head_apply(blk, h, scores, v_s, sg_s, gng, bo_s, get_state, put_state):
    vsl = slice(h * GLA_HV, (h + 1) * GLA_HV)
    o, s_new = _gla_apply(scores, v_s[blk, :, vsl].astype(BF16), get_state(h))
    put_state(h, s_new)
    o = _rms(o) * gng
    bo_s[blk, :, vsl] = (o * sg_s[blk, :, vsl]).astype(BF16)


def _gla_heads(blk, q_s, k_s, b_s, v_s, sg_s, gng, bo_s, get_state, put_state):
    for h in range(GLA_HEADS):
        scores = _gla_head_scores(blk, h, q_s, k_s, b_s)
        _gla_head_apply(blk, h, scores, v_s, sg_s, gng, bo_s, get_state, put_state)


def _pool_delta(u_b, s2_b, s4_b, s8_b, n, pos0):
    r = HIST_BASE + n
    s2_b[16:r, :] = u_b[16:r, :] + u_b[15:r - 1, :]
    s4_b[16:r, :] = s2_b[16:r, 128:512] + s2_b[14:r - 2, 128:512]
    s8_b[24:r, :] = s4_b[24:r, 128:384] + s4_b[20:r - 4, 128:384]
    s16 = s8_b[HIST_BASE:r, 128:256] + s8_b[HIST_BASE - 8:r - 8, 128:256]
    sums = (s2_b[HIST_BASE:r, 0:128], s4_b[HIST_BASE:r, 0:128], s8_b[HIST_BASE:r, 0:128], s16)
    pos1 = pos0 + lax.broadcasted_iota(jnp.int32, (n, 1), 0) + 1
    out = []
    for gi, w in enumerate(POOL_WINDOWS):
        cnt = jnp.minimum(pos1, w).astype(F32)
        out.append(sums[gi] / cnt - u_b[HIST_BASE:r, gi * POOL_GD:(gi + 1) * POOL_GD])
    return out


def _pool_mix(d, wpool, pscale):
    mixed = [_bdot(d[gi].astype(BF16), wpool[gi]) for gi in range(len(POOL_WINDOWS))]
    return (jnp.concatenate(mixed, axis=1) * pscale).astype(BF16)


def _project_gla(x, mod_ref, g1, wq, wk, wv, wg, walr, walpha, balpha, period,
                 h_s, q_s, k_s, v_s, sg_s, b_s):
    h = _norm_mod(x, g1[...], mod_ref[0], mod_ref[1])
    h_s[...] = h
    _store_blocks(q_s, _proj(h, wq) * (GLA_HK ** -0.5))
    _store_blocks(k_s, _proj(h, wk))
    _store_blocks(v_s, _proj(h, wv))
    g = _proj(h, wg)
    _store_blocks(sg_s, g * _sigmoid(g))
    alr = _proj(h, walr)
    z = _bdot(alr.astype(BF16), walpha[...]) + balpha[...]
    a = _log_sigmoid(z) * (LOG2_E / GLA_GATE_NORM)
    b = [_cumsum_rows(a[r:r + SLAB], period) for r in range(0, a.shape[0], SLAB)]
    _store_blocks(b_s, jnp.concatenate(b, axis=0))


def _gated_pool(h_s, aout, wga, wpa):
    return _sigmoid(_proj(h_s[...], wga)) * _bdot(aout, wpa[...])


def _gate_b(h_s, wgb):
    return _sigmoid(_proj(h_s[...], wgb))


def _merge_out(x, gate1, merged_a, gate_b, bo, wpb, wout):
    merged = merged_a + gate_b * _bdot(bo, wpb[...])
    y = _bdot(merged.astype(BF16), wout[...])
    return x + _rows_per_seq(gate1, x.shape[0]) * y


def _adaln_kernel(c_ref, w_ref, b_ref, o_ref):
    c = c_ref[...]
    sc = (c * _sigmoid(c)).astype(BF16)
    o_ref[...] = _bdot(sc, w_ref[...].astype(BF16)) + b_ref[...]


def _prompt_layer_kernel(x_ref, xn_ref, mod_ref, g1, wu, wq, wk, wv, wg, walr, wga, wgb, walpha, balpha,
                         wpool, pscale, gng, wpa, wpb, wout, g2, w1, w2, gf,
                         y_ref, st_ref, hist_ref,
                         u_b, s2_b, s4_b, s8_b, h_s, q_s, k_s, v_s, sg_s, b_s, bo_s,
                         st_s, x1_s, h2_s, y_s, acc_s):
    t = pl.program_id(0)
    n_tiles = pl.num_programs(0) - 1
    tl = x_ref.shape[0]
    n_chunks = tl // CHUNK
    n_piece = GLA_HEADS
    piece = D_MODEL // n_piece
    assert n_chunks * FF_CHUNK == w1.shape[1] and FF_CHUNK == n_piece * piece

    @pl.when(t == 0)
    def _():
        st_s[...] = jnp.zeros_like(st_s)
        u_b[0:HIST_BASE, :] = jnp.zeros((HIST_BASE, POOL_WIDTH), F32)
        s2_b[0:16, :] = jnp.zeros((16, POOL_WIDTH), F32)
        x1_s[...] = jnp.zeros_like(x1_s)
        h2_s[...] = jnp.zeros_like(h2_s)
        h_s[...] = _norm_mod(x_ref[...], g1[...], mod_ref[0], mod_ref[1])

    h = h_s[...]
    alr = _proj(h, walr)
    _store_blocks(k_s, _proj(h, wk))
    z = _bdot(alr.astype(BF16), walpha[...]) + balpha[...]
    a = _log_sigmoid(z) * (LOG2_E / GLA_GATE_NORM)
    _store_blocks(q_s, _proj(h, wq) * (GLA_HK ** -0.5))
    b = [_cumsum_rows(a[r:r + SLAB], CHUNK) for r in range(0, tl, SLAB)]
    _store_blocks(b_s, jnp.concatenate(b, axis=0))
    _store_blocks(v_s, _proj(h, wv))
    g = _proj(h, wg)
    _store_blocks(sg_s, g * _sigmoid(g))

    def get_state(h):
        return st_s[h]

    def put_state(h, s_new):
        st_s[h] = s_new

    acc_s[...] = jnp.zeros_like(acc_s)

    def chunk_body(c, carry):
        def scores(h):
            return _gla_head_scores(c, h, q_s, k_s, b_s)

        def apply(h, sc):
            _gla_head_apply(c, h, sc, v_s, sg_s, gng[...], bo_s, get_state, put_state)

        def ffn_up(j):
            cols = slice(j * piece, (j + 1) * piece)
            w1_cols = slice(c * FF_CHUNK + j * piece, c * FF_CHUNK + (j + 1) * piece)
            y = jnp.maximum(_bdot(h2_s[...], w1[:, w1_cols]), 0.0)
            y_s[:, cols] = (y * y).astype(BF16)

        def ffn_down(j):
            cols = slice(j * piece, (j + 1) * piece)
            acc_s[:, cols] += _bdot(y_s[...], w2[c * FF_CHUNK:(c + 1) * FF_CHUNK, cols])

        sc0 = scores(0)
        ffn_up(0)
        sc1 = scores(1)
        ffn_up(1)
        apply(0, sc0)
        ffn_up(2)
        sc2 = scores(2)
        ffn_up(3)
        apply(1, sc1)
        ffn_down(0)
        sc3 = scores(3)
        ffn_down(1)
        apply(2, sc2)
        ffn_down(2)
        apply(3, sc3)
        ffn_down(3)
        return carry

    for c in range(n_chunks):
        chunk_body(c, 0)

    x2 = x1_s[...] + mod_ref[5] * acc_s[...]
    y_ref[...] = _rms(x2) * gf[...]

    u_b[HIST_BASE:HIST_BASE + tl, :] = _proj(h, wu)
    gate_a = _sigmoid(_proj(h, wga))
    gate_b = _sigmoid(_proj(h, wgb))
    d = _pool_delta(u_b, s2_b, s4_b, s8_b, tl, t * tl)
    aout = _pool_mix(d, wpool, pscale[...])
    u_b[16:HIST_BASE, :] = u_b[16 + tl:HIST_BASE + tl, :]
    merged = gate_a * _bdot(aout, wpa[...]) + gate_b * _bdot(_load_blocks(bo_s), wpb[...])
    x1 = x_ref[...] + mod_ref[2] * _bdot(merged.astype(BF16), wout[...])
    x1_s[...] = x1
    h2_s[...] = _norm_mod(x1, g2[...], mod_ref[3], mod_ref[4])
    h_s[...] = _norm_mod(xn_ref[...], g1[...], mod_ref[0], mod_ref[1])

    @pl.when(t == n_tiles - 1)
    def _():
        st_ref[...] = st_s[...]
        hist_ref[...] = u_b[16:HIST_BASE, :]


def _sample_mixer_kernel(x_ref, mod_ref, s0_ref, cache_ref, g1, wu, wq, wk, wv, wg, walr, wga, wgb,
                         walpha, balpha, wpool, pscale, gng, wpa, wpb, wout,
                         x1_ref, st_ref, hist_ref,
                         u_b, s2_b, s4_b, s8_b, h_s, q_s, k_s, v_s, sg_s, b_s, bo_s, u_s, ao_s):
    s = pl.program_id(0)
    seq = q_s.shape[1]
    group = s0_ref.shape[0]

    @pl.when(s == 0)
    def _():
        u_b[0:HIST_BASE, :] = jnp.zeros((HIST_BASE, POOL_WIDTH), F32)
        s2_b[0:16, :] = jnp.zeros((16, POOL_WIDTH), F32)
        _project_gla(x_ref[...], mod_ref, g1, wq, wk, wv, wg, walr, walpha, balpha, seq,
                     h_s, q_s, k_s, v_s, sg_s, b_s)
        _store_blocks(u_s, _proj(h_s[...], wu))

    for i in range(group):
        blk = s * group + i
        u_b[HIST_BASE - POOL_HIST:HIST_BASE, :] = cache_ref[i]
        u_b[HIST_BASE:HIST_BASE + seq, :] = u_s[blk]
        d = _pool_delta(u_b, s2_b, s4_b, s8_b, seq, PAST_LEN)
        ao_s[blk] = _pool_mix(d, wpool, pscale[...])
        hist_ref[i] = u_b[HIST_BASE + seq - POOL_HIST:HIST_BASE + seq, :]

        def get_state(h, i=i):
            return s0_ref[i, h]

        def put_state(h, s_new, i=i):
            st_ref[i, h] = s_new

        _gla_heads(blk, q_s, k_s, b_s, v_s, sg_s, gng[...], bo_s, get_state, put_state)

    @pl.when(s == pl.num_programs(0) - 1)
    def _():
        merged_a = _gated_pool(h_s, _load_blocks(ao_s), wga, wpa)
        x1_ref[...] = _merge_out(x_ref[...], mod_ref[2], merged_a, _gate_b(h_s, wgb),
                                 _load_blocks(bo_s), wpb, wout)


def _ffn_kernel(x_ref, mod_ref, g2, w1, w2, gf, o_ref):
    x = x_ref[...]
    h2 = _norm_mod(x, g2[...], mod_ref[0], mod_ref[1])
    acc = jnp.zeros(x.shape, F32)
    for c in range(D_FF // FF_CHUNK):
        sl = slice(c * FF_CHUNK, (c + 1) * FF_CHUNK)
        y = jnp.maximum(_bdot(h2, w1[:, sl]), 0.0)
        acc = acc + _bdot((y * y).astype(BF16), w2[sl, :])
    x2 = x + _rows_per_seq(mod_ref[2], x.shape[0]) * acc
    o_ref[...] = _rms(x2) * gf[...]


def _const_spec(shape):
    nd = len(shape)
    return pl.BlockSpec(shape, lambda *_: (0,) * nd, pipeline_mode=pl.Buffered(1))


def _params(vmem_limit=None):
    return pltpu.CompilerParams(dimension_semantics=("arbitrary",),
                                vmem_limit_bytes=vmem_limit or VMEM_LIMIT)


def _mixer_scratch(nblk, rows, hist_rows):
    return [
        pltpu.VMEM((hist_rows, POOL_WIDTH), F32),
        pltpu.VMEM((hist_rows, POOL_WIDTH), F32),
        pltpu.VMEM((hist_rows, POOL_WIDTH - 128), F32),
        pltpu.VMEM((hist_rows, POOL_WIDTH - 256), F32),
        pltpu.VMEM((nblk * rows, D_MODEL), BF16),
        pltpu.VMEM((nblk, rows, GLA_DK), F32),
        pltpu.VMEM((nblk, rows, GLA_DK), F32),
        pltpu.VMEM((nblk, rows, GLA_DV), F32),
        pltpu.VMEM((nblk, rows, GLA_DV), F32),
        pltpu.VMEM((nblk, rows, GLA_DK), F32),
        pltpu.VMEM((nblk, rows, GLA_DV), BF16),
    ]


def _adaln(c_all, w_ada, b_ada):
    rows = c_all.shape[0]
    n = w_ada.shape[1]
    return pl.pallas_call(
        _adaln_kernel,
        grid=(n // ADA_BLOCK,),
        in_specs=[pl.BlockSpec((rows, D_MODEL), lambda j: (0, 0)),
                  pl.BlockSpec((D_MODEL, ADA_BLOCK), lambda j: (0, j)),
                  pl.BlockSpec((1, ADA_BLOCK), lambda j: (0, j))],
        out_specs=pl.BlockSpec((rows, ADA_BLOCK), lambda j: (0, j)),
        out_shape=jax.ShapeDtypeStruct((rows, n), F32),
        compiler_params=_params(),
        name="adaln_mod",
    )(c_all, w_ada, b_ada)


def _prompt_layer(x, mod, weights, ffn_weights):
    n_tok = x.shape[0]
    tl = TL_MIX
    n_tiles = n_tok // tl
    consts = (mod,) + tuple(weights) + tuple(ffn_weights)
    return pl.pallas_call(
        _prompt_layer_kernel,
        grid=(n_tiles + 1,),
        in_specs=[pl.BlockSpec((tl, D_MODEL), lambda t: (jnp.minimum(t, n_tiles - 1), 0)),
                  pl.BlockSpec((tl, D_MODEL), lambda t: (jnp.minimum(t + 1, n_tiles - 1), 0))]
        + [_const_spec(w.shape) for w in consts],
        out_specs=[pl.BlockSpec((tl, D_MODEL), lambda t: (jnp.maximum(t - 1, 0), 0)),
                   pl.BlockSpec((GLA_HEADS, GLA_HK, GLA_HV), lambda t: (0, 0, 0)),
                   pl.BlockSpec((16, POOL_WIDTH), lambda t: (0, 0))],
        out_shape=[jax.ShapeDtypeStruct((n_tok, D_MODEL), F32),
                   jax.ShapeDtypeStruct((GLA_HEADS, GLA_HK, GLA_HV), F32),
                   jax.ShapeDtypeStruct((16, POOL_WIDTH), F32)],
        scratch_shapes=_mixer_scratch(tl // CHUNK, CHUNK, HIST_BASE + tl) + [
            pltpu.VMEM((GLA_HEADS, GLA_HK, GLA_HV), F32),
            pltpu.VMEM((tl, D_MODEL), F32),
            pltpu.VMEM((tl, D_MODEL), BF16),
            pltpu.VMEM((tl, FF_CHUNK), BF16),
            pltpu.VMEM((tl, D_MODEL), F32),
        ],
        compiler_params=_params(VMEM_LIMIT_LAYER),
        name="prompt_layer",
    )(x, x, *consts)


def _sample_mixer(x, mod, s0, cache, weights):
    n_tok = x.shape[0]
    n_seq = s0.shape[0]
    seq = n_tok // n_seq
    group = SEQ_GROUP
    w_specs = [_const_spec(w.shape) for w in weights]
    st_spec = pl.BlockSpec((group, GLA_HEADS, GLA_HK, GLA_HV), lambda s: (s, 0, 0, 0))
    hist_spec = pl.BlockSpec((group, POOL_HIST, POOL_WIDTH), lambda s: (s, 0, 0))
    return pl.pallas_call(
        _sample_mixer_kernel,
        grid=(n_seq // group,),
        in_specs=[_const_spec(x.shape), _const_spec(mod.shape), st_spec, hist_spec] + w_specs,
        out_specs=[pl.BlockSpec((n_tok, D_MODEL), lambda s: (0, 0)), st_spec, hist_spec],
        out_shape=[jax.ShapeDtypeStruct((n_tok, D_MODEL), F32),
                   jax.ShapeDtypeStruct(s0.shape, F32),
                   jax.ShapeDtypeStruct(cache.shape, F32)],
        scratch_shapes=_mixer_scratch(n_seq, seq, HIST_BASE + seq) + [
            pltpu.VMEM((n_seq, seq, POOL_WIDTH), F32),
            pltpu.VMEM((n_seq, seq, POOL_WIDTH), BF16),
        ],
        compiler_params=_params(),
        name="sample_mixer",
    )(x, mod, s0, cache, *weights)


def _ffn_final(x, mod, g2, w1, w2, gf):
    n_tok = x.shape[0]
    tl = min(TL_FFN, n_tok)
    assert mod.shape[1] == 1 or n_tok == tl
    return pl.pallas_call(
        _ffn_kernel,
        grid=(n_tok // tl,),
        in_specs=[pl.BlockSpec((tl, D_MODEL), lambda t: (t, 0)), _const_spec(mod.shape),
                  _const_spec(g2.shape), _const_spec(w1.shape), _const_spec(w2.shape),
                  _const_spec(gf.shape)],
        out_specs=pl.BlockSpec((tl, D_MODEL), lambda t: (t, 0)),
        out_shape=jax.ShapeDtypeStruct((n_tok, D_MODEL), F32),
        compiler_params=_params(),
        name="ffn_final",
    )(x, mod, g2, w1, w2, gf)


def kernel(x_prompt, x_sample, c_prompt, c_sample, state_gla, cache_pool, w_ada, b_ada, norm1_g,
           w_in, w_alpha, b_alpha, w_pool, pool_scale, gla_norm_g, w_pa, w_pb, w_out, norm2_g,
           w_ff1, w_ff2, final_g):
    n_batch, n_seq_p, _ = x_prompt.shape
    n_dec, n_seq_s, _ = x_sample.shape
    assert n_batch == 1 and w_ada.shape[0] == 1
    assert n_seq_p % TL_MIX == 0 and SLAB % n_seq_s == 0 and n_seq_s % SUB == 0
    assert n_dec % SEQ_GROUP == 0

    n_c = n_batch + n_dec
    pad = (-n_c) % 8
    c_all = jnp.concatenate([c_prompt, c_sample, jnp.zeros((pad, D_MODEL), F32)], axis=0)
    mod = _adaln(c_all, w_ada[0], b_ada)
    mod = mod.reshape(n_c + pad, 6, D_MODEL).transpose(1, 0, 2)
    mod_p = mod[:, 0:1]
    mod_s = mod[:, n_batch:n_c]

    offs = [0]
    for sz in IN_SIZES:
        offs.append(offs[-1] + sz)
    w_in_t = jnp.swapaxes(w_in[0], 0, 1).astype(BF16)
    wu, wq, wk, wv, wg, walr, wga, wgb = [w_in_t[offs[i]:offs[i + 1]] for i in range(8)]
    walr = jnp.pad(walr, ((0, LANES - GLA_LOWRANK), (0, 0)))
    walpha = jnp.pad(w_alpha[0].astype(BF16), ((0, LANES - GLA_LOWRANK), (0, 0)))
    weights = (norm1_g, wu, wq, wk, wv, wg, walr, wga, wgb,
               walpha, b_alpha, w_pool[0].astype(BF16), pool_scale, gla_norm_g,
               w_pa[0].astype(BF16), w_pb[0].astype(BF16), w_out[0].astype(BF16))

    w1 = w_ff1[0].astype(BF16)
    w2 = w_ff2[0].astype(BF16)
    gf = final_g.reshape(1, D_MODEL)

    y_p, st_p, hist_p = _prompt_layer(x_prompt[0], mod_p, weights, (norm2_g, w1, w2, gf))
    x1_s, st_s, hist_s = _sample_mixer(x_sample.reshape(n_dec * n_seq_s, D_MODEL), mod_s[0:3],
                                       state_gla[0], cache_pool[0], weights)
    y_s = _ffn_final(x1_s, mod_s[3:6], norm2_g, w1, w2, gf)

    return (y_p[None], y_s.reshape(n_dec, n_seq_s, D_MODEL), st_p[None, None],
            hist_p[None, None, 1:], st_s[None], hist_s[None])
```

```python
import jax
import jax.numpy as jnp
from jax import lax
from jax.experimental import pallas as pl
from jax.experimental.pallas import tpu as pltpu

D_MODEL = 1024
PAST_LEN = 4096
POOL_WIDTH = 512
POOL_WINDOWS = (2, 4, 8, 16)
POOL_GD = 128
POOL_HIST = 15
GLA_HEADS = 4
GLA_DK = 512
GLA_DV = 1024
GLA_HK = 128
GLA_HV = 256
GLA_LOWRANK = 16
GLA_GATE_NORM = 16.0
D_FF = 4096
EPS = 1e-6
LOG2_E = 1.4426950408889634
LANES = 128
IN_SIZES = (POOL_WIDTH, GLA_DK, GLA_DK, GLA_DV, GLA_DV, GLA_LOWRANK, D_MODEL, D_MODEL)

SUB = 8
CHUNK = 64
SLAB = 64
HIST_BASE = 32
TL_MIX = 256
TL_FFN = 512
FF_CHUNK = 1024
ADA_BLOCK = 1536
SEQ_GROUP = 4
VMEM_LIMIT = 48 * 1024 * 1024
VMEM_LIMIT_LAYER = 60 * 1024 * 1024

F32 = jnp.float32
BF16 = jnp.bfloat16


def _bdot(a, b):
    return jnp.dot(a, b, preferred_element_type=F32)


def _proj(h, wt_ref):
    return lax.dot_general(h, wt_ref[...], (((1,), (1,)), ((), ())), preferred_element_type=F32)


def _rms(xf):
    return xf * lax.rsqrt(jnp.mean(xf * xf, axis=-1, keepdims=True) + EPS)


def _sigmoid(x):
    return 0.5 * jnp.tanh(0.5 * x) + 0.5


def _log_sigmoid(z):
    return jnp.minimum(z, 0.0) - jnp.log(1.0 + jnp.exp(-jnp.abs(z)))


def _rows_per_seq(m, n_rows):
    n_seq = m.shape[0]
    if n_seq == 1:
        return m
    rep = n_rows // n_seq
    return jnp.concatenate([jnp.broadcast_to(m[i:i + 1], (rep, m.shape[1])) for i in range(n_seq)],
                           axis=0)


def _norm_mod(x, g, shift, scale):
    n = x.shape[0]
    return (_rms(x) * g * (1.0 + _rows_per_seq(scale, n)) + _rows_per_seq(shift, n)).astype(BF16)


def _store_blocks(ref, val):
    nblk, rows = ref.shape[0], ref.shape[1]
    for i in range(nblk):
        ref[i] = val[i * rows:(i + 1) * rows].astype(ref.dtype)


def _load_blocks(ref):
    return jnp.concatenate([ref[i] for i in range(ref.shape[0])], axis=0)


def _cumsum_rows(a, period):
    n = a.shape[0]
    ri = lax.broadcasted_iota(jnp.int32, (n, n), 0)
    ci = lax.broadcasted_iota(jnp.int32, (n, n), 1)
    shift = period.bit_length() - 1
    same = jnp.right_shift(ri, shift) == jnp.right_shift(ci, shift)
    tri = jnp.where((ci <= ri) & same, 1.0, 0.0).astype(BF16)
    hi = a.astype(BF16)
    r1 = a - hi.astype(F32)
    mid = r1.astype(BF16)
    lo = (r1 - mid.astype(F32)).astype(BF16)
    return _bdot(tri, hi) + _bdot(tri, mid) + _bdot(tri, lo)


def _gla_scores(q, k, b, k_row, b_row, nsub):
    L = SUB * nsub
    blast = b_row(L - 1)
    qt = (q * jnp.exp2(b)).astype(BF16)
    kt = (k * jnp.exp2(blast - b)).astype(BF16)

    lane = lax.broadcasted_iota(jnp.int32, (SUB, L), 1)
    row = lax.broadcasted_iota(jnp.int32, (SUB, L), 0)
    qs = [q[SUB * i:SUB * (i + 1)] for i in range(nsub)]
    ks = [k[SUB * i:SUB * (i + 1)] for i in range(nsub)]
    bs = [b[SUB * i:SUB * (i + 1)] for i in range(nsub)]

    diag = []
    for i in range(nsub):
        acc = jnp.zeros((SUB, L), F32)
        for j in range(SUB):
            r = SUB * i + j
            dec = jnp.exp2(bs[i] - b_row(r))
            col = jnp.sum(qs[i] * dec * k_row(r), axis=1, keepdims=True)
            acc = jnp.where(lane == r, col, acc)
        diag.append(jnp.where(lane <= row + SUB * i, acc, 0.0))
    p = diag[0] if nsub == 1 else jnp.concatenate(diag, axis=0)

    p_off = None
    if nsub > 1:
        zero = jnp.zeros((SUB, GLA_HK), F32)
        lhs, rhs = [], []
        for j in range(nsub - 1):
            bend = b_row(SUB * j + SUB - 1)
            lrows = [zero if i <= j else qs[i] * jnp.exp2(bs[i] - bend) for i in range(nsub)]
            rrows = [ks[j] * jnp.exp2(bend - bs[j]) if i == j else zero for i in range(nsub)]
            lhs.append(jnp.concatenate(lrows, axis=0).astype(BF16))
            rhs.append(jnp.concatenate(rrows, axis=0).astype(BF16))
        lhs = jnp.concatenate(lhs, axis=1)
        rhs = jnp.concatenate(rhs, axis=1)
        p_off = lax.dot_general(lhs, rhs, (((1,), (1,)), ((), ())), preferred_element_type=F32)
    return qt, kt, p, p_off, blast


def _gla_apply(scores, v, s):
    qt, kt, p, p_off, blast = scores
    if p_off is not None:
        p = p + p_off
    o = _bdot(jnp.concatenate([qt, p.astype(BF16)], axis=1),
              jnp.concatenate([s.astype(BF16), v], axis=0))

    ri = lax.broadcasted_iota(jnp.int32, (GLA_HK, GLA_HK), 0)
    ci = lax.broadcasted_iota(jnp.int32, (GLA_HK, GLA_HK), 1)
    erow = jnp.broadcast_to(jnp.exp2(blast), (GLA_HK, GLA_HK))
    ecol = jnp.sum(jnp.where(ri == ci, erow, 0.0), axis=1, keepdims=True)
    s_new = s * ecol + lax.dot_general(kt, v, (((0,), (0,)), ((), ())), preferred_element_type=F32)
    return o, s_new


def _gla_head_scores(blk, h, q_s, k_s, b_s):
    ksl = slice(h * GLA_HK, (h + 1) * GLA_HK)
    k_row = lambda r: k_s[blk, r:r + 1, ksl]
    b_row = lambda r: b_s[blk, r:r + 1, ksl]
    return _gla_scores(q_s[blk, :, ksl], k_s[blk, :, ksl], b_s[blk, :, ksl], k_row, b_row,
                       q_s.shape[1] // SUB)


def _gla_head_apply(blk, h, scores, v_s, sg_s, gng, bo_s, get_state, put_state):
    vsl = slice(h * GLA_HV, (h + 1) * GLA_HV)
    o, s_new = _gla_apply(scores, v_s[blk, :, vsl].astype(BF16), get_state(h))
    put_state(h, s_new)
    o = _rms(o) * gng
    bo_s[blk, :, vsl] = (o * sg_s[blk, :, vsl]).astype(BF16)


def _gla_heads(blk, q_s, k_s, b_s, v_s, sg_s, gng, bo_s, get_state, put_state):
    for h in range(GLA_HEADS):
        scores = _gla_head_scores(blk, h, q_s, k_s, b_s)
        _gla_head_apply(blk, h, scores, v_s, sg_s, gng, bo_s, get_state, put_state)


def _pool_delta(u_b, s2_b, s4_b, s8_b, n, pos0):
    r = HIST_BASE + n
    s2_b[16:r, :] = u_b[16:r, :] + u_b[15:r - 1, :]
    s4_b[16:r, :] = s2_b[16:r, 128:512] + s2_b[14:r - 2, 128:512]
    s8_b[24:r, :] = s4_b[24:r, 128:384] + s4_b[20:r - 4, 128:384]
    s16 = s8_b[HIST_BASE:r, 128:256] + s8_b[HIST_BASE - 8:r - 8, 128:256]
    sums = (s2_b[HIST_BASE:r, 0:128], s4_b[HIST_BASE:r, 0:128], s8_b[HIST_BASE:r, 0:128], s16)
    pos1 = pos0 + lax.broadcasted_iota(jnp.int32, (n, 1), 0) + 1
    out = []
    for gi, w in enumerate(POOL_WINDOWS):
        cnt = jnp.minimum(pos1, w).astype(F32)
        out.append(sums[gi] / cnt - u_b[HIST_BASE:r, gi * POOL_GD:(gi + 1) * POOL_GD])
    return out


def _pool_mix(d, wpool, pscale):
    mixed = [_bdot(d[gi].astype(BF16), wpool[gi]) for gi in range(len(POOL_WINDOWS))]
    return (jnp.concatenate(mixed, axis=1) * pscale).astype(BF16)


def _project_gla(x, mod_ref, g1, wq, wk, wv, wg, walr, walpha, balpha, period,
                 h_s, q_s, k_s, v_s, sg_s, b_s):
    h = _norm_mod(x, g1[...], mod_ref[0], mod_ref[1])
    h_s[...] = h
    _store_blocks(q_s, _proj(h, wq) * (GLA_HK ** -0.5))
    _store_blocks(k_s, _proj(h, wk))
    _store_blocks(v_s, _proj(h, wv))
    g = _proj(h, wg)
    _store_blocks(sg_s, g * _sigmoid(g))
    alr = _proj(h, walr)
    z = _bdot(alr.astype(BF16), walpha[...]) + balpha[...]
    a = _log_sigmoid(z) * (LOG2_E / GLA_GATE_NORM)
    b = [_cumsum_rows(a[r:r + SLAB], period) for r in range(0, a.shape[0], SLAB)]
    _store_blocks(b_s, jnp.concatenate(b, axis=0))


def _gated_pool(h_s, aout, wga, wpa):
    return _sigmoid(_proj(h_s[...], wga)) * _bdot(aout, wpa[...])


def _gate_b(h_s, wgb):
    return _sigmoid(_proj(h_s[...], wgb))


def _merge_out(x, gate1, merged_a, gate_b, bo, wpb, wout):
    merged = merged_a + gate_b * _bdot(bo, wpb[...])
    y = _bdot(merged.astype(BF16), wout[...])
    return x + _rows_per_seq(gate1, x.shape[0]) * y


def _adaln_kernel(c_ref, w_ref, b_ref, o_ref):
    c = c_ref[...]
    sc = (c * _sigmoid(c)).astype(BF16)
    o_ref[...] = _bdot(sc, w_ref[...].astype(BF16)) + b_ref[...]


def _prompt_layer_kernel(x_ref, xn_ref, mod_ref, g1, wu, wq, wk, wv, wg, walr, wga, wgb, walpha, balpha,
                         wpool, pscale, gng, wpa, wpb, wout, g2, w1, w2, gf,
                         y_ref, st_ref, hist_ref,
                         u_b, s2_b, s4_b, s8_b, hh_s, q_s, k_s, v_s, sg_s, b_s, bo_s,
                         st_s, x1_s, h2_s, y_s, acc_s):
    t = pl.program_id(0)
    n_tiles = pl.num_programs(0) - 1
    tl = x_ref.shape[0]
    n_chunks = tl // CHUNK
    n_piece = GLA_HEADS
    piece = D_MODEL // n_piece
    assert n_chunks * FF_CHUNK == w1.shape[1] and FF_CHUNK == n_piece * piece

    @pl.when(t == 0)
    def _():
        st_s[...] = jnp.zeros_like(st_s)
        u_b[0:HIST_BASE, :] = jnp.zeros((HIST_BASE, POOL_WIDTH), F32)
        s2_b[0:16, :] = jnp.zeros((16, POOL_WIDTH), F32)
        x1_s[...] = jnp.zeros_like(x1_s)
        hh_s[0] = _norm_mod(x_ref[...], g1[...], mod_ref[0], mod_ref[1])

    slot = lax.rem(t, 2)
    alr = _proj(hh_s[slot], walr)
    _store_blocks(k_s, _proj(hh_s[slot], wk))
    h2_s[...] = _norm_mod(x1_s[...], g2[...], mod_ref[3], mod_ref[4])
    z = _bdot(alr.astype(BF16), walpha[...]) + balpha[...]
    a = _log_sigmoid(z) * (LOG2_E / GLA_GATE_NORM)
    _store_blocks(q_s, _proj(hh_s[slot], wq) * (GLA_HK ** -0.5))
    b = [_cumsum_rows(a[r:r + SLAB], CHUNK) for r in range(0, tl, SLAB)]
    _store_blocks(b_s, jnp.concatenate(b, axis=0))
    _store_blocks(v_s, _proj(hh_s[slot], wv))
    g = _proj(hh_s[slot], wg)
    _store_blocks(sg_s, g * _sigmoid(g))

    def get_state(h):
        return st_s[h]

    def put_state(h, s_new):
        st_s[h] = s_new

    acc_s[...] = jnp.zeros_like(acc_s)

    def chunk_body(c, carry):
        def scores(h):
            return _gla_head_scores(c, h, q_s, k_s, b_s)

        def apply(h, sc):
            _gla_head_apply(c, h, sc, v_s, sg_s, gng[...], bo_s, get_state, put_state)

        def ffn_up(j):
            cols = slice(j * piece, (j + 1) * piece)
            w1_cols = slice(c * FF_CHUNK + j * piece, c * FF_CHUNK + (j + 1) * piece)
            y = jnp.maximum(_bdot(h2_s[...], w1[:, w1_cols]), 0.0)
            y_s[:, cols] = (y * y).astype(BF16)

        def ffn_down(j):
            cols = slice(j * piece, (j + 1) * piece)
            acc_s[:, cols] += _bdot(y_s[...], w2[c * FF_CHUNK:(c + 1) * FF_CHUNK, cols])

        sc0 = scores(0)
        ffn_up(0)
        sc1 = scores(1)
        ffn_up(1)
        apply(0, sc0)
        ffn_up(2)
        sc2 = scores(2)
        ffn_up(3)
        apply(1, sc1)
        ffn_down(0)
        sc3 = scores(3)
        ffn_down(1)
        apply(2, sc2)
        ffn_down(2)
        apply(3, sc3)
        ffn_down(3)
        return carry

    for c in range(n_chunks):
        chunk_body(c, 0)

    x2 = x1_s[...] + mod_ref[5] * acc_s[...]
    y_ref[...] = _rms(x2) * gf[...]

    hh_s[1 - slot] = _norm_mod(xn_ref[...], g1[...], mod_ref[0], mod_ref[1])

    u_b[HIST_BASE:HIST_BASE + tl, :] = _proj(hh_s[slot], wu)
    gate_a = _sigmoid(_proj(hh_s[slot], wga))
    gate_b = _sigmoid(_proj(hh_s[slot], wgb))
    d = _pool_delta(u_b, s2_b, s4_b, s8_b, tl, t * tl)
    aout = _pool_mix(d, wpool, pscale[...])
    u_b[16:HIST_BASE, :] = u_b[16 + tl:HIST_BASE + tl, :]
    merged = gate_a * _bdot(aout, wpa[...]) + gate_b * _bdot(_load_blocks(bo_s), wpb[...])
    x1 = x_ref[...] + mod_ref[2] * _bdot(merged.astype(BF16), wout[...])
    x1_s[...] = x1

    @pl.when(t == n_tiles - 1)
    def _():
        st_ref[...] = st_s[...]
        hist_ref[...] = u_b[16:HIST_BASE, :]


def _sample_mixer_kernel(x_ref, mod_ref, s0_ref, cache_ref, g1, wu, wq, wk, wv, wg, walr, wga, wgb,
                         walpha, balpha, wpool, pscale, gng, wpa, wpb, wout,
                         x1_ref, st_ref, hist_ref,
                         u_b, s2_b, s4_b, s8_b, h_s, q_s, k_s, v_s, sg_s, b_s, bo_s, u_s, ao_s):
    s = pl.program_id(0)
    seq = q_s.shape[1]
    group = s0_ref.shape[0]

    @pl.when(s == 0)
    def _():
        u_b[0:HIST_BASE, :] = jnp.zeros((HIST_BASE, POOL_WIDTH), F32)
        s2_b[0:16, :] = jnp.zeros((16, POOL_WIDTH), F32)
        _project_gla(x_ref[...], mod_ref, g1, wq, wk, wv, wg, walr, walpha, balpha, seq,
                     h_s, q_s, k_s, v_s, sg_s, b_s)
        _store_blocks(u_s, _proj(h_s[...], wu))

    for i in range(group):
        blk = s * group + i
        u_b[HIST_BASE - POOL_HIST:HIST_BASE, :] = cache_ref[i]
        u_b[HIST_BASE:HIST_BASE + seq, :] = u_s[blk]
        d = _pool_delta(u_b, s2_b, s4_b, s8_b, seq, PAST_LEN)
        ao_s[blk] = _pool_mix(d, wpool, pscale[...])
        hist_ref[i] = u_b[HIST_BASE + seq - POOL_HIST:HIST_BASE + seq, :]

        def get_state(h, i=i):
            return s0_ref[i, h]

        def put_state(h, s_new, i=i):
            st_ref[i, h] = s_new

        _gla_heads(blk, q_s, k_s, b_s, v_s, sg_s, gng[...], bo_s, get_state, put_state)

    @pl.when(s == pl.num_programs(0) - 1)
    def _():
        merged_a = _gated_pool(h_s, _load_blocks(ao_s), wga, wpa)
        x1_ref[...] = _merge_out(x_ref[...], mod_ref[2], merged_a, _gate_b(h_s, wgb),
                                 _load_blocks(bo_s), wpb, wout)


def _ffn_kernel(x_ref, mod_ref, g2, w1, w2, gf, o_ref):
    x = x_ref[...]
    h2 = _norm_mod(x, g2[...], mod_ref[0], mod_ref[1])
    acc = jnp.zeros(x.shape, F32)
    for c in range(D_FF // FF_CHUNK):
        sl = slice(c * FF_CHUNK, (c + 1) * FF_CHUNK)
        y = jnp.maximum(_bdot(h2, w1[:, sl]), 0.0)
        acc = acc + _bdot((y * y).astype(BF16), w2[sl, :])
    x2 = x + _rows_per_seq(mod_ref[2], x.shape[0]) * acc
    o_ref[...] = _rms(x2) * gf[...]


def _const_spec(shape):
    nd = len(shape)
    return pl.BlockSpec(shape, lambda *_: (0,) * nd, pipeline_mode=pl.Buffered(1))


def _params(vmem_limit=None):
    return pltpu.CompilerParams(dimension_semantics=("arbitrary",),
                                vmem_limit_bytes=vmem_limit or VMEM_LIMIT)


def _mixer_scratch(nblk, rows, hist_rows, h_slots=None):
    h_shape = (nblk * rows, D_MODEL) if h_slots is None else (h_slots, nblk * rows, D_MODEL)
    return [
        pltpu.VMEM((hist_rows, POOL_WIDTH), F32),
        pltpu.VMEM((hist_rows, POOL_WIDTH), F32),
        pltpu.VMEM((hist_rows, POOL_WIDTH - 128), F32),
        pltpu.VMEM((hist_rows, POOL_WIDTH - 256), F32),
        pltpu.VMEM(h_shape, BF16),
        pltpu.VMEM((nblk, rows, GLA_DK), F32),
        pltpu.VMEM((nblk, rows, GLA_DK), F32),
        pltpu.VMEM((nblk, rows, GLA_DV), F32),
        pltpu.VMEM((nblk, rows, GLA_DV), F32),
        pltpu.VMEM((nblk, rows, GLA_DK), F32),
        pltpu.VMEM((nblk, rows, GLA_DV), BF16),
    ]


def _adaln(c_all, w_ada, b_ada):
    rows = c_all.shape[0]
    n = w_ada.shape[1]
    return pl.pallas_call(
        _adaln_kernel,
        grid=(n // ADA_BLOCK,),
        in_specs=[pl.BlockSpec((rows, D_MODEL), lambda j: (0, 0)),
                  pl.BlockSpec((D_MODEL, ADA_BLOCK), lambda j: (0, j)),
                  pl.BlockSpec((1, ADA_BLOCK), lambda j: (0, j))],
        out_specs=pl.BlockSpec((rows, ADA_BLOCK), lambda j: (0, j)),
        out_shape=jax.ShapeDtypeStruct((rows, n), F32),
        compiler_params=_params(),
        name="adaln_mod",
    )(c_all, w_ada, b_ada)


def _prompt_layer(x, mod, weights, ffn_weights):
    n_tok = x.shape[0]
    tl = TL_MIX
    n_tiles = n_tok // tl
    consts = (mod,) + tuple(weights) + tuple(ffn_weights)
    return pl.pallas_call(
        _prompt_layer_kernel,
        grid=(n_tiles + 1,),
        in_specs=[pl.BlockSpec((tl, D_MODEL), lambda t: (jnp.minimum(t, n_tiles - 1), 0)),
                  pl.BlockSpec((tl, D_MODEL), lambda t: (jnp.minimum(t + 1, n_tiles - 1), 0))]
        + [_const_spec(w.shape) for w in consts],
        out_specs=[pl.BlockSpec((tl, D_MODEL), lambda t: (jnp.maximum(t - 1, 0), 0)),
                   pl.BlockSpec((GLA_HEADS, GLA_HK, GLA_HV), lambda t: (0, 0, 0)),
                   pl.BlockSpec((16, POOL_WIDTH), lambda t: (0, 0))],
        out_shape=[jax.ShapeDtypeStruct((n_tok, D_MODEL), F32),
                   jax.ShapeDtypeStruct((GLA_HEADS, GLA_HK, GLA_HV), F32),
                   jax.ShapeDtypeStruct((16, POOL_WIDTH), F32)],
        scratch_shapes=_mixer_scratch(tl // CHUNK, CHUNK, HIST_BASE + tl, h_slots=2) + [
            pltpu.VMEM((GLA_HEADS, GLA_HK, GLA_HV), F32),
            pltpu.VMEM((tl, D_MODEL), F32),
            pltpu.VMEM((tl, D_MODEL), BF16),
            pltpu.VMEM((tl, FF_CHUNK), BF16),
            pltpu.VMEM((tl, D_MODEL), F32),
        ],
        compiler_params=_params(VMEM_LIMIT_LAYER),
        name="prompt_layer",
    )(x, x, *consts)


def _sample_mixer(x, mod, s0, cache, weights):
    n_tok = x.shape[0]
    n_seq = s0.shape[0]
    seq = n_tok // n_seq
    group = SEQ_GROUP
    w_specs = [_const_spec(w.shape) for w in weights]
    st_spec = pl.BlockSpec((group, GLA_HEADS, GLA_HK, GLA_HV), lambda s: (s, 0, 0, 0))
    hist_spec = pl.BlockSpec((group, POOL_HIST, POOL_WIDTH), lambda s: (s, 0, 0))
    return pl.pallas_call(
        _sample_mixer_kernel,
        grid=(n_seq // group,),
        in_specs=[_const_spec(x.shape), _const_spec(mod.shape), st_spec, hist_spec] + w_specs,
        out_specs=[pl.BlockSpec((n_tok, D_MODEL), lambda s: (0, 0)), st_spec, hist_spec],
        out_shape=[jax.ShapeDtypeStruct((n_tok, D_MODEL), F32),
                   jax.ShapeDtypeStruct(s0.shape, F32),
                   jax.ShapeDtypeStruct(cache.shape, F32)],
        scratch_shapes=_mixer_scratch(n_seq, seq, HIST_BASE + seq) + [
            pltpu.VMEM((n_seq, seq, POOL_WIDTH), F32),
            pltpu.VMEM((n_seq, seq, POOL_WIDTH), BF16),
        ],
        compiler_params=_params(),
        name="sample_mixer",
    )(x, mod, s0, cache, *weights)


def _ffn_final(x, mod, g2, w1, w2, gf):
    n_tok = x.shape[0]
    tl = min(TL_FFN, n_tok)
    assert mod.shape[1] == 1 or n_tok == tl
    return pl.pallas_call(
        _ffn_kernel,
        grid=(n_tok // tl,),
        in_specs=[pl.BlockSpec((tl, D_MODEL), lambda t: (t, 0)), _const_spec(mod.shape),
                  _const_spec(g2.shape), _const_spec(w1.shape), _const_spec(w2.shape),
                  _const_spec(gf.shape)],
        out_specs=pl.BlockSpec((tl, D_MODEL), lambda t: (t, 0)),
        out_shape=jax.ShapeDtypeStruct((n_tok, D_MODEL), F32),
        compiler_params=_params(),
        name="ffn_final",
    )(x, mod, g2, w1, w2, gf)


def kernel(x_prompt, x_sample, c_prompt, c_sample, state_gla, cache_pool, w_ada, b_ada, norm1_g,
           w_in, w_alpha, b_alpha, w_pool, pool_scale, gla_norm_g, w_pa, w_pb, w_out, norm2_g,
           w_ff1, w_ff2, final_g):
    n_batch, n_seq_p, _ = x_prompt.shape
    n_dec, n_seq_s, _ = x_sample.shape
    assert n_batch == 1 and w_ada.shape[0] == 1
    assert n_seq_p % TL_MIX == 0 and SLAB % n_seq_s == 0 and n_seq_s % SUB == 0
    assert n_dec % SEQ_GROUP == 0

    n_c = n_batch + n_dec
    pad = (-n_c) % 8
    c_all = jnp.concatenate([c_prompt, c_sample, jnp.zeros((pad, D_MODEL), F32)], axis=0)
    mod = _adaln(c_all, w_ada[0], b_ada)
    mod = mod.reshape(n_c + pad, 6, D_MODEL).transpose(1, 0, 2)
    mod_p = mod[:, 0:1]
    mod_s = mod[:, n_batch:n_c]

    offs = [0]
    for sz in IN_SIZES:
        offs.append(offs[-1] + sz)
    w_in_t = jnp.swapaxes(w_in[0], 0, 1).astype(BF16)
    wu, wq, wk, wv, wg, walr, wga, wgb = [w_in_t[offs[i]:offs[i + 1]] for i in range(8)]
    walr = jnp.pad(walr, ((0, LANES - GLA_LOWRANK), (0, 0)))
    walpha = jnp.pad(w_alpha[0].astype(BF16), ((0, LANES - GLA_LOWRANK), (0, 0)))
    weights = (norm1_g, wu, wq, wk, wv, wg, walr, wga, wgb,
               walpha, b_alpha, w_pool[0].astype(BF16), pool_scale, gla_norm_g,
               w_pa[0].astype(BF16), w_pb[0].astype(BF16), w_out[0].astype(BF16))

    w1 = w_ff1[0].astype(BF16)
    w2 = w_ff2[0].astype(BF16)
    gf = final_g.reshape(1, D_MODEL)

    y_p, st_p, hist_p = _prompt_layer(x_prompt[0], mod_p, weights, (norm2_g, w1, w2, gf))
    x1_s, st_s, hist_s = _sample_mixer(x_sample.reshape(n_dec * n_seq_s, D_MODEL), mod_s[0:3],
                                       state_gla[0], cache_pool[0], weights)
    y_s = _ffn_final(x1_s, mod_s[3:6], norm2_g, w1, w2, gf)

    return (y_p[None], y_s.reshape(n_dec, n_seq_s, D_MODEL), st_p[None, None],
            hist_p[None, None, 1:], st_s[None], hist_s[None])
```

```python
import jax
import jax.numpy as jnp
from jax import lax
from jax.experimental import pallas as pl
from jax.experimental.pallas import tpu as pltpu

D_MODEL = 1024
PAST_LEN = 4096
POOL_WIDTH = 512
POOL_WINDOWS = (2, 4, 8, 16)
POOL_GD = 128
POOL_HIST = 15
GLA_HEADS = 4
GLA_DK = 512
GLA_DV = 1024
GLA_HK = 128
GLA_HV = 256
GLA_LOWRANK = 16
GLA_GATE_NORM = 16.0
D_FF = 4096
EPS = 1e-6
LOG2_E = 1.4426950408889634
LANES = 128
IN_SIZES = (POOL_WIDTH, GLA_DK, GLA_DK, GLA_DV, GLA_DV, GLA_LOWRANK, D_MODEL, D_MODEL)

SUB = 8
CHUNK = 64
SLAB = 64
HIST_BASE = 32
TL_MIX = 256
TL_FFN = 512
FF_CHUNK = 1024
ADA_BLOCK = 1536
SEQ_GROUP = 4
VMEM_LIMIT = 48 * 1024 * 1024
VMEM_LIMIT_LAYER = 60 * 1024 * 1024

F32 = jnp.float32
BF16 = jnp.bfloat16


def _bdot(a, b):
    return jnp.dot(a, b, preferred_element_type=F32)


def _proj(h, wt_ref):
    return lax.dot_general(h, wt_ref[...], (((1,), (1,)), ((), ())), preferred_element_type=F32)


def _rms(xf):
    return xf * lax.rsqrt(jnp.mean(xf * xf, axis=-1, keepdims=True) + EPS)


def _sigmoid(x):
    return 0.5 * jnp.tanh(0.5 * x) + 0.5


def _log_sigmoid(z):
    return jnp.minimum(z, 0.0) - jnp.log(1.0 + jnp.exp(-jnp.abs(z)))


def _rows_per_seq(m, n_rows):
    n_seq = m.shape[0]
    if n_seq == 1:
        return m
    rep = n_rows // n_seq
    return jnp.concatenate([jnp.broadcast_to(m[i:i + 1], (rep, m.shape[1])) for i in range(n_seq)],
                           axis=0)


def _norm_mod(x, g, shift, scale):
    n = x.shape[0]
    return (_rms(x) * g * (1.0 + _rows_per_seq(scale, n)) + _rows_per_seq(shift, n)).astype(BF16)


def _store_blocks(ref, val):
    nblk, rows = ref.shape[0], ref.shape[1]
    for i in range(nblk):
        ref[i] = val[i * rows:(i + 1) * rows].astype(ref.dtype)


def _load_blocks(ref):
    return jnp.concatenate([ref[i] for i in range(ref.shape[0])], axis=0)


def _cumsum_rows(a, period):
    n = a.shape[0]
    ri = lax.broadcasted_iota(jnp.int32, (n, n), 0)
    ci = lax.broadcasted_iota(jnp.int32, (n, n), 1)
    shift = period.bit_length() - 1
    same = jnp.right_shift(ri, shift) == jnp.right_shift(ci, shift)
    tri = jnp.where((ci <= ri) & same, 1.0, 0.0).astype(BF16)
    hi = a.astype(BF16)
    r1 = a - hi.astype(F32)
    mid = r1.astype(BF16)
    lo = (r1 - mid.astype(F32)).astype(BF16)
    return _bdot(tri, hi) + _bdot(tri, mid) + _bdot(tri, lo)


def _gla_scores(q, k, b, k_row, b_row, nsub):
    L = SUB * nsub
    blast = b_row(L - 1)
    qt = (q * jnp.exp2(b)).astype(BF16)
    kt = (k * jnp.exp2(blast - b)).astype(BF16)

    lane = lax.broadcasted_iota(jnp.int32, (SUB, L), 1)
    row = lax.broadcasted_iota(jnp.int32, (SUB, L), 0)
    qs = [q[SUB * i:SUB * (i + 1)] for i in range(nsub)]
    ks = [k[SUB * i:SUB * (i + 1)] for i in range(nsub)]
    bs = [b[SUB * i:SUB * (i + 1)] for i in range(nsub)]

    diag = []
    for i in range(nsub):
        acc = jnp.zeros((SUB, L), F32)
        for j in range(SUB):
            r = SUB * i + j
            dec = jnp.exp2(bs[i] - b_row(r))
            col = jnp.sum(qs[i] * dec * k_row(r), axis=1, keepdims=True)
            acc = jnp.where(lane == r, col, acc)
        diag.append(jnp.where(lane <= row + SUB * i, acc, 0.0))
    p = diag[0] if nsub == 1 else jnp.concatenate(diag, axis=0)

    p_off = None
    if nsub > 1:
        zero = jnp.zeros((SUB, GLA_HK), F32)
        lhs, rhs = [], []
        for j in range(nsub - 1):
            bend = b_row(SUB * j + SUB - 1)
            lrows = [zero if i <= j else qs[i] * jnp.exp2(bs[i] - bend) for i in range(nsub)]
            rrows = [ks[j] * jnp.exp2(bend - bs[j]) if i == j else zero for i in range(nsub)]
            lhs.append(jnp.concatenate(lrows, axis=0).astype(BF16))
            rhs.append(jnp.concatenate(rrows, axis=0).astype(BF16))
        lhs = jnp.concatenate(lhs, axis=1)
        rhs = jnp.concatenate(rhs, axis=1)
        p_off = lax.dot_general(lhs, rhs, (((1,), (1,)), ((), ())), preferred_element_type=F32)
    return qt, kt, p, p_off, blast


def _gla_apply(scores, v, s):
    qt, kt, p, p_off, blast = scores
    if p_off is not None:
        p = p + p_off
    o = _bdot(jnp.concatenate([qt, p.astype(BF16)], axis=1),
              jnp.concatenate([s.astype(BF16), v], axis=0))

    ri = lax.broadcasted_iota(jnp.int32, (GLA_HK, GLA_HK), 0)
    ci = lax.broadcasted_iota(jnp.int32, (GLA_HK, GLA_HK), 1)
    erow = jnp.broadcast_to(jnp.exp2(blast), (GLA_HK, GLA_HK))
    ecol = jnp.sum(jnp.where(ri == ci, erow, 0.0), axis=1, keepdims=True)
    s_new = s * ecol + lax.dot_general(kt, v, (((0,), (0,)), ((), ())), preferred_element_type=F32)
    return o, s_new


def _gla_head_scores(blk, h, q_s, k_s, b_s):
    ksl = slice(h * GLA_HK, (h + 1) * GLA_HK)
    k_row = lambda r: k_s[blk, r:r + 1, ksl]
    b_row = lambda r: b_s[blk, r:r + 1, ksl]
    return _gla_scores(q_s[blk, :, ksl], k_s[blk, :, ksl], b_s[blk, :, ksl], k_row, b_row,
                       q_s.shape[1] // SUB)


def _gla_head_apply(blk, h, scores, v_s, sg_s, gng, bo_s, get_state, put_state):
    vsl = slice(h * GLA_HV, (h + 1) * GLA_HV)
    o, s_new = _gla_apply(scores, v_s[blk, :, vsl].astype(BF16), get_state(h))
    put_state(h, s_new)
    o = _rms(o) * gng
    bo_s[blk, :, vsl] = (o * sg_s[blk, :, vsl]).astype(BF16)


def _gla_heads(blk, q_s, k_s, b_s, v_s, sg_s, gng, bo_s, get_state, put_state):
    for h in range(GLA_HEADS):
        scores = _gla_head_scores(blk, h, q_s, k_s, b_s)
        _gla_head_apply(blk, h, scores, v_s, sg_s, gng, bo_s, get_state, put_state)


def _pool_delta(u_b, s2_b, s4_b, s8_b, n, pos0):
    r = HIST_BASE + n
    s2_b[16:r, :] = u_b[16:r, :] + u_b[15:r - 1, :]
    s4_b[16:r, :] = s2_b[16:r, 128:512] + s2_b[14:r - 2, 128:512]
    s8_b[24:r, :] = s4_b[24:r, 128:384] + s4_b[20:r - 4, 128:384]
    s16 = s8_b[HIST_BASE:r, 128:256] + s8_b[HIST_BASE - 8:r - 8, 128:256]
    sums = (s2_b[HIST_BASE:r, 0:128], s4_b[HIST_BASE:r, 0:128], s8_b[HIST_BASE:r, 0:128], s16)
    pos1 = pos0 + lax.broadcasted_iota(jnp.int32, (n, 1), 0) + 1
    out = []
    for gi, w in enumerate(POOL_WINDOWS):
        cnt = jnp.minimum(pos1, w).astype(F32)
        out.append(sums[gi] / cnt - u_b[HIST_BASE:r, gi * POOL_GD:(gi + 1) * POOL_GD])
    return out


def _pool_mix(d, wpool, pscale):
    mixed = [_bdot(d[gi].astype(BF16), wpool[gi]) for gi in range(len(POOL_WINDOWS))]
    return (jnp.concatenate(mixed, axis=1) * pscale).astype(BF16)


def _project_gla(x, mod_ref, g1, wq, wk, wv, wg, walr, walpha, balpha, period,
                 h_s, q_s, k_s, v_s, sg_s, b_s):
    h = _norm_mod(x, g1[...], mod_ref[0], mod_ref[1])
    h_s[...] = h
    _store_blocks(q_s, _proj(h, wq) * (GLA_HK ** -0.5))
    _store_blocks(k_s, _proj(h, wk))
    _store_blocks(v_s, _proj(h, wv))
    g = _proj(h, wg)
    _store_blocks(sg_s, g * _sigmoid(g))
    alr = _proj(h, walr)
    z = _bdot(alr.astype(BF16), walpha[...]) + balpha[...]
    a = _log_sigmoid(z) * (LOG2_E / GLA_GATE_NORM)
    b = [_cumsum_rows(a[r:r + SLAB], period) for r in range(0, a.shape[0], SLAB)]
    _store_blocks(b_s, jnp.concatenate(b, axis=0))


def _gated_pool(h_s, aout, wga, wpa):
    return _sigmoid(_proj(h_s[...], wga)) * _bdot(aout, wpa[...])


def _gate_b(h_s, wgb):
    return _sigmoid(_proj(h_s[...], wgb))


def _merge_out(x, gate1, merged_a, gate_b, bo, wpb, wout):
    merged = merged_a + gate_b * _bdot(bo, wpb[...])
    y = _bdot(merged.astype(BF16), wout[...])
    return x + _rows_per_seq(gate1, x.shape[0]) * y


def _ffn_tile(x, shift, scale, gate, g2, w1, w2, gf, o_ref):
    h2 = _norm_mod(x, g2[...], shift, scale)
    acc = jnp.zeros(x.shape, F32)
    for c in range(D_FF // FF_CHUNK):
        sl = slice(c * FF_CHUNK, (c + 1) * FF_CHUNK)
        y = jnp.maximum(_bdot(h2, w1[:, sl]), 0.0)
        acc = acc + _bdot((y * y).astype(BF16), w2[sl, :])
    x2 = x + _rows_per_seq(gate, x.shape[0]) * acc
    o_ref[...] = _rms(x2) * gf[...]


def _adaln_kernel(c_ref, w_ref, b_ref, o_ref):
    c = c_ref[...]
    sc = (c * _sigmoid(c)).astype(BF16)
    o_ref[...] = _bdot(sc, w_ref[...].astype(BF16)) + b_ref[...]


def _prompt_layer_kernel(x_ref, xn_ref, mod_ref, g1, wu, wq, wk, wv, wg, walr, wga, wgb, walpha, balpha,
                         wpool, pscale, gng, wpa, wpb, wout, g2, w1, w2, gf,
                         y_ref, st_ref, hist_ref,
                         u_b, s2_b, s4_b, s8_b, hh_s, q_s, k_s, v_s, sg_s, b_s, bo_s,
                         st_s, x1_s, h2_s, y_s, acc_s):
    t = pl.program_id(0)
    n_tiles = pl.num_programs(0) - 1
    tl = x_ref.shape[0]
    n_chunks = tl // CHUNK
    n_piece = GLA_HEADS
    piece = D_MODEL // n_piece
    assert n_chunks * FF_CHUNK == w1.shape[1] and FF_CHUNK == n_piece * piece

    @pl.when(t == 0)
    def _():
        st_s[...] = jnp.zeros_like(st_s)
        u_b[0:HIST_BASE, :] = jnp.zeros((HIST_BASE, POOL_WIDTH), F32)
        s2_b[0:16, :] = jnp.zeros((16, POOL_WIDTH), F32)
        x1_s[...] = jnp.zeros_like(x1_s)
        hh_s[0] = _norm_mod(x_ref[...], g1[...], mod_ref[0], mod_ref[1])

    @pl.when(t < n_tiles)
    def _():
        slot = lax.rem(t, 2)
        alr = _proj(hh_s[slot], walr)
        _store_blocks(k_s, _proj(hh_s[slot], wk))
        h2_s[...] = _norm_mod(x1_s[...], g2[...], mod_ref[3], mod_ref[4])
        z = _bdot(alr.astype(BF16), walpha[...]) + balpha[...]
        a = _log_sigmoid(z) * (LOG2_E / GLA_GATE_NORM)
        _store_blocks(q_s, _proj(hh_s[slot], wq) * (GLA_HK ** -0.5))
        b = [_cumsum_rows(a[r:r + SLAB], CHUNK) for r in range(0, tl, SLAB)]
        _store_blocks(b_s, jnp.concatenate(b, axis=0))
        _store_blocks(v_s, _proj(hh_s[slot], wv))
        g = _proj(hh_s[slot], wg)
        _store_blocks(sg_s, g * _sigmoid(g))

        def get_state(h):
            return st_s[h]

        def put_state(h, s_new):
            st_s[h] = s_new

        acc_s[...] = jnp.zeros_like(acc_s)

        def chunk_body(c):
            def scores(h):
                return _gla_head_scores(c, h, q_s, k_s, b_s)

            def apply(h, sc):
                _gla_head_apply(c, h, sc, v_s, sg_s, gng[...], bo_s, get_state, put_state)

            def ffn_up(j):
                cols = slice(j * piece, (j + 1) * piece)
                w1_cols = slice(c * FF_CHUNK + j * piece, c * FF_CHUNK + (j + 1) * piece)
                y = jnp.maximum(_bdot(h2_s[...], w1[:, w1_cols]), 0.0)
                y_s[:, cols] = (y * y).astype(BF16)

            def ffn_down(j):
                cols = slice(j * piece, (j + 1) * piece)
                acc_s[:, cols] += _bdot(y_s[...], w2[c * FF_CHUNK:(c + 1) * FF_CHUNK, cols])

            sc0 = scores(0)
            ffn_up(0)
            sc1 = scores(1)
            ffn_up(1)
            apply(0, sc0)
            ffn_up(2)
            sc2 = scores(2)
            ffn_up(3)
            apply(1, sc1)
            ffn_down(0)
            sc3 = scores(3)
            ffn_down(1)
            apply(2, sc2)
            ffn_down(2)
            apply(3, sc3)
            ffn_down(3)

        for c in range(n_chunks):
            chunk_body(c)

        x2 = x1_s[...] + mod_ref[5] * acc_s[...]
        y_ref[...] = _rms(x2) * gf[...]

        hh_s[1 - slot] = _norm_mod(xn_ref[...], g1[...], mod_ref[0], mod_ref[1])

        u_b[HIST_BASE:HIST_BASE + tl, :] = _proj(hh_s[slot], wu)
        gate_a = _sigmoid(_proj(hh_s[slot], wga))
        gate_b = _sigmoid(_proj(hh_s[slot], wgb))
        d = _pool_delta(u_b, s2_b, s4_b, s8_b, tl, t * tl)
        aout = _pool_mix(d, wpool, pscale[...])
        u_b[16:HIST_BASE, :] = u_b[16 + tl:HIST_BASE + tl, :]
        merged = gate_a * _bdot(aout, wpa[...]) + gate_b * _bdot(_load_blocks(bo_s), wpb[...])
        x1 = x_ref[...] + mod_ref[2] * _bdot(merged.astype(BF16), wout[...])
        x1_s[...] = x1

    @pl.when(t == n_tiles - 1)
    def _():
        st_ref[...] = st_s[...]
        hist_ref[...] = u_b[16:HIST_BASE, :]

    @pl.when(t == n_tiles)
    def _():
        _ffn_tile(x1_s[...], mod_ref[3], mod_ref[4], mod_ref[5], g2, w1, w2, gf, y_ref)


def _sample_mixer_kernel(x_ref, mod_ref, s0_ref, cache_ref, g1, wu, wq, wk, wv, wg, walr, wga, wgb,
                         walpha, balpha, wpool, pscale, gng, wpa, wpb, wout,
                         x1_ref, st_ref, hist_ref,
                         u_b, s2_b, s4_b, s8_b, h_s, q_s, k_s, v_s, sg_s, b_s, bo_s, u_s, ao_s):
    s = pl.program_id(0)
    seq = q_s.shape[1]
    group = s0_ref.shape[0]

    @pl.when(s == 0)
    def _():
        u_b[0:HIST_BASE, :] = jnp.zeros((HIST_BASE, POOL_WIDTH), F32)
        s2_b[0:16, :] = jnp.zeros((16, POOL_WIDTH), F32)
        _project_gla(x_ref[...], mod_ref, g1, wq, wk, wv, wg, walr, walpha, balpha, seq,
                     h_s, q_s, k_s, v_s, sg_s, b_s)
        _store_blocks(u_s, _proj(h_s[...], wu))

    for i in range(group):
        blk = s * group + i
        u_b[HIST_BASE - POOL_HIST:HIST_BASE, :] = cache_ref[i]
        u_b[HIST_BASE:HIST_BASE + seq, :] = u_s[blk]
        d = _pool_delta(u_b, s2_b, s4_b, s8_b, seq, PAST_LEN)
        ao_s[blk] = _pool_mix(d, wpool, pscale[...])
        hist_ref[i] = u_b[HIST_BASE + seq - POOL_HIST:HIST_BASE + seq, :]

        def get_state(h, i=i):
            return s0_ref[i, h]

        def put_state(h, s_new, i=i):
            st_ref[i, h] = s_new

        _gla_heads(blk, q_s, k_s, b_s, v_s, sg_s, gng[...], bo_s, get_state, put_state)

    @pl.when(s == pl.num_programs(0) - 1)
    def _():
        merged_a = _gated_pool(h_s, _load_blocks(ao_s), wga, wpa)
        x1_ref[...] = _merge_out(x_ref[...], mod_ref[2], merged_a, _gate_b(h_s, wgb),
                                 _load_blocks(bo_s), wpb, wout)


def _ffn_kernel(x_ref, mod_ref, g2, w1, w2, gf, o_ref):
    _ffn_tile(x_ref[...], mod_ref[0], mod_ref[1], mod_ref[2], g2, w1, w2, gf, o_ref)


def _const_spec(shape):
    nd = len(shape)
    return pl.BlockSpec(shape, lambda *_: (0,) * nd, pipeline_mode=pl.Buffered(1))


def _params(vmem_limit=None):
    return pltpu.CompilerParams(dimension_semantics=("arbitrary",),
                                vmem_limit_bytes=vmem_limit or VMEM_LIMIT)


def _mixer_scratch(nblk, rows, hist_rows, h_slots=None):
    h_shape = (nblk * rows, D_MODEL) if h_slots is None else (h_slots, nblk * rows, D_MODEL)
    return [
        pltpu.VMEM((hist_rows, POOL_WIDTH), F32),
        pltpu.VMEM((hist_rows, POOL_WIDTH), F32),
        pltpu.VMEM((hist_rows, POOL_WIDTH - 128), F32),
        pltpu.VMEM((hist_rows, POOL_WIDTH - 256), F32),
        pltpu.VMEM(h_shape, BF16),
        pltpu.VMEM((nblk, rows, GLA_DK), F32),
        pltpu.VMEM((nblk, rows, GLA_DK), F32),
        pltpu.VMEM((nblk, rows, GLA_DV), F32),
        pltpu.VMEM((nblk, rows, GLA_DV), F32),
        pltpu.VMEM((nblk, rows, GLA_DK), F32),
        pltpu.VMEM((nblk, rows, GLA_DV), BF16),
    ]


def _adaln(c_all, w_ada, b_ada):
    rows = c_all.shape[0]
    n = w_ada.shape[1]
    return pl.pallas_call(
        _adaln_kernel,
        grid=(n // ADA_BLOCK,),
        in_specs=[pl.BlockSpec((rows, D_MODEL), lambda j: (0, 0)),
                  pl.BlockSpec((D_MODEL, ADA_BLOCK), lambda j: (0, j)),
                  pl.BlockSpec((1, ADA_BLOCK), lambda j: (0, j))],
        out_specs=pl.BlockSpec((rows, ADA_BLOCK), lambda j: (0, j)),
        out_shape=jax.ShapeDtypeStruct((rows, n), F32),
        compiler_params=_params(),
        name="adaln_mod",
    )(c_all, w_ada, b_ada)


def _prompt_layer(x, mod, weights, ffn_weights):
    n_tok = x.shape[0]
    tl = TL_MIX
    n_tiles = n_tok // tl
    consts = (mod,) + tuple(weights) + tuple(ffn_weights)
    return pl.pallas_call(
        _prompt_layer_kernel,
        grid=(n_tiles + 1,),
        in_specs=[pl.BlockSpec((tl, D_MODEL), lambda t: (jnp.minimum(t, n_tiles - 1), 0)),
                  pl.BlockSpec((tl, D_MODEL), lambda t: (jnp.minimum(t + 1, n_tiles - 1), 0))]
        + [_const_spec(w.shape) for w in consts],
        out_specs=[pl.BlockSpec((tl, D_MODEL), lambda t: (jnp.maximum(t - 1, 0), 0)),
                   pl.BlockSpec((GLA_HEADS, GLA_HK, GLA_HV), lambda t: (0, 0, 0)),
                   pl.BlockSpec((16, POOL_WIDTH), lambda t: (0, 0))],
        out_shape=[jax.ShapeDtypeStruct((n_tok, D_MODEL), F32),
                   jax.ShapeDtypeStruct((GLA_HEADS, GLA_HK, GLA_HV), F32),
                   jax.ShapeDtypeStruct((16, POOL_WIDTH), F32)],
        scratch_shapes=_mixer_scratch(tl // CHUNK, CHUNK, HIST_BASE + tl, h_slots=2) + [
            pltpu.VMEM((GLA_HEADS, GLA_HK, GLA_HV), F32),
            pltpu.VMEM((tl, D_MODEL), F32),
            pltpu.VMEM((tl, D_MODEL), BF16),
            pltpu.VMEM((tl, FF_CHUNK), BF16),
            pltpu.VMEM((tl, D_MODEL), F32),
        ],
        compiler_params=_params(VMEM_LIMIT_LAYER),
        name="prompt_layer",
    )(x, x, *consts)


def _sample_mixer(x, mod, s0, cache, weights):
    n_tok = x.shape[0]
    n_seq = s0.shape[0]
    seq = n_tok // n_seq
    group = SEQ_GROUP
    w_specs = [_const_spec(w.shape) for w in weights]
    st_spec = pl.BlockSpec((group, GLA_HEADS, GLA_HK, GLA_HV), lambda s: (s, 0, 0, 0))
    hist_spec = pl.BlockSpec((group, POOL_HIST, POOL_WIDTH), lambda s: (s, 0, 0))
    return pl.pallas_call(
        _sample_mixer_kernel,
        grid=(n_seq // group,),
        in_specs=[_const_spec(x.shape), _const_spec(mod.shape), st_spec, hist_spec] + w_specs,
        out_specs=[pl.BlockSpec((n_tok, D_MODEL), lambda s: (0, 0)), st_spec, hist_spec],
        out_shape=[jax.ShapeDtypeStruct((n_tok, D_MODEL), F32),
                   jax.ShapeDtypeStruct(s0.shape, F32),
                   jax.ShapeDtypeStruct(cache.shape, F32)],
        scratch_shapes=_mixer_scratch(n_seq, seq, HIST_BASE + seq) + [
            pltpu.VMEM((n_seq, seq, POOL_WIDTH), F32),
            pltpu.VMEM((n_seq, seq, POOL_WIDTH), BF16),
        ],
        compiler_params=_params(),
        name="sample_mixer",
    )(x, mod, s0, cache, *weights)


def _ffn_final(x, mod, g2, w1, w2, gf):
    n_tok = x.shape[0]
    tl = min(TL_FFN, n_tok)
    assert mod.shape[1] == 1 or n_tok == tl
    return pl.pallas_call(
        _ffn_kernel,
        grid=(n_tok // tl,),
        in_specs=[pl.BlockSpec((tl, D_MODEL), lambda t: (t, 0)), _const_spec(mod.shape),
                  _const_spec(g2.shape), _const_spec(w1.shape), _const_spec(w2.shape),
                  _const_spec(gf.shape)],
        out_specs=pl.BlockSpec((tl, D_MODEL), lambda t: (t, 0)),
        out_shape=jax.ShapeDtypeStruct((n_tok, D_MODEL), F32),
        compiler_params=_params(),
        name="ffn_final",
    )(x, mod, g2, w1, w2, gf)


def kernel(x_prompt, x_sample, c_prompt, c_sample, state_gla, cache_pool, w_ada, b_ada, norm1_g,
           w_in, w_alpha, b_alpha, w_pool, pool_scale, gla_norm_g, w_pa, w_pb, w_out, norm2_g,
           w_ff1, w_ff2, final_g):
    n_batch, n_seq_p, _ = x_prompt.shape
    n_dec, n_seq_s, _ = x_sample.shape
    assert n_batch == 1 and w_ada.shape[0] == 1
    assert n_seq_p % TL_MIX == 0 and SLAB % n_seq_s == 0 and n_seq_s % SUB == 0
    assert n_dec % SEQ_GROUP == 0

    n_c = n_batch + n_dec
    pad = (-n_c) % 8
    c_all = jnp.concatenate([c_prompt, c_sample, jnp.zeros((pad, D_MODEL), F32)], axis=0)
    mod = _adaln(c_all, w_ada[0], b_ada)
    mod = mod.reshape(n_c + pad, 6, D_MODEL).transpose(1, 0, 2)
    mod_p = mod[:, 0:1]
    mod_s = mod[:, n_batch:n_c]

    offs = [0]
    for sz in IN_SIZES:
        offs.append(offs[-1] + sz)
    w_in_t = jnp.swapaxes(w_in[0], 0, 1).astype(BF16)
    wu, wq, wk, wv, wg, walr, wga, wgb = [w_in_t[offs[i]:offs[i + 1]] for i in range(8)]
    walr = jnp.pad(walr, ((0, LANES - GLA_LOWRANK), (0, 0)))
    walpha = jnp.pad(w_alpha[0].astype(BF16), ((0, LANES - GLA_LOWRANK), (0, 0)))
    weights = (norm1_g, wu, wq, wk, wv, wg, walr, wga, wgb,
               walpha, b_alpha, w_pool[0].astype(BF16), pool_scale, gla_norm_g,
               w_pa[0].astype(BF16), w_pb[0].astype(BF16), w_out[0].astype(BF16))

    w1 = w_ff1[0].astype(BF16)
    w2 = w_ff2[0].astype(BF16)
    gf = final_g.reshape(1, D_MODEL)

    y_p, st_p, hist_p = _prompt_layer(x_prompt[0], mod_p, weights, (norm2_g, w1, w2, gf))
    x1_s, st_s, hist_s = _sample_mixer(x_sample.reshape(n_dec * n_seq_s, D_MODEL), mod_s[0:3],
                                       state_gla[0], cache_pool[0], weights)
    y_s = _ffn_final(x1_s, mod_s[3:6], norm2_g, w1, w2, gf)

    return (y_p[None], y_s.reshape(n_dec, n_seq_s, D_MODEL), st_p[None, None],
            hist_p[None, None, 1:], st_s[None], hist_s[None])
```

```python
import jax
import jax.numpy as jnp
from jax import lax
from jax.experimental import pallas as pl
from jax.experimental.pallas import tpu as pltpu

D_MODEL = 1024
PAST_LEN = 4096
POOL_WIDTH = 512
POOL_WINDOWS = (2, 4, 8, 16)
POOL_GD = 128
POOL_HIST = 15
GLA_HEADS = 4
GLA_DK = 512
GLA_DV = 1024
GLA_HK = 128
GLA_HV = 256
GLA_LOWRANK = 16
GLA_GATE_NORM = 16.0
D_FF = 4096
EPS = 1e-6
LOG2_E = 1.4426950408889634
LANES = 128
IN_SIZES = (POOL_WIDTH, GLA_DK, GLA_DK, GLA_DV, GLA_DV, GLA_LOWRANK, D_MODEL, D_MODEL)

SUB = 8
CHUNK = 64
SLAB = 64
HIST_BASE = 32
TL_MIX = 256
TL_FFN = 512
FF_CHUNK = 1024
ADA_BLOCK = 1536
SEQ_GROUP = 4
VMEM_LIMIT = 48 * 1024 * 1024
VMEM_LIMIT_LAYER = 60 * 1024 * 1024

F32 = jnp.float32
BF16 = jnp.bfloat16


def _bdot(a, b):
    return jnp.dot(a, b, preferred_element_type=F32)


def _proj(h, wt_ref):
    return lax.dot_general(h, wt_ref[...], (((1,), (1,)), ((), ())), preferred_element_type=F32)


def _rms(xf):
    return xf * lax.rsqrt(jnp.mean(xf * xf, axis=-1, keepdims=True) + EPS)


def _sigmoid(x):
    return 0.5 * jnp.tanh(0.5 * x) + 0.5


def _log_sigmoid(z):
    return jnp.minimum(z, 0.0) - jnp.log(1.0 + jnp.exp(-jnp.abs(z)))


def _rows_per_seq(m, n_rows):
    n_seq = m.shape[0]
    if n_seq == 1:
        return m
    rep = n_rows // n_seq
    return jnp.concatenate([jnp.broadcast_to(m[i:i + 1], (rep, m.shape[1])) for i in range(n_seq)],
                           axis=0)


def _norm_mod(x, g, shift, scale):
    n = x.shape[0]
    return (_rms(x) * g * (1.0 + _rows_per_seq(scale, n)) + _rows_per_seq(shift, n)).astype(BF16)


def _store_blocks(ref, val):
    nblk, rows = ref.shape[0], ref.shape[1]
    for i in range(nblk):
        ref[i] = val[i * rows:(i + 1) * rows].astype(ref.dtype)


def _load_blocks(ref):
    return jnp.concatenate([ref[i] for i in range(ref.shape[0])], axis=0)


def _cumsum_rows(a, period):
    n = a.shape[0]
    ri = lax.broadcasted_iota(jnp.int32, (n, n), 0)
    ci = lax.broadcasted_iota(jnp.int32, (n, n), 1)
    shift = period.bit_length() - 1
    same = jnp.right_shift(ri, shift) == jnp.right_shift(ci, shift)
    tri = jnp.where((ci <= ri) & same, 1.0, 0.0).astype(BF16)
    hi = a.astype(BF16)
    r1 = a - hi.astype(F32)
    mid = r1.astype(BF16)
    lo = (r1 - mid.astype(F32)).astype(BF16)
    return _bdot(tri, hi) + _bdot(tri, mid) + _bdot(tri, lo)


def _gla_scores(q, k, b, k_row, b_row, nsub):
    L = SUB * nsub
    blast = b_row(L - 1)
    qt = (q * jnp.exp2(b)).astype(BF16)
    kt = (k * jnp.exp2(blast - b)).astype(BF16)

    lane = lax.broadcasted_iota(jnp.int32, (SUB, L), 1)
    row = lax.broadcasted_iota(jnp.int32, (SUB, L), 0)
    qs = [q[SUB * i:SUB * (i + 1)] for i in range(nsub)]
    ks = [k[SUB * i:SUB * (i + 1)] for i in range(nsub)]
    bs = [b[SUB * i:SUB * (i + 1)] for i in range(nsub)]

    diag = []
    for i in range(nsub):
        acc = jnp.zeros((SUB, L), F32)
        for j in range(SUB):
            r = SUB * i + j
            dec = jnp.exp2(bs[i] - b_row(r))
            col = jnp.sum(qs[i] * dec * k_row(r), axis=1, keepdims=True)
            acc = jnp.where(lane == r, col, acc)
        diag.append(jnp.where(lane <= row + SUB * i, acc, 0.0))
    p = diag[0] if nsub == 1 else jnp.concatenate(diag, axis=0)

    p_off = None
    if nsub > 1:
        zero = jnp.zeros((SUB, GLA_HK), F32)
        lhs, rhs = [], []
        for j in range(nsub - 1):
            bend = b_row(SUB * j + SUB - 1)
            lrows = [zero if i <= j else qs[i] * jnp.exp2(bs[i] - bend) for i in range(nsub)]
            rrows = [ks[j] * jnp.exp2(bend - bs[j]) if i == j else zero for i in range(nsub)]
            lhs.append(jnp.concatenate(lrows, axis=0).astype(BF16))
            rhs.append(jnp.concatenate(rrows, axis=0).astype(BF16))
        lhs = jnp.concatenate(lhs, axis=1)
        rhs = jnp.concatenate(rhs, axis=1)
        p_off = lax.dot_general(lhs, rhs, (((1,), (1,)), ((), ())), preferred_element_type=F32)
    return qt, kt, p, p_off, blast


def _gla_apply(scores, v, s):
    qt, kt, p, p_off, blast = scores
    if p_off is not None:
        p = p + p_off
    o = _bdot(jnp.concatenate([qt, p.astype(BF16)], axis=1),
              jnp.concatenate([s.astype(BF16), v], axis=0))

    ri = lax.broadcasted_iota(jnp.int32, (GLA_HK, GLA_HK), 0)
    ci = lax.broadcasted_iota(jnp.int32, (GLA_HK, GLA_HK), 1)
    erow = jnp.broadcast_to(jnp.exp2(blast), (GLA_HK, GLA_HK))
    ecol = jnp.sum(jnp.where(ri == ci, erow, 0.0), axis=1, keepdims=True)
    s_new = s * ecol + lax.dot_general(kt, v, (((0,), (0,)), ((), ())), preferred_element_type=F32)
    return o, s_new


def _gla_head_scores(blk, h, q_s, k_s, b_s):
    ksl = slice(h * GLA_HK, (h + 1) * GLA_HK)
    k_row = lambda r: k_s[blk, r:r + 1, ksl]
    b_row = lambda r: b_s[blk, r:r + 1, ksl]
    return _gla_scores(q_s[blk, :, ksl], k_s[blk, :, ksl], b_s[blk, :, ksl], k_row, b_row,
                       q_s.shape[1] // SUB)


def _gla_head_apply(blk, h, scores, v_s, sg_s, gng, bo_s, get_state, put_state):
    vsl = slice(h * GLA_HV, (h + 1) * GLA_HV)
    o, s_new = _gla_apply(scores, v_s[blk, :, vsl].astype(BF16), get_state(h))
    put_state(h, s_new)
    o = _rms(o) * gng
    bo_s[blk, :, vsl] = (o * sg_s[blk, :, vsl]).astype(BF16)


def _gla_heads(blk, q_s, k_s, b_s, v_s, sg_s, gng, bo_s, get_state, put_state):
    for h in range(GLA_HEADS):
        scores = _gla_head_scores(blk, h, q_s, k_s, b_s)
        _gla_head_apply(blk, h, scores, v_s, sg_s, gng, bo_s, get_state, put_state)


def _pool_delta(u_b, s2_b, s4_b, s8_b, n, pos0):
    r = HIST_BASE + n
    s2_b[16:r, :] = u_b[16:r, :] + u_b[15:r - 1, :]
    s4_b[16:r, :] = s2_b[16:r, 128:512] + s2_b[14:r - 2, 128:512]
    s8_b[24:r, :] = s4_b[24:r, 128:384] + s4_b[20:r - 4, 128:384]
    s16 = s8_b[HIST_BASE:r, 128:256] + s8_b[HIST_BASE - 8:r - 8, 128:256]
    sums = (s2_b[HIST_BASE:r, 0:128], s4_b[HIST_BASE:r, 0:128], s8_b[HIST_BASE:r, 0:128], s16)
    pos1 = pos0 + lax.broadcasted_iota(jnp.int32, (n, 1), 0) + 1
    out = []
    for gi, w in enumerate(POOL_WINDOWS):
        cnt = jnp.minimum(pos1, w).astype(F32)
        out.append(sums[gi] / cnt - u_b[HIST_BASE:r, gi * POOL_GD:(gi + 1) * POOL_GD])
    return out


def _pool_mix(d, wpool, pscale):
    mixed = [_bdot(d[gi].astype(BF16), wpool[gi]) for gi in range(len(POOL_WINDOWS))]
    return (jnp.concatenate(mixed, axis=1) * pscale).astype(BF16)


def _project_gla(x, mod_ref, g1, wq, wk, wv, wg, walr, walpha, balpha, period,
                 h_s, q_s, k_s, v_s, sg_s, b_s):
    h = _norm_mod(x, g1[...], mod_ref[0], mod_ref[1])
    h_s[...] = h
    _store_blocks(q_s, _proj(h, wq) * (GLA_HK ** -0.5))
    _store_blocks(k_s, _proj(h, wk))
    _store_blocks(v_s, _proj(h, wv))
    g = _proj(h, wg)
    _store_blocks(sg_s, g * _sigmoid(g))
    alr = _proj(h, walr)
    z = _bdot(alr.astype(BF16), walpha[...]) + balpha[...]
    a = _log_sigmoid(z) * (LOG2_E / GLA_GATE_NORM)
    b = [_cumsum_rows(a[r:r + SLAB], period) for r in range(0, a.shape[0], SLAB)]
    _store_blocks(b_s, jnp.concatenate(b, axis=0))


def _gated_pool(h_s, aout, wga, wpa):
    return _sigmoid(_proj(h_s[...], wga)) * _bdot(aout, wpa[...])


def _gate_b(h_s, wgb):
    return _sigmoid(_proj(h_s[...], wgb))


def _merge_out(x, gate1, merged_a, gate_b, bo, wpb, wout):
    merged = merged_a + gate_b * _bdot(bo, wpb[...])
    y = _bdot(merged.astype(BF16), wout[...])
    return x + _rows_per_seq(gate1, x.shape[0]) * y


def _ffn_tile(x, shift, scale, gate, g2, w1, w2, gf, o_ref):
    h2 = _norm_mod(x, g2[...], shift, scale)
    acc = jnp.zeros(x.shape, F32)
    for c in range(D_FF // FF_CHUNK):
        sl = slice(c * FF_CHUNK, (c + 1) * FF_CHUNK)
        y = jnp.maximum(_bdot(h2, w1[:, sl]), 0.0)
        acc = acc + _bdot((y * y).astype(BF16), w2[sl, :])
    x2 = x + _rows_per_seq(gate, x.shape[0]) * acc
    o_ref[...] = _rms(x2) * gf[...]


def _adaln_kernel(c_ref, w_ref, b_ref, o_ref):
    c = c_ref[...]
    sc = (c * _sigmoid(c)).astype(BF16)
    o_ref[...] = _bdot(sc, w_ref[...].astype(BF16)) + b_ref[...]


def _prompt_layer_kernel(x_ref, xn_ref, mod_ref, g1, wu, wq, wk, wv, wg, walr, wga, wgb, walpha, balpha,
                         wpool, pscale, gng, wpa, wpb, wout, g2, w1, w2, gf,
                         y_ref, st_ref, hist_ref,
                         u_b, s2_b, s4_b, s8_b, hh_s, q_s, k_s, v_s, sg_s, b_s, bo_s,
                         st_s, x1_s, h2_s, y_s, acc_s):
    t = pl.program_id(0)
    n_tiles = pl.num_programs(0) - 1
    tl = x_ref.shape[0]
    n_chunks = tl // CHUNK
    n_piece = GLA_HEADS
    piece = D_MODEL // n_piece
    assert n_chunks * FF_CHUNK == w1.shape[1] and FF_CHUNK == n_piece * piece

    @pl.when(t == 0)
    def _():
        st_s[...] = jnp.zeros_like(st_s)
        u_b[0:HIST_BASE, :] = jnp.zeros((HIST_BASE, POOL_WIDTH), F32)
        s2_b[0:16, :] = jnp.zeros((16, POOL_WIDTH), F32)
        hh_s[0] = _norm_mod(x_ref[...], g1[...], mod_ref[0], mod_ref[1])

    def step(with_ffn):
        slot = lax.rem(t, 2)
        alr = _proj(hh_s[slot], walr)
        _store_blocks(k_s, _proj(hh_s[slot], wk))
        if with_ffn:
            h2_s[...] = _norm_mod(x1_s[...], g2[...], mod_ref[3], mod_ref[4])
            acc_s[...] = jnp.zeros_like(acc_s)
        z = _bdot(alr.astype(BF16), walpha[...]) + balpha[...]
        a = _log_sigmoid(z) * (LOG2_E / GLA_GATE_NORM)
        _store_blocks(q_s, _proj(hh_s[slot], wq) * (GLA_HK ** -0.5))
        b = [_cumsum_rows(a[r:r + SLAB], CHUNK) for r in range(0, tl, SLAB)]
        _store_blocks(b_s, jnp.concatenate(b, axis=0))
        _store_blocks(v_s, _proj(hh_s[slot], wv))
        g = _proj(hh_s[slot], wg)
        _store_blocks(sg_s, g * _sigmoid(g))

        def get_state(h):
            return st_s[h]

        def put_state(h, s_new):
            st_s[h] = s_new

        def chunk_body(c):
            def scores(h):
                return _gla_head_scores(c, h, q_s, k_s, b_s)

            def apply(h, sc):
                _gla_head_apply(c, h, sc, v_s, sg_s, gng[...], bo_s, get_state, put_state)

            def ffn_up(j):
                if not with_ffn:
                    return
                cols = slice(j * piece, (j + 1) * piece)
                w1_cols = slice(c * FF_CHUNK + j * piece, c * FF_CHUNK + (j + 1) * piece)
                y = jnp.maximum(_bdot(h2_s[...], w1[:, w1_cols]), 0.0)
                y_s[:, cols] = (y * y).astype(BF16)

            def ffn_down(j):
                if not with_ffn:
                    return
                cols = slice(j * piece, (j + 1) * piece)
                acc_s[:, cols] += _bdot(y_s[...], w2[c * FF_CHUNK:(c + 1) * FF_CHUNK, cols])

            sc0 = scores(0)
            ffn_up(0)
            sc1 = scores(1)
            ffn_up(1)
            apply(0, sc0)
            ffn_up(2)
            sc2 = scores(2)
            ffn_up(3)
            apply(1, sc1)
            ffn_down(0)
            sc3 = scores(3)
            ffn_down(1)
            apply(2, sc2)
            ffn_down(2)
            apply(3, sc3)
            ffn_down(3)

        for c in range(n_chunks):
            chunk_body(c)

        if with_ffn:
            x2 = x1_s[...] + mod_ref[5] * acc_s[...]
            y_ref[...] = _rms(x2) * gf[...]

        hh_s[1 - slot] = _norm_mod(xn_ref[...], g1[...], mod_ref[0], mod_ref[1])

        u_b[HIST_BASE:HIST_BASE + tl, :] = _proj(hh_s[slot], wu)
        gate_a = _sigmoid(_proj(hh_s[slot], wga))
        gate_b = _sigmoid(_proj(hh_s[slot], wgb))
        d = _pool_delta(u_b, s2_b, s4_b, s8_b, tl, t * tl)
        aout = _pool_mix(d, wpool, pscale[...])
        u_b[16:HIST_BASE, :] = u_b[16 + tl:HIST_BASE + tl, :]
        merged = gate_a * _bdot(aout, wpa[...]) + gate_b * _bdot(_load_blocks(bo_s), wpb[...])
        x1 = x_ref[...] + mod_ref[2] * _bdot(merged.astype(BF16), wout[...])
        x1_s[...] = x1

    pl.when(t == 0)(lambda: step(False))
    pl.when((t > 0) & (t < n_tiles))(lambda: step(True))

    @pl.when(t == n_tiles - 1)
    def _():
        st_ref[...] = st_s[...]
        hist_ref[...] = u_b[16:HIST_BASE, :]

    @pl.when(t == n_tiles)
    def _():
        _ffn_tile(x1_s[...], mod_ref[3], mod_ref[4], mod_ref[5], g2, w1, w2, gf, y_ref)


def _sample_mixer_kernel(x_ref, mod_ref, s0_ref, cache_ref, g1, wu, wq, wk, wv, wg, walr, wga, wgb,
                         walpha, balpha, wpool, pscale, gng, wpa, wpb, wout,
                         x1_ref, st_ref, hist_ref,
                         u_b, s2_b, s4_b, s8_b, h_s, q_s, k_s, v_s, sg_s, b_s, bo_s, u_s, ao_s):
    s = pl.program_id(0)
    seq = q_s.shape[1]
    group = s0_ref.shape[0]

    @pl.when(s == 0)
    def _():
        u_b[0:HIST_BASE, :] = jnp.zeros((HIST_BASE, POOL_WIDTH), F32)
        s2_b[0:16, :] = jnp.zeros((16, POOL_WIDTH), F32)
        _project_gla(x_ref[...], mod_ref, g1, wq, wk, wv, wg, walr, walpha, balpha, seq,
                     h_s, q_s, k_s, v_s, sg_s, b_s)
        _store_blocks(u_s, _proj(h_s[...], wu))

    for i in range(group):
        blk = s * group + i
        u_b[HIST_BASE - POOL_HIST:HIST_BASE, :] = cache_ref[i]
        u_b[HIST_BASE:HIST_BASE + seq, :] = u_s[blk]
        d = _pool_delta(u_b, s2_b, s4_b, s8_b, seq, PAST_LEN)
        ao_s[blk] = _pool_mix(d, wpool, pscale[...])
        hist_ref[i] = u_b[HIST_BASE + seq - POOL_HIST:HIST_BASE + seq, :]

        def get_state(h, i=i):
            return s0_ref[i, h]

        def put_state(h, s_new, i=i):
            st_ref[i, h] = s_new

        _gla_heads(blk, q_s, k_s, b_s, v_s, sg_s, gng[...], bo_s, get_state, put_state)

    @pl.when(s == pl.num_programs(0) - 1)
    def _():
        merged_a = _gated_pool(h_s, _load_blocks(ao_s), wga, wpa)
        x1_ref[...] = _merge_out(x_ref[...], mod_ref[2], merged_a, _gate_b(h_s, wgb),
                                 _load_blocks(bo_s), wpb, wout)


def _ffn_kernel(x_ref, mod_ref, g2, w1, w2, gf, o_ref, h2_s, acc_s):
    c = pl.program_id(0)

    @pl.when(c == 0)
    def _():
        h2_s[...] = _norm_mod(x_ref[...], g2[...], mod_ref[0], mod_ref[1])
        acc_s[...] = jnp.zeros_like(acc_s)

    y = jnp.maximum(_bdot(h2_s[...], w1[...]), 0.0)
    acc_s[...] += _bdot((y * y).astype(BF16), w2[...])

    @pl.when(c == pl.num_programs(0) - 1)
    def _():
        x = x_ref[...]
        x2 = x + _rows_per_seq(mod_ref[2], x.shape[0]) * acc_s[...]
        o_ref[...] = _rms(x2) * gf[...]


def _const_spec(shape):
    nd = len(shape)
    return pl.BlockSpec(shape, lambda *_: (0,) * nd, pipeline_mode=pl.Buffered(1))


def _params(vmem_limit=None):
    return pltpu.CompilerParams(dimension_semantics=("arbitrary",),
                                vmem_limit_bytes=vmem_limit or VMEM_LIMIT)


def _mixer_scratch(nblk, rows, hist_rows, h_slots=None):
    h_shape = (nblk * rows, D_MODEL) if h_slots is None else (h_slots, nblk * rows, D_MODEL)
    return [
        pltpu.VMEM((hist_rows, POOL_WIDTH), F32),
        pltpu.VMEM((hist_rows, POOL_WIDTH), F32),
        pltpu.VMEM((hist_rows, POOL_WIDTH - 128), F32),
        pltpu.VMEM((hist_rows, POOL_WIDTH - 256), F32),
        pltpu.VMEM(h_shape, BF16),
        pltpu.VMEM((nblk, rows, GLA_DK), F32),
        pltpu.VMEM((nblk, rows, GLA_DK), F32),
        pltpu.VMEM((nblk, rows, GLA_DV), F32),
        pltpu.VMEM((nblk, rows, GLA_DV), F32),
        pltpu.VMEM((nblk, rows, GLA_DK), F32),
        pltpu.VMEM((nblk, rows, GLA_DV), BF16),
    ]


def _adaln(c_all, w_ada, b_ada):
    rows = c_all.shape[0]
    n = w_ada.shape[1]
    return pl.pallas_call(
        _adaln_kernel,
        grid=(n // ADA_BLOCK,),
        in_specs=[pl.BlockSpec((rows, D_MODEL), lambda j: (0, 0)),
                  pl.BlockSpec((D_MODEL, ADA_BLOCK), lambda j: (0, j)),
                  pl.BlockSpec((1, ADA_BLOCK), lambda j: (0, j))],
        out_specs=pl.BlockSpec((rows, ADA_BLOCK), lambda j: (0, j)),
        out_shape=jax.ShapeDtypeStruct((rows, n), F32),
        compiler_params=_params(),
        name="adaln_mod",
    )(c_all, w_ada, b_ada)


def _prompt_layer(x, mod, weights, ffn_weights):
    n_tok = x.shape[0]
    tl = TL_MIX
    n_tiles = n_tok // tl
    consts = (mod,) + tuple(weights) + tuple(ffn_weights)
    return pl.pallas_call(
        _prompt_layer_kernel,
        grid=(n_tiles + 1,),
        in_specs=[pl.BlockSpec((tl, D_MODEL), lambda t: (jnp.minimum(t, n_tiles - 1), 0)),
                  pl.BlockSpec((tl, D_MODEL), lambda t: (jnp.minimum(t + 1, n_tiles - 1), 0))]
        + [_const_spec(w.shape) for w in consts],
        out_specs=[pl.BlockSpec((tl, D_MODEL), lambda t: (jnp.maximum(t - 1, 0), 0)),
                   pl.BlockSpec((GLA_HEADS, GLA_HK, GLA_HV), lambda t: (0, 0, 0)),
                   pl.BlockSpec((16, POOL_WIDTH), lambda t: (0, 0))],
        out_shape=[jax.ShapeDtypeStruct((n_tok, D_MODEL), F32),
                   jax.ShapeDtypeStruct((GLA_HEADS, GLA_HK, GLA_HV), F32),
                   jax.ShapeDtypeStruct((16, POOL_WIDTH), F32)],
        scratch_shapes=_mixer_scratch(tl // CHUNK, CHUNK, HIST_BASE + tl, h_slots=2) + [
            pltpu.VMEM((GLA_HEADS, GLA_HK, GLA_HV), F32),
            pltpu.VMEM((tl, D_MODEL), F32),
            pltpu.VMEM((tl, D_MODEL), BF16),
            pltpu.VMEM((tl, FF_CHUNK), BF16),
            pltpu.VMEM((tl, D_MODEL), F32),
        ],
        compiler_params=_params(VMEM_LIMIT_LAYER),
        name="prompt_layer",
    )(x, x, *consts)


def _sample_mixer(x, mod, s0, cache, weights):
    n_tok = x.shape[0]
    n_seq = s0.shape[0]
    seq = n_tok // n_seq
    group = SEQ_GROUP
    w_specs = [_const_spec(w.shape) for w in weights]
    st_spec = pl.BlockSpec((group, GLA_HEADS, GLA_HK, GLA_HV), lambda s: (s, 0, 0, 0))
    hist_spec = pl.BlockSpec((group, POOL_HIST, POOL_WIDTH), lambda s: (s, 0, 0))
    return pl.pallas_call(
        _sample_mixer_kernel,
        grid=(n_seq // group,),
        in_specs=[_const_spec(x.shape), _const_spec(mod.shape), st_spec, hist_spec] + w_specs,
        out_specs=[pl.BlockSpec((n_tok, D_MODEL), lambda s: (0, 0)), st_spec, hist_spec],
        out_shape=[jax.ShapeDtypeStruct((n_tok, D_MODEL), F32),
                   jax.ShapeDtypeStruct(s0.shape, F32),
                   jax.ShapeDtypeStruct(cache.shape, F32)],
        scratch_shapes=_mixer_scratch(n_seq, seq, HIST_BASE + seq) + [
            pltpu.VMEM((n_seq, seq, POOL_WIDTH), F32),
            pltpu.VMEM((n_seq, seq, POOL_WIDTH), BF16),
        ],
        compiler_params=_params(),
        name="sample_mixer",
    )(x, mod, s0, cache, *weights)


def _ffn_final(x, mod, g2, w1, w2, gf):
    n_tok = x.shape[0]
    return pl.pallas_call(
        _ffn_kernel,
        grid=(D_FF // FF_CHUNK,),
        in_specs=[_const_spec(x.shape), _const_spec(mod.shape), _const_spec(g2.shape),
                  pl.BlockSpec((D_MODEL, FF_CHUNK), lambda c: (0, c)),
                  pl.BlockSpec((FF_CHUNK, D_MODEL), lambda c: (c, 0)),
                  _const_spec(gf.shape)],
        out_specs=pl.BlockSpec((n_tok, D_MODEL), lambda c: (0, 0)),
        out_shape=jax.ShapeDtypeStruct((n_tok, D_MODEL), F32),
        scratch_shapes=[pltpu.VMEM((n_tok, D_MODEL), BF16),
                        pltpu.VMEM((n_tok, D_MODEL), F32)],
        compiler_params=_params(),
        name="ffn_final",
    )(x, mod, g2, w1, w2, gf)


def kernel(x_prompt, x_sample, c_prompt, c_sample, state_gla, cache_pool, w_ada, b_ada, norm1_g,
           w_in, w_alpha, b_alpha, w_pool, pool_scale, gla_norm_g, w_pa, w_pb, w_out, norm2_g,
           w_ff1, w_ff2, final_g):
    n_batch, n_seq_p, _ = x_prompt.shape
    n_dec, n_seq_s, _ = x_sample.shape
    assert n_batch == 1 and w_ada.shape[0] == 1
    assert n_seq_p % TL_MIX == 0 and SLAB % n_seq_s == 0 and n_seq_s % SUB == 0
    assert n_dec % SEQ_GROUP == 0

    n_c = n_batch + n_dec
    pad = (-n_c) % 8
    c_all = jnp.concatenate([c_prompt, c_sample, jnp.zeros((pad, D_MODEL), F32)], axis=0)
    mod = _adaln(c_all, w_ada[0], b_ada)
    mod = mod.reshape(n_c + pad, 6, D_MODEL).transpose(1, 0, 2)
    mod_p = mod[:, 0:1]
    mod_s = mod[:, n_batch:n_c]

    offs = [0]
    for sz in IN_SIZES:
        offs.append(offs[-1] + sz)
    w_in_t = jnp.swapaxes(w_in[0], 0, 1).astype(BF16)
    wu, wq, wk, wv, wg, walr, wga, wgb = [w_in_t[offs[i]:offs[i + 1]] for i in range(8)]
    walr = jnp.pad(walr, ((0, LANES - GLA_LOWRANK), (0, 0)))
    walpha = jnp.pad(w_alpha[0].astype(BF16), ((0, LANES - GLA_LOWRANK), (0, 0)))
    weights = (norm1_g, wu, wq, wk, wv, wg, walr, wga, wgb,
               walpha, b_alpha, w_pool[0].astype(BF16), pool_scale, gla_norm_g,
               w_pa[0].astype(BF16), w_pb[0].astype(BF16), w_out[0].astype(BF16))

    w1 = w_ff1[0].astype(BF16)
    w2 = w_ff2[0].astype(BF16)
    gf = final_g.reshape(1, D_MODEL)

    y_p, st_p, hist_p = _prompt_layer(x_prompt[0], mod_p, weights, (norm2_g, w1, w2, gf))
    x1_s, st_s, hist_s = _sample_mixer(x_sample.reshape(n_dec * n_seq_s, D_MODEL), mod_s[0:3],
                                       state_gla[0], cache_pool[0], weights)
    y_s = _ffn_final(x1_s, mod_s[3:6], norm2_g, w1, w2, gf)

    return (y_p[None], y_s.reshape(n_dec, n_seq_s, D_MODEL), st_p[None, None],
            hist_p[None, None, 1:], st_s[None], hist_s[None])
```

```python
import jax
import jax.numpy as jnp
from jax import lax
from jax.experimental import pallas as pl
from jax.experimental.pallas import tpu as pltpu

D_MODEL = 1024
PAST_LEN = 4096
POOL_WIDTH = 512
POOL_WINDOWS = (2, 4, 8, 16)
POOL_GD = 128
POOL_HIST = 15
GLA_HEADS = 4
GLA_DK = 512
GLA_DV = 1024
GLA_HK = 128
GLA_HV = 256
GLA_LOWRANK = 16
GLA_GATE_NORM = 16.0
D_FF = 4096
EPS = 1e-6
LOG2_E = 1.4426950408889634
LANES = 128
IN_SIZES = (POOL_WIDTH, GLA_DK, GLA_DK, GLA_DV, GLA_DV, GLA_LOWRANK, D_MODEL, D_MODEL)

SUB = 8
CHUNK = 64
SLAB = 64
HIST_BASE = 32
TL_MIX = 256
TL_FFN = 512
FF_CHUNK = 1024
ADA_BLOCK = 1536
SEQ_GROUP = 4
VMEM_LIMIT = 48 * 1024 * 1024
VMEM_LIMIT_LAYER = 60 * 1024 * 1024

F32 = jnp.float32
BF16 = jnp.bfloat16


def _bdot(a, b):
    return jnp.dot(a, b, preferred_element_type=F32)


def _proj(h, wt_ref):
    return lax.dot_general(h, wt_ref[...], (((1,), (1,)), ((), ())), preferred_element_type=F32)


def _rms(xf):
    return xf * lax.rsqrt(jnp.mean(xf * xf, axis=-1, keepdims=True) + EPS)


def _sigmoid(x):
    return 0.5 * jnp.tanh(0.5 * x) + 0.5


def _log_sigmoid(z):
    return jnp.minimum(z, 0.0) - jnp.log(1.0 + jnp.exp(-jnp.abs(z)))


def _rows_per_seq(m, n_rows):
    n_seq = m.shape[0]
    if n_seq == 1:
        return m
    rep = n_rows // n_seq
    return jnp.concatenate([jnp.broadcast_to(m[i:i + 1], (rep, m.shape[1])) for i in range(n_seq)],
                           axis=0)


def _norm_mod(x, g, shift, scale):
    n = x.shape[0]
    return (_rms(x) * g * (1.0 + _rows_per_seq(scale, n)) + _rows_per_seq(shift, n)).astype(BF16)


def _store_blocks(ref, val):
    nblk, rows = ref.shape[0], ref.shape[1]
    for i in range(nblk):
        ref[i] = val[i * rows:(i + 1) * rows].astype(ref.dtype)


def _load_blocks(ref):
    return jnp.concatenate([ref[i] for i in range(ref.shape[0])], axis=0)


def _cumsum_rows(a, period):
    n = a.shape[0]
    ri = lax.broadcasted_iota(jnp.int32, (n, n), 0)
    ci = lax.broadcasted_iota(jnp.int32, (n, n), 1)
    shift = period.bit_length() - 1
    same = jnp.right_shift(ri, shift) == jnp.right_shift(ci, shift)
    tri = jnp.where((ci <= ri) & same, 1.0, 0.0).astype(BF16)
    hi = a.astype(BF16)
    r1 = a - hi.astype(F32)
    mid = r1.astype(BF16)
    lo = (r1 - mid.astype(F32)).astype(BF16)
    return _bdot(tri, hi) + _bdot(tri, mid) + _bdot(tri, lo)


def _gla_scores(q, k, b, k_row, b_row, nsub):
    L = SUB * nsub
    blast = b_row(L - 1)
    qt = (q * jnp.exp2(b)).astype(BF16)
    kt = (k * jnp.exp2(blast - b)).astype(BF16)

    lane = lax.broadcasted_iota(jnp.int32, (SUB, L), 1)
    row = lax.broadcasted_iota(jnp.int32, (SUB, L), 0)
    qs = [q[SUB * i:SUB * (i + 1)] for i in range(nsub)]
    ks = [k[SUB * i:SUB * (i + 1)] for i in range(nsub)]
    bs = [b[SUB * i:SUB * (i + 1)] for i in range(nsub)]

    diag = []
    for i in range(nsub):
        acc = jnp.zeros((SUB, L), F32)
        for j in range(SUB):
            r = SUB * i + j
            dec = jnp.exp2(bs[i] - b_row(r))
            col = jnp.sum(qs[i] * dec * k_row(r), axis=1, keepdims=True)
            acc = jnp.where(lane == r, col, acc)
        diag.append(jnp.where(lane <= row + SUB * i, acc, 0.0))
    p = diag[0] if nsub == 1 else jnp.concatenate(diag, axis=0)

    p_off = None
    if nsub > 1:
        zero = jnp.zeros((SUB, GLA_HK), F32)
        lhs, rhs = [], []
        for j in range(nsub - 1):
            bend = b_row(SUB * j + SUB - 1)
            lrows = [zero if i <= j else qs[i] * jnp.exp2(bs[i] - bend) for i in range(nsub)]
            rrows = [ks[j] * jnp.exp2(bend - bs[j]) if i == j else zero for i in range(nsub)]
            lhs.append(jnp.concatenate(lrows, axis=0).astype(BF16))
            rhs.append(jnp.concatenate(rrows, axis=0).astype(BF16))
        lhs = jnp.concatenate(lhs, axis=1)
        rhs = jnp.concatenate(rhs, axis=1)
        p_off = lax.dot_general(lhs, rhs, (((1,), (1,)), ((), ())), preferred_element_type=F32)
    return qt, kt, p, p_off, blast


def _gla_apply(scores, v, s):
    qt, kt, p, p_off, blast = scores
    if p_off is not None:
        p = p + p_off
    o = _bdot(jnp.concatenate([qt, p.astype(BF16)], axis=1),
              jnp.concatenate([s.astype(BF16), v], axis=0))

    ri = lax.broadcasted_iota(jnp.int32, (GLA_HK, GLA_HK), 0)
    ci = lax.broadcasted_iota(jnp.int32, (GLA_HK, GLA_HK), 1)
    erow = jnp.broadcast_to(jnp.exp2(blast), (GLA_HK, GLA_HK))
    ecol = jnp.sum(jnp.where(ri == ci, erow, 0.0), axis=1, keepdims=True)
    s_new = s * ecol + lax.dot_general(kt, v, (((0,), (0,)), ((), ())), preferred_element_type=F32)
    return o, s_new


def _gla_head_scores(blk, h, q_s, k_s, b_s):
    ksl = slice(h * GLA_HK, (h + 1) * GLA_HK)
    k_row = lambda r: k_s[blk, r:r + 1, ksl]
    b_row = lambda r: b_s[blk, r:r + 1, ksl]
    return _gla_scores(q_s[blk, :, ksl], k_s[blk, :, ksl], b_s[blk, :, ksl], k_row, b_row,
                       q_s.shape[1] // SUB)


def _gla_head_apply(blk, h, scores, v_s, sg_s, gng, bo_s, get_state, put_state):
    vsl = slice(h * GLA_HV, (h + 1) * GLA_HV)
    o, s_new = _gla_apply(scores, v_s[blk, :, vsl].astype(BF16), get_state(h))
    put_state(h, s_new)
    o = _rms(o) * gng
    bo_s[blk, :, vsl] = (o * sg_s[blk, :, vsl]).astype(BF16)


def _gla_heads(blk, q_s, k_s, b_s, v_s, sg_s, gng, bo_s, get_state, put_state):
    for h in range(GLA_HEADS):
        scores = _gla_head_scores(blk, h, q_s, k_s, b_s)
        _gla_head_apply(blk, h, scores, v_s, sg_s, gng, bo_s, get_state, put_state)


def _pool_delta(u_b, s2_b, s4_b, s8_b, n, pos0):
    r = HIST_BASE + n
    s2_b[16:r, :] = u_b[16:r, :] + u_b[15:r - 1, :]
    s4_b[16:r, :] = s2_b[16:r, 128:512] + s2_b[14:r - 2, 128:512]
    s8_b[24:r, :] = s4_b[24:r, 128:384] + s4_b[20:r - 4, 128:384]
    s16 = s8_b[HIST_BASE:r, 128:256] + s8_b[HIST_BASE - 8:r - 8, 128:256]
    sums = (s2_b[HIST_BASE:r, 0:128], s4_b[HIST_BASE:r, 0:128], s8_b[HIST_BASE:r, 0:128], s16)
    pos1 = pos0 + lax.broadcasted_iota(jnp.int32, (n, 1), 0) + 1
    out = []
    for gi, w in enumerate(POOL_WINDOWS):
        cnt = jnp.minimum(pos1, w).astype(F32)
        out.append(sums[gi] / cnt - u_b[HIST_BASE:r, gi * POOL_GD:(gi + 1) * POOL_GD])
    return out


def _pool_mix(d, wpool, pscale):
    mixed = [_bdot(d[gi].astype(BF16), wpool[gi]) for gi in range(len(POOL_WINDOWS))]
    return (jnp.concatenate(mixed, axis=1) * pscale).astype(BF16)


def _project_gla(x, mod_ref, g1, wq, wk, wv, wg, walr, walpha, balpha, period,
                 h_s, q_s, k_s, v_s, sg_s, b_s):
    h = _norm_mod(x, g1[...], mod_ref[0], mod_ref[1])
    h_s[...] = h
    _store_blocks(q_s, _proj(h, wq) * (GLA_HK ** -0.5))
    _store_blocks(k_s, _proj(h, wk))
    _store_blocks(v_s, _proj(h, wv))
    g = _proj(h, wg)
    _store_blocks(sg_s, g * _sigmoid(g))
    alr = _proj(h, walr)
    z = _bdot(alr.astype(BF16), walpha[...]) + balpha[...]
    a = _log_sigmoid(z) * (LOG2_E / GLA_GATE_NORM)
    b = [_cumsum_rows(a[r:r + SLAB], period) for r in range(0, a.shape[0], SLAB)]
    _store_blocks(b_s, jnp.concatenate(b, axis=0))


def _gated_pool(h_s, aout, wga, wpa):
    return _sigmoid(_proj(h_s[...], wga)) * _bdot(aout, wpa[...])


def _gate_b(h_s, wgb):
    return _sigmoid(_proj(h_s[...], wgb))


def _merge_out(x, gate1, merged_a, gate_b, bo, wpb, wout):
    merged = merged_a + gate_b * _bdot(bo, wpb[...])
    y = _bdot(merged.astype(BF16), wout[...])
    return x + _rows_per_seq(gate1, x.shape[0]) * y


def _ffn_tile(x, shift, scale, gate, g2, w1, w2, gf, o_ref):
    h2 = _norm_mod(x, g2[...], shift, scale)
    acc = jnp.zeros(x.shape, F32)
    for c in range(D_FF // FF_CHUNK):
        sl = slice(c * FF_CHUNK, (c + 1) * FF_CHUNK)
        y = jnp.maximum(_bdot(h2, w1[:, sl]), 0.0)
        acc = acc + _bdot((y * y).astype(BF16), w2[sl, :])
    x2 = x + _rows_per_seq(gate, x.shape[0]) * acc
    o_ref[...] = _rms(x2) * gf[...]


def _adaln_kernel(c_ref, w_ref, b_ref, o_ref):
    c = c_ref[...]
    sc = (c * _sigmoid(c)).astype(BF16)
    o_ref[...] = _bdot(sc, w_ref[...].astype(BF16)) + b_ref[...]


def _prompt_layer_kernel(x_ref, xn_ref, mod_ref, g1, wu, wq, wk, wv, wg, walr, wga, wgb, walpha, balpha,
                         wpool, pscale, gng, wpa, wpb, wout, g2, w1, w2, gf,
                         y_ref, st_ref, hist_ref,
                         u_b, s2_b, s4_b, s8_b, hh_s, q_s, k_s, v_s, sg_s, b_s, bo_s,
                         st_s, x1_s, h2_s, y_s, acc_s):
    t = pl.program_id(0)
    n_tiles = pl.num_programs(0) - 1
    tl = x_ref.shape[0]
    n_chunks = tl // CHUNK
    n_piece = GLA_HEADS
    piece = D_MODEL // n_piece
    assert n_chunks * FF_CHUNK == w1.shape[1] and FF_CHUNK == n_piece * piece

    @pl.when(t == 0)
    def _():
        st_s[...] = jnp.zeros_like(st_s)
        u_b[0:HIST_BASE, :] = jnp.zeros((HIST_BASE, POOL_WIDTH), F32)
        s2_b[0:16, :] = jnp.zeros((16, POOL_WIDTH), F32)
        x1_s[...] = jnp.zeros_like(x1_s)
        hh_s[0] = _norm_mod(x_ref[...], g1[...], mod_ref[0], mod_ref[1])

    @pl.when(t < n_tiles)
    def _():
        slot = lax.rem(t, 2)
        alr = _proj(hh_s[slot], walr)
        _store_blocks(k_s, _proj(hh_s[slot], wk))
        h2_s[...] = _norm_mod(x1_s[...], g2[...], mod_ref[3], mod_ref[4])
        z = _bdot(alr.astype(BF16), walpha[...]) + balpha[...]
        a = _log_sigmoid(z) * (LOG2_E / GLA_GATE_NORM)
        _store_blocks(q_s, _proj(hh_s[slot], wq) * (GLA_HK ** -0.5))
        b = [_cumsum_rows(a[r:r + SLAB], CHUNK) for r in range(0, tl, SLAB)]
        _store_blocks(b_s, jnp.concatenate(b, axis=0))
        _store_blocks(v_s, _proj(hh_s[slot], wv))
        g = _proj(hh_s[slot], wg)
        _store_blocks(sg_s, g * _sigmoid(g))

        def get_state(h):
            return st_s[h]

        def put_state(h, s_new):
            st_s[h] = s_new

        acc_s[...] = jnp.zeros_like(acc_s)

        def chunk_body(c):
            def scores(h):
                return _gla_head_scores(c, h, q_s, k_s, b_s)

            def apply(h, sc):
                _gla_head_apply(c, h, sc, v_s, sg_s, gng[...], bo_s, get_state, put_state)

            def ffn_up(j):
                cols = slice(j * piece, (j + 1) * piece)
                w1_cols = slice(c * FF_CHUNK + j * piece, c * FF_CHUNK + (j + 1) * piece)
                y = jnp.maximum(_bdot(h2_s[...], w1[:, w1_cols]), 0.0)
                y_s[:, cols] = (y * y).astype(BF16)

            def ffn_down(j):
                cols = slice(j * piece, (j + 1) * piece)
                acc_s[:, cols] += _bdot(y_s[...], w2[c * FF_CHUNK:(c + 1) * FF_CHUNK, cols])

            sc0 = scores(0)
            ffn_up(0)
            sc1 = scores(1)
            ffn_up(1)
            apply(0, sc0)
            ffn_up(2)
            sc2 = scores(2)
            ffn_up(3)
            apply(1, sc1)
            ffn_down(0)
            sc3 = scores(3)
            ffn_down(1)
            apply(2, sc2)
            ffn_down(2)
            apply(3, sc3)
            ffn_down(3)

        for c in range(n_chunks):
            chunk_body(c)

        x2 = x1_s[...] + mod_ref[5] * acc_s[...]
        y_ref[...] = _rms(x2) * gf[...]

        hh_s[1 - slot] = _norm_mod(xn_ref[...], g1[...], mod_ref[0], mod_ref[1])

        u_b[HIST_BASE:HIST_BASE + tl, :] = _proj(hh_s[slot], wu)
        gate_a = _sigmoid(_proj(hh_s[slot], wga))
        gate_b = _sigmoid(_proj(hh_s[slot], wgb))
        d = _pool_delta(u_b, s2_b, s4_b, s8_b, tl, t * tl)
        aout = _pool_mix(d, wpool, pscale[...])
        u_b[16:HIST_BASE, :] = u_b[16 + tl:HIST_BASE + tl, :]
        merged = gate_a * _bdot(aout, wpa[...]) + gate_b * _bdot(_load_blocks(bo_s), wpb[...])
        x1 = x_ref[...] + mod_ref[2] * _bdot(merged.astype(BF16), wout[...])
        x1_s[...] = x1

    @pl.when(t == n_tiles - 1)
    def _():
        st_ref[...] = st_s[...]
        hist_ref[...] = u_b[16:HIST_BASE, :]

    @pl.when(t == n_tiles)
    def _():
        _ffn_tile(x1_s[...], mod_ref[3], mod_ref[4], mod_ref[5], g2, w1, w2, gf, y_ref)


def _sample_mixer_kernel(x_ref, mod_ref, s0_ref, cache_ref, g1, wu, wq, wk, wv, wg, walr, wga, wgb,
                         walpha, balpha, wpool, pscale, gng, wpa, wpb, wout,
                         x1_ref, st_ref, hist_ref,
                         u_b, s2_b, s4_b, s8_b, h_s, q_s, k_s, v_s, sg_s, b_s, bo_s, u_s, ao_s):
    s = pl.program_id(0)
    seq = q_s.shape[1]
    group = s0_ref.shape[0]

    @pl.when(s == 0)
    def _():
        u_b[0:HIST_BASE, :] = jnp.zeros((HIST_BASE, POOL_WIDTH), F32)
        s2_b[0:16, :] = jnp.zeros((16, POOL_WIDTH), F32)
        _project_gla(x_ref[...], mod_ref, g1, wq, wk, wv, wg, walr, walpha, balpha, seq,
                     h_s, q_s, k_s, v_s, sg_s, b_s)
        _store_blocks(u_s, _proj(h_s[...], wu))

    scores = [[_gla_head_scores(s * group + i, h, q_s, k_s, b_s) for h in range(GLA_HEADS)]
              for i in range(group)]

    for i in range(group):
        blk = s * group + i
        u_b[HIST_BASE - POOL_HIST:HIST_BASE, :] = cache_ref[i]
        u_b[HIST_BASE:HIST_BASE + seq, :] = u_s[blk]
        d = _pool_delta(u_b, s2_b, s4_b, s8_b, seq, PAST_LEN)
        ao_s[blk] = _pool_mix(d, wpool, pscale[...])
        hist_ref[i] = u_b[HIST_BASE + seq - POOL_HIST:HIST_BASE + seq, :]

    for i in range(group):
        def get_state(h, i=i):
            return s0_ref[i, h]

        def put_state(h, s_new, i=i):
            st_ref[i, h] = s_new

        for h in range(GLA_HEADS):
            _gla_head_apply(s * group + i, h, scores[i][h], v_s, sg_s, gng[...], bo_s,
                            get_state, put_state)

    @pl.when(s == pl.num_programs(0) - 1)
    def _():
        merged_a = _gated_pool(h_s, _load_blocks(ao_s), wga, wpa)
        x1_ref[...] = _merge_out(x_ref[...], mod_ref[2], merged_a, _gate_b(h_s, wgb),
                                 _load_blocks(bo_s), wpb, wout)


def _ffn_kernel(x_ref, mod_ref, g2, w1, w2, gf, o_ref):
    _ffn_tile(x_ref[...], mod_ref[0], mod_ref[1], mod_ref[2], g2, w1, w2, gf, o_ref)


def _const_spec(shape):
    nd = len(shape)
    return pl.BlockSpec(shape, lambda *_: (0,) * nd, pipeline_mode=pl.Buffered(1))


def _params(vmem_limit=None):
    return pltpu.CompilerParams(dimension_semantics=("arbitrary",),
                                vmem_limit_bytes=vmem_limit or VMEM_LIMIT)


def _mixer_scratch(nblk, rows, hist_rows, h_slots=None):
    h_shape = (nblk * rows, D_MODEL) if h_slots is None else (h_slots, nblk * rows, D_MODEL)
    return [
        pltpu.VMEM((hist_rows, POOL_WIDTH), F32),
        pltpu.VMEM((hist_rows, POOL_WIDTH), F32),
        pltpu.VMEM((hist_rows, POOL_WIDTH - 128), F32),
        pltpu.VMEM((hist_rows, POOL_WIDTH - 256), F32),
        pltpu.VMEM(h_shape, BF16),
        pltpu.VMEM((nblk, rows, GLA_DK), F32),
        pltpu.VMEM((nblk, rows, GLA_DK), F32),
        pltpu.VMEM((nblk, rows, GLA_DV), F32),
        pltpu.VMEM((nblk, rows, GLA_DV), F32),
        pltpu.VMEM((nblk, rows, GLA_DK), F32),
        pltpu.VMEM((nblk, rows, GLA_DV), BF16),
    ]


def _adaln(c_all, w_ada, b_ada):
    rows = c_all.shape[0]
    n = w_ada.shape[1]
    return pl.pallas_call(
        _adaln_kernel,
        grid=(n // ADA_BLOCK,),
        in_specs=[pl.BlockSpec((rows, D_MODEL), lambda j: (0, 0)),
                  pl.BlockSpec((D_MODEL, ADA_BLOCK), lambda j: (0, j)),
                  pl.BlockSpec((1, ADA_BLOCK), lambda j: (0, j))],
        out_specs=pl.BlockSpec((rows, ADA_BLOCK), lambda j: (0, j)),
        out_shape=jax.ShapeDtypeStruct((rows, n), F32),
        compiler_params=_params(),
        name="adaln_mod",
    )(c_all, w_ada, b_ada)


def _prompt_layer(x, mod, weights, ffn_weights):
    n_tok = x.shape[0]
    tl = TL_MIX
    n_tiles = n_tok // tl
    consts = (mod,) + tuple(weights) + tuple(ffn_weights)
    return pl.pallas_call(
        _prompt_layer_kernel,
        grid=(n_tiles + 1,),
        in_specs=[pl.BlockSpec((tl, D_MODEL), lambda t: (jnp.minimum(t, n_tiles - 1), 0)),
                  pl.BlockSpec((tl, D_MODEL), lambda t: (jnp.minimum(t + 1, n_tiles - 1), 0))]
        + [_const_spec(w.shape) for w in consts],
        out_specs=[pl.BlockSpec((tl, D_MODEL), lambda t: (jnp.maximum(t - 1, 0), 0)),
                   pl.BlockSpec((GLA_HEADS, GLA_HK, GLA_HV), lambda t: (0, 0, 0)),
                   pl.BlockSpec((16, POOL_WIDTH), lambda t: (0, 0))],
        out_shape=[jax.ShapeDtypeStruct((n_tok, D_MODEL), F32),
                   jax.ShapeDtypeStruct((GLA_HEADS, GLA_HK, GLA_HV), F32),
                   jax.ShapeDtypeStruct((16, POOL_WIDTH), F32)],
        scratch_shapes=_mixer_scratch(tl // CHUNK, CHUNK, HIST_BASE + tl, h_slots=2) + [
            pltpu.VMEM((GLA_HEADS, GLA_HK, GLA_HV), F32),
            pltpu.VMEM((tl, D_MODEL), F32),
            pltpu.VMEM((tl, D_MODEL), BF16),
            pltpu.VMEM((tl, FF_CHUNK), BF16),
            pltpu.VMEM((tl, D_MODEL), F32),
        ],
        compiler_params=_params(VMEM_LIMIT_LAYER),
        name="prompt_layer",
    )(x, x, *consts)


def _sample_mixer(x, mod, s0, cache, weights):
    n_tok = x.shape[0]
    n_seq = s0.shape[0]
    seq = n_tok // n_seq
    group = SEQ_GROUP
    w_specs = [_const_spec(w.shape) for w in weights]
    st_spec = pl.BlockSpec((group, GLA_HEADS, GLA_HK, GLA_HV), lambda s: (s, 0, 0, 0))
    hist_spec = pl.BlockSpec((group, POOL_HIST, POOL_WIDTH), lambda s: (s, 0, 0))
    return pl.pallas_call(
        _sample_mixer_kernel,
        grid=(n_seq // group,),
        in_specs=[_const_spec(x.shape), _const_spec(mod.shape), st_spec, hist_spec] + w_specs,
        out_specs=[pl.BlockSpec((n_tok, D_MODEL), lambda s: (0, 0)), st_spec, hist_spec],
        out_shape=[jax.ShapeDtypeStruct((n_tok, D_MODEL), F32),
                   jax.ShapeDtypeStruct(s0.shape, F32),
                   jax.ShapeDtypeStruct(cache.shape, F32)],
        scratch_shapes=_mixer_scratch(n_seq, seq, HIST_BASE + seq) + [
            pltpu.VMEM((n_seq, seq, POOL_WIDTH), F32),
            pltpu.VMEM((n_seq, seq, POOL_WIDTH), BF16),
        ],
        compiler_params=_params(),
        name="sample_mixer",
    )(x, mod, s0, cache, *weights)


def _ffn_final(x, mod, g2, w1, w2, gf):
    n_tok = x.shape[0]
    tl = min(TL_FFN, n_tok)
    assert mod.shape[1] == 1 or n_tok == tl
    return pl.pallas_call(
        _ffn_kernel,
        grid=(n_tok // tl,),
        in_specs=[pl.BlockSpec((tl, D_MODEL), lambda t: (t, 0)), _const_spec(mod.shape),
                  _const_spec(g2.shape), _const_spec(w1.shape), _const_spec(w2.shape),
                  _const_spec(gf.shape)],
        out_specs=pl.BlockSpec((tl, D_MODEL), lambda t: (t, 0)),
        out_shape=jax.ShapeDtypeStruct((n_tok, D_MODEL), F32),
        compiler_params=_params(),
        name="ffn_final",
    )(x, mod, g2, w1, w2, gf)


def kernel(x_prompt, x_sample, c_prompt, c_sample, state_gla, cache_pool, w_ada, b_ada, norm1_g,
           w_in, w_alpha, b_alpha, w_pool, pool_scale, gla_norm_g, w_pa, w_pb, w_out, norm2_g,
           w_ff1, w_ff2, final_g):
    n_batch, n_seq_p, _ = x_prompt.shape
    n_dec, n_seq_s, _ = x_sample.shape
    assert n_batch == 1 and w_ada.shape[0] == 1
    assert n_seq_p % TL_MIX == 0 and SLAB % n_seq_s == 0 and n_seq_s % SUB == 0
    assert n_dec % SEQ_GROUP == 0

    n_c = n_batch + n_dec
    pad = (-n_c) % 8
    c_all = jnp.concatenate([c_prompt, c_sample, jnp.zeros((pad, D_MODEL), F32)], axis=0)
    mod = _adaln(c_all, w_ada[0], b_ada)
    mod = mod.reshape(n_c + pad, 6, D_MODEL).transpose(1, 0, 2)
    mod_p = mod[:, 0:1]
    mod_s = mod[:, n_batch:n_c]

    offs = [0]
    for sz in IN_SIZES:
        offs.append(offs[-1] + sz)
    w_in_t = jnp.swapaxes(w_in[0], 0, 1).astype(BF16)
    wu, wq, wk, wv, wg, walr, wga, wgb = [w_in_t[offs[i]:offs[i + 1]] for i in range(8)]
    walr = jnp.pad(walr, ((0, LANES - GLA_LOWRANK), (0, 0)))
    walpha = jnp.pad(w_alpha[0].astype(BF16), ((0, LANES - GLA_LOWRANK), (0, 0)))
    weights = (norm1_g, wu, wq, wk, wv, wg, walr, wga, wgb,
               walpha, b_alpha, w_pool[0].astype(BF16), pool_scale, gla_norm_g,
               w_pa[0].astype(BF16), w_pb[0].astype(BF16), w_out[0].astype(BF16))

    w1 = w_ff1[0].astype(BF16)
    w2 = w_ff2[0].astype(BF16)
    gf = final_g.reshape(1, D_MODEL)

    y_p, st_p, hist_p = _prompt_layer(x_prompt[0], mod_p, weights, (norm2_g, w1, w2, gf))
    x1_s, st_s, hist_s = _sample_mixer(x_sample.reshape(n_dec * n_seq_s, D_MODEL), mod_s[0:3],
                                       state_gla[0], cache_pool[0], weights)
    y_s = _ffn_final(x1_s, mod_s[3:6], norm2_g, w1, w2, gf)

    return (y_p[None], y_s.reshape(n_dec, n_seq_s, D_MODEL), st_p[None, None],
            hist_p[None, None, 1:], st_s[None], hist_s[None])
```

```python
import jax
import jax.numpy as jnp
from jax import lax
from jax.experimental import pallas as pl
from jax.experimental.pallas import tpu as pltpu

D_MODEL = 1024
PAST_LEN = 4096
POOL_WIDTH = 512
POOL_WINDOWS = (2, 4, 8, 16)
POOL_GD = 128
POOL_HIST = 15
GLA_HEADS = 4
GLA_DK = 512
GLA_DV = 1024
GLA_HK = 128
GLA_HV = 256
GLA_LOWRANK = 16
GLA_GATE_NORM = 16.0
D_FF = 4096
EPS = 1e-6
LOG2_E = 1.4426950408889634
LANES = 128
IN_SIZES = (POOL_WIDTH, GLA_DK, GLA_DK, GLA_DV, GLA_DV, GLA_LOWRANK, D_MODEL, D_MODEL)

SUB = 8
CHUNK = 64
SLAB = 64
HIST_BASE = 32
TL_MIX = 256
TL_FFN = 512
FF_CHUNK = 1024
ADA_BLOCK = 1536
SEQ_GROUP = 4
VMEM_LIMIT = 48 * 1024 * 1024
VMEM_LIMIT_LAYER = 60 * 1024 * 1024

F32 = jnp.float32
BF16 = jnp.bfloat16


def _bdot(a, b):
    return jnp.dot(a, b, preferred_element_type=F32)


def _proj(h, wt_ref):
    return lax.dot_general(h, wt_ref[...], (((1,), (1,)), ((), ())), preferred_element_type=F32)


def _rms(xf):
    return xf * lax.rsqrt(jnp.mean(xf * xf, axis=-1, keepdims=True) + EPS)


def _sigmoid(x):
    return 0.5 * jnp.tanh(0.5 * x) + 0.5


def _log_sigmoid(z):
    return jnp.minimum(z, 0.0) - jnp.log(1.0 + jnp.exp(-jnp.abs(z)))


def _rows_per_seq(m, n_rows):
    n_seq = m.shape[0]
    if n_seq == 1:
        return m
    rep = n_rows // n_seq
    return jnp.concatenate([jnp.broadcast_to(m[i:i + 1], (rep, m.shape[1])) for i in range(n_seq)],
                           axis=0)


def _norm_mod(x, g, shift, scale):
    n = x.shape[0]
    return (_rms(x) * g * (1.0 + _rows_per_seq(scale, n)) + _rows_per_seq(shift, n)).astype(BF16)


def _store_blocks(ref, val):
    nblk, rows = ref.shape[0], ref.shape[1]
    for i in range(nblk):
        ref[i] = val[i * rows:(i + 1) * rows].astype(ref.dtype)


def _load_blocks(ref):
    return jnp.concatenate([ref[i] for i in range(ref.shape[0])], axis=0)


def _cumsum_rows(a, period):
    n = a.shape[0]
    ri = lax.broadcasted_iota(jnp.int32, (n, n), 0)
    ci = lax.broadcasted_iota(jnp.int32, (n, n), 1)
    shift = period.bit_length() - 1
    same = jnp.right_shift(ri, shift) == jnp.right_shift(ci, shift)
    tri = jnp.where((ci <= ri) & same, 1.0, 0.0).astype(BF16)
    hi = a.astype(BF16)
    r1 = a - hi.astype(F32)
    mid = r1.astype(BF16)
    lo = (r1 - mid.astype(F32)).astype(BF16)
    return _bdot(tri, hi) + _bdot(tri, mid) + _bdot(tri, lo)


def _gla_scores(q, k, b, k_row, b_row, nsub):
    L = SUB * nsub
    blast = b_row(L - 1)
    qt = (q * jnp.exp2(b)).astype(BF16)
    kt = (k * jnp.exp2(blast - b)).astype(BF16)

    lane = lax.broadcasted_iota(jnp.int32, (SUB, L), 1)
    row = lax.broadcasted_iota(jnp.int32, (SUB, L), 0)
    qs = [q[SUB * i:SUB * (i + 1)] for i in range(nsub)]
    ks = [k[SUB * i:SUB * (i + 1)] for i in range(nsub)]
    bs = [b[SUB * i:SUB * (i + 1)] for i in range(nsub)]

    diag = []
    for i in range(nsub):
        acc = jnp.zeros((SUB, L), F32)
        for j in range(SUB):
            r = SUB * i + j
            dec = jnp.exp2(bs[i] - b_row(r))
            col = jnp.sum(qs[i] * dec * k_row(r), axis=1, keepdims=True)
            acc = jnp.where(lane == r, col, acc)
        diag.append(jnp.where(lane <= row + SUB * i, acc, 0.0))
    p = diag[0] if nsub == 1 else jnp.concatenate(diag, axis=0)

    p_off = None
    if nsub > 1:
        zero = jnp.zeros((SUB, GLA_HK), F32)
        lhs, rhs = [], []
        for j in range(nsub - 1):
            bend = b_row(SUB * j + SUB - 1)
            lrows = [zero if i <= j else qs[i] * jnp.exp2(bs[i] - bend) for i in range(nsub)]
            rrows = [ks[j] * jnp.exp2(bend - bs[j]) if i == j else zero for i in range(nsub)]
            lhs.append(jnp.concatenate(lrows, axis=0).astype(BF16))
            rhs.append(jnp.concatenate(rrows, axis=0).astype(BF16))
        lhs = jnp.concatenate(lhs, axis=1)
        rhs = jnp.concatenate(rhs, axis=1)
        p_off = lax.dot_general(lhs, rhs, (((1,), (1,)), ((), ())), preferred_element_type=F32)
    return qt, kt, p, p_off, blast


def _gla_apply(scores, v, s):
    qt, kt, p, p_off, blast = scores
    if p_off is not None:
        p = p + p_off
    o = _bdot(jnp.concatenate([qt, p.astype(BF16)], axis=1),
              jnp.concatenate([s.astype(BF16), v], axis=0))

    ri = lax.broadcasted_iota(jnp.int32, (GLA_HK, GLA_HK), 0)
    ci = lax.broadcasted_iota(jnp.int32, (GLA_HK, GLA_HK), 1)
    erow = jnp.broadcast_to(jnp.exp2(blast), (GLA_HK, GLA_HK))
    ecol = jnp.sum(jnp.where(ri == ci, erow, 0.0), axis=1, keepdims=True)
    s_new = s * ecol + lax.dot_general(kt, v, (((0,), (0,)), ((), ())), preferred_element_type=F32)
    return o, s_new


def _gla_head_scores(blk, h, q_s, k_s, b_s):
    ksl = slice(h * GLA_HK, (h + 1) * GLA_HK)
    k_row = lambda r: k_s[blk, r:r + 1, ksl]
    b_row = lambda r: b_s[blk, r:r + 1, ksl]
    return _gla_scores(q_s[blk, :, ksl], k_s[blk, :, ksl], b_s[blk, :, ksl], k_row, b_row,
                       q_s.shape[1] // SUB)


def _gla_head_apply(blk, h, scores, v_s, sg_s, gng, bo_s, get_state, put_state):
    vsl = slice(h * GLA_HV, (h + 1) * GLA_HV)
    o, s_new = _gla_apply(scores, v_s[blk, :, vsl].astype(BF16), get_state(h))
    put_state(h, s_new)
    o = _rms(o) * gng
    bo_s[blk, :, vsl] = (o * sg_s[blk, :, vsl]).astype(BF16)


def _pool_delta(u_b, s2_b, s4_b, s8_b, n, pos0):
    r = HIST_BASE + n
    gd = POOL_GD
    s2_b[16:r, :] = u_b[16:r, :] + u_b[15:r - 1, :]
    s4_b[16:r, :] = s2_b[16:r, gd:] + s2_b[14:r - 2, gd:]
    s8_b[24:r, :] = s4_b[24:r, gd:] + s4_b[20:r - 4, gd:]
    s16 = s8_b[HIST_BASE:r, gd:] + s8_b[HIST_BASE - 8:r - 8, gd:]
    sums = (s2_b[HIST_BASE:r, 0:gd], s4_b[HIST_BASE:r, 0:gd], s8_b[HIST_BASE:r, 0:gd], s16)
    pos1 = pos0 + lax.broadcasted_iota(jnp.int32, (n, 1), 0) + 1
    out = []
    for gi, w in enumerate(POOL_WINDOWS):
        cnt = jnp.minimum(pos1, w).astype(F32)
        out.append(sums[gi] / cnt - u_b[HIST_BASE:r, gi * POOL_GD:(gi + 1) * POOL_GD])
    return out


def _pool_mix(d, wpool, pscale):
    mixed = [_bdot(d[gi].astype(BF16), wpool[gi]) for gi in range(len(POOL_WINDOWS))]
    return (jnp.concatenate(mixed, axis=1) * pscale).astype(BF16)


def _project_gla(x, mod_ref, g1, wq, wk, wv, wg, walr, walpha, balpha, period,
                 h_s, q_s, k_s, v_s, sg_s, b_s):
    h = _norm_mod(x, g1[...], mod_ref[0], mod_ref[1])
    h_s[...] = h
    _store_blocks(q_s, _proj(h, wq) * (GLA_HK ** -0.5))
    _store_blocks(k_s, _proj(h, wk))
    _store_blocks(v_s, _proj(h, wv))
    g = _proj(h, wg)
    _store_blocks(sg_s, g * _sigmoid(g))
    alr = _proj(h, walr)
    z = _bdot(alr.astype(BF16), walpha[...]) + balpha[...]
    a = _log_sigmoid(z) * (LOG2_E / GLA_GATE_NORM)
    b = [_cumsum_rows(a[r:r + SLAB], period) for r in range(0, a.shape[0], SLAB)]
    _store_blocks(b_s, jnp.concatenate(b, axis=0))


def _gated_pool(h_s, aout, wga, wpa):
    return _sigmoid(_proj(h_s[...], wga)) * _bdot(aout, wpa[...])


def _gate_b(h_s, wgb):
    return _sigmoid(_proj(h_s[...], wgb))


def _merge_out(x, gate1, merged_a, gate_b, bo, wpb, wout):
    merged = merged_a + gate_b * _bdot(bo, wpb[...])
    y = _bdot(merged.astype(BF16), wout[...])
    return x + _rows_per_seq(gate1, x.shape[0]) * y


def _ffn_tile(x, shift, scale, gate, g2, w1, w2, gf, o_ref):
    h2 = _norm_mod(x, g2[...], shift, scale)
    acc = jnp.zeros(x.shape, F32)
    for c in range(D_FF // FF_CHUNK):
        sl = slice(c * FF_CHUNK, (c + 1) * FF_CHUNK)
        y = jnp.maximum(_bdot(h2, w1[:, sl]), 0.0)
        acc = acc + _bdot((y * y).astype(BF16), w2[sl, :])
    x2 = x + _rows_per_seq(gate, x.shape[0]) * acc
    o_ref[...] = _rms(x2) * gf[...]


def _adaln_kernel(c_ref, w_ref, b_ref, o_ref):
    c = c_ref[...]
    sc = (c * _sigmoid(c)).astype(BF16)
    o_ref[...] = _bdot(sc, w_ref[...].astype(BF16)) + b_ref[...]


def _prompt_layer_kernel(x_ref, xn_ref, mod_ref, g1, wu, wq, wk, wv, wg, walr, wga, wgb, walpha, balpha,
                         wpool, pscale, gng, wpa, wpb, wout, g2, w1, w2, gf,
                         y_ref, st_ref, hist_ref,
                         u_b, s2_b, s4_b, s8_b, hh_s, q_s, k_s, v_s, sg_s, b_s, bo_s,
                         st_s, x1_s, h2_s, y_s, acc_s):
    t = pl.program_id(0)
    n_tiles = pl.num_programs(0) - 1
    tl = x_ref.shape[0]
    n_chunks = tl // CHUNK
    n_piece = GLA_HEADS
    piece = D_MODEL // n_piece
    assert n_chunks * FF_CHUNK == w1.shape[1] and FF_CHUNK == n_piece * piece

    @pl.when(t == 0)
    def _():
        st_s[...] = jnp.zeros_like(st_s)
        u_b[0:HIST_BASE, :] = jnp.zeros((HIST_BASE, POOL_WIDTH), F32)
        s2_b[0:16, :] = jnp.zeros((16, POOL_WIDTH), F32)
        x1_s[...] = jnp.zeros_like(x1_s)
        hh_s[0] = _norm_mod(x_ref[...], g1[...], mod_ref[0], mod_ref[1])

    @pl.when(t < n_tiles)
    def _():
        slot = lax.rem(t, 2)
        alr = _proj(hh_s[slot], walr)
        _store_blocks(k_s, _proj(hh_s[slot], wk))
        h2_s[...] = _norm_mod(x1_s[...], g2[...], mod_ref[3], mod_ref[4])
        z = _bdot(alr.astype(BF16), walpha[...]) + balpha[...]
        a = _log_sigmoid(z) * (LOG2_E / GLA_GATE_NORM)
        _store_blocks(q_s, _proj(hh_s[slot], wq) * (GLA_HK ** -0.5))
        b = [_cumsum_rows(a[r:r + SLAB], CHUNK) for r in range(0, tl, SLAB)]
        _store_blocks(b_s, jnp.concatenate(b, axis=0))
        _store_blocks(v_s, _proj(hh_s[slot], wv))
        g = _proj(hh_s[slot], wg)
        _store_blocks(sg_s, g * _sigmoid(g))

        def get_state(h):
            return st_s[h]

        def put_state(h, s_new):
            st_s[h] = s_new

        acc_s[...] = jnp.zeros_like(acc_s)

        def chunk_body(c):
            def scores(h):
                return _gla_head_scores(c, h, q_s, k_s, b_s)

            def apply(h, sc):
                _gla_head_apply(c, h, sc, v_s, sg_s, gng[...], bo_s, get_state, put_state)

            def ffn_up(j):
                cols = slice(j * piece, (j + 1) * piece)
                w1_cols = slice(c * FF_CHUNK + j * piece, c * FF_CHUNK + (j + 1) * piece)
                y = jnp.maximum(_bdot(h2_s[...], w1[:, w1_cols]), 0.0)
                y_s[:, cols] = (y * y).astype(BF16)

            def ffn_down(j):
                cols = slice(j * piece, (j + 1) * piece)
                acc_s[:, cols] += _bdot(y_s[...], w2[c * FF_CHUNK:(c + 1) * FF_CHUNK, cols])

            sc0 = scores(0)
            ffn_up(0)
            sc1 = scores(1)
            ffn_up(1)
            apply(0, sc0)
            ffn_up(2)
            sc2 = scores(2)
            ffn_up(3)
            apply(1, sc1)
            ffn_down(0)
            sc3 = scores(3)
            ffn_down(1)
            apply(2, sc2)
            ffn_down(2)
            apply(3, sc3)
            ffn_down(3)

        for c in range(n_chunks):
            chunk_body(c)

        x2 = x1_s[...] + mod_ref[5] * acc_s[...]
        y_ref[...] = _rms(x2) * gf[...]

        hh_s[1 - slot] = _norm_mod(xn_ref[...], g1[...], mod_ref[0], mod_ref[1])

        u_b[HIST_BASE:HIST_BASE + tl, :] = _proj(hh_s[slot], wu)
        gate_a = _sigmoid(_proj(hh_s[slot], wga))
        gate_b = _sigmoid(_proj(hh_s[slot], wgb))
        d = _pool_delta(u_b, s2_b, s4_b, s8_b, tl, t * tl)
        aout = _pool_mix(d, wpool, pscale[...])
        u_b[16:HIST_BASE, :] = u_b[16 + tl:HIST_BASE + tl, :]
        merged = gate_a * _bdot(aout, wpa[...]) + gate_b * _bdot(_load_blocks(bo_s), wpb[...])
        x1 = x_ref[...] + mod_ref[2] * _bdot(merged.astype(BF16), wout[...])
        x1_s[...] = x1

    @pl.when(t == n_tiles - 1)
    def _():
        st_ref[...] = st_s[...]
        hist_ref[...] = u_b[16:HIST_BASE, :]

    @pl.when(t == n_tiles)
    def _():
        _ffn_tile(x1_s[...], mod_ref[3], mod_ref[4], mod_ref[5], g2, w1, w2, gf, y_ref)


def _sample_mixer_kernel(x_ref, mod_ref, s0_ref, cache_ref, g1, wu, wq, wk, wv, wg, walr, wga, wgb,
                         walpha, balpha, wpool, pscale, gng, wpa, wpb, wout,
                         x1_ref, st_ref, hist_ref,
                         u_b, s2_b, s4_b, s8_b, h_s, q_s, k_s, v_s, sg_s, b_s, bo_s, u_s, ao_s):
    s = pl.program_id(0)
    seq = q_s.shape[1]
    group = s0_ref.shape[0]

    @pl.when(s == 0)
    def _():
        u_b[0:HIST_BASE, :] = jnp.zeros((HIST_BASE, POOL_WIDTH), F32)
        s2_b[0:16, :] = jnp.zeros((16, POOL_WIDTH), F32)
        _project_gla(x_ref[...], mod_ref, g1, wq, wk, wv, wg, walr, walpha, balpha, seq,
                     h_s, q_s, k_s, v_s, sg_s, b_s)
        _store_blocks(u_s, _proj(h_s[...], wu))

    scores = [[_gla_head_scores(s * group + i, h, q_s, k_s, b_s) for h in range(GLA_HEADS)]
              for i in range(group)]

    for i in range(group):
        blk = s * group + i
        u_b[HIST_BASE - POOL_HIST:HIST_BASE, :] = cache_ref[i]
        u_b[HIST_BASE:HIST_BASE + seq, :] = u_s[blk]
        d = _pool_delta(u_b, s2_b, s4_b, s8_b, seq, PAST_LEN)
        ao_s[blk] = _pool_mix(d, wpool, pscale[...])
        hist_ref[i] = u_b[HIST_BASE + seq - POOL_HIST:HIST_BASE + seq, :]

    for i in range(group):
        def get_state(h, i=i):
            return s0_ref[i, h]

        def put_state(h, s_new, i=i):
            st_ref[i, h] = s_new

        for h in range(GLA_HEADS):
            _gla_head_apply(s * group + i, h, scores[i][h], v_s, sg_s, gng[...], bo_s,
                            get_state, put_state)

    @pl.when(s == pl.num_programs(0) - 1)
    def _():
        merged_a = _gated_pool(h_s, _load_blocks(ao_s), wga, wpa)
        x1_ref[...] = _merge_out(x_ref[...], mod_ref[2], merged_a, _gate_b(h_s, wgb),
                                 _load_blocks(bo_s), wpb, wout)


def _ffn_kernel(x_ref, mod_ref, g2, w1, w2, gf, o_ref):
    _ffn_tile(x_ref[...], mod_ref[0], mod_ref[1], mod_ref[2], g2, w1, w2, gf, o_ref)


def _const_spec(shape):
    nd = len(shape)
    return pl.BlockSpec(shape, lambda *_: (0,) * nd, pipeline_mode=pl.Buffered(1))


def _params(vmem_limit=None):
    return pltpu.CompilerParams(dimension_semantics=("arbitrary",),
                                vmem_limit_bytes=vmem_limit or VMEM_LIMIT)


def _mixer_scratch(nblk, rows, hist_rows, h_slots=None):
    h_shape = (nblk * rows, D_MODEL) if h_slots is None else (h_slots, nblk * rows, D_MODEL)
    return [
        pltpu.VMEM((hist_rows, POOL_WIDTH), F32),
        pltpu.VMEM((hist_rows, POOL_WIDTH), F32),
        pltpu.VMEM((hist_rows, POOL_WIDTH - POOL_GD), F32),
        pltpu.VMEM((hist_rows, POOL_WIDTH - 2 * POOL_GD), F32),
        pltpu.VMEM(h_shape, BF16),
        pltpu.VMEM((nblk, rows, GLA_DK), F32),
        pltpu.VMEM((nblk, rows, GLA_DK), F32),
        pltpu.VMEM((nblk, rows, GLA_DV), F32),
        pltpu.VMEM((nblk, rows, GLA_DV), F32),
        pltpu.VMEM((nblk, rows, GLA_DK), F32),
        pltpu.VMEM((nblk, rows, GLA_DV), BF16),
    ]


def _adaln(c_all, w_ada, b_ada):
    rows = c_all.shape[0]
    n = w_ada.shape[1]
    return pl.pallas_call(
        _adaln_kernel,
        grid=(n // ADA_BLOCK,),
        in_specs=[pl.BlockSpec((rows, D_MODEL), lambda j: (0, 0)),
                  pl.BlockSpec((D_MODEL, ADA_BLOCK), lambda j: (0, j)),
                  pl.BlockSpec((1, ADA_BLOCK), lambda j: (0, j))],
        out_specs=pl.BlockSpec((rows, ADA_BLOCK), lambda j: (0, j)),
        out_shape=jax.ShapeDtypeStruct((rows, n), F32),
        compiler_params=_params(),
        name="adaln_mod",
    )(c_all, w_ada, b_ada)


def _prompt_layer(x, mod, weights, ffn_weights):
    n_tok = x.shape[0]
    tl = TL_MIX
    n_tiles = n_tok // tl
    consts = (mod,) + tuple(weights) + tuple(ffn_weights)
    return pl.pallas_call(
        _prompt_layer_kernel,
        grid=(n_tiles + 1,),
        in_specs=[pl.BlockSpec((tl, D_MODEL), lambda t: (jnp.minimum(t, n_tiles - 1), 0)),
                  pl.BlockSpec((tl, D_MODEL), lambda t: (jnp.minimum(t + 1, n_tiles - 1), 0))]
        + [_const_spec(w.shape) for w in consts],
        out_specs=[pl.BlockSpec((tl, D_MODEL), lambda t: (jnp.maximum(t - 1, 0), 0)),
                   pl.BlockSpec((GLA_HEADS, GLA_HK, GLA_HV), lambda t: (0, 0, 0)),
                   pl.BlockSpec((16, POOL_WIDTH), lambda t: (0, 0))],
        out_shape=[jax.ShapeDtypeStruct((n_tok, D_MODEL), F32),
                   jax.ShapeDtypeStruct((GLA_HEADS, GLA_HK, GLA_HV), F32),
                   jax.ShapeDtypeStruct((16, POOL_WIDTH), F32)],
        scratch_shapes=_mixer_scratch(tl // CHUNK, CHUNK, HIST_BASE + tl, h_slots=2) + [
            pltpu.VMEM((GLA_HEADS, GLA_HK, GLA_HV), F32),
            pltpu.VMEM((tl, D_MODEL), F32),
            pltpu.VMEM((tl, D_MODEL), BF16),
            pltpu.VMEM((tl, FF_CHUNK), BF16),
            pltpu.VMEM((tl, D_MODEL), F32),
        ],
        compiler_params=_params(VMEM_LIMIT_LAYER),
        name="prompt_layer",
    )(x, x, *consts)


def _sample_mixer(x, mod, s0, cache, weights):
    n_tok = x.shape[0]
    n_seq = s0.shape[0]
    seq = n_tok // n_seq
    group = SEQ_GROUP
    w_specs = [_const_spec(w.shape) for w in weights]
    st_spec = pl.BlockSpec((group, GLA_HEADS, GLA_HK, GLA_HV), lambda s: (s, 0, 0, 0))
    hist_spec = pl.BlockSpec((group, POOL_HIST, POOL_WIDTH), lambda s: (s, 0, 0))
    return pl.pallas_call(
        _sample_mixer_kernel,
        grid=(n_seq // group,),
        in_specs=[_const_spec(x.shape), _const_spec(mod.shape), st_spec, hist_spec] + w_specs,
        out_specs=[pl.BlockSpec((n_tok, D_MODEL), lambda s: (0, 0)), st_spec, hist_spec],
        out_shape=[jax.ShapeDtypeStruct((n_tok, D_MODEL), F32),
                   jax.ShapeDtypeStruct(s0.shape, F32),
                   jax.ShapeDtypeStruct(cache.shape, F32)],
        scratch_shapes=_mixer_scratch(n_seq, seq, HIST_BASE + seq) + [
            pltpu.VMEM((n_seq, seq, POOL_WIDTH), F32),
            pltpu.VMEM((n_seq, seq, POOL_WIDTH), BF16),
        ],
        compiler_params=_params(),
        name="sample_mixer",
    )(x, mod, s0, cache, *weights)


def _ffn_final(x, mod, g2, w1, w2, gf):
    n_tok = x.shape[0]
    tl = min(TL_FFN, n_tok)
    assert mod.shape[1] == 1 or n_tok == tl
    return pl.pallas_call(
        _ffn_kernel,
        grid=(n_tok // tl,),
        in_specs=[pl.BlockSpec((tl, D_MODEL), lambda t: (t, 0)), _const_spec(mod.shape),
                  _const_spec(g2.shape), _const_spec(w1.shape), _const_spec(w2.shape),
                  _const_spec(gf.shape)],
        out_specs=pl.BlockSpec((tl, D_MODEL), lambda t: (t, 0)),
        out_shape=jax.ShapeDtypeStruct((n_tok, D_MODEL), F32),
        compiler_params=_params(),
        name="ffn_final",
    )(x, mod, g2, w1, w2, gf)


def kernel(x_prompt, x_sample, c_prompt, c_sample, state_gla, cache_pool, w_ada, b_ada, norm1_g,
           w_in, w_alpha, b_alpha, w_pool, pool_scale, gla_norm_g, w_pa, w_pb, w_out, norm2_g,
           w_ff1, w_ff2, final_g):
    n_batch, n_seq_p, _ = x_prompt.shape
    n_dec, n_seq_s, _ = x_sample.shape
    assert n_batch == 1 and w_ada.shape[0] == 1
    assert n_seq_p % TL_MIX == 0 and SLAB % n_seq_s == 0 and n_seq_s % SUB == 0
    assert n_dec % SEQ_GROUP == 0

    n_c = n_batch + n_dec
    pad = (-n_c) % 8
    c_all = jnp.concatenate([c_prompt, c_sample, jnp.zeros((pad, D_MODEL), F32)], axis=0)
    mod = _adaln(c_all, w_ada[0], b_ada)
    mod = mod.reshape(n_c + pad, 6, D_MODEL).transpose(1, 0, 2)
    mod_p = mod[:, 0:1]
    mod_s = mod[:, n_batch:n_c]

    offs = [0]
    for sz in IN_SIZES:
        offs.append(offs[-1] + sz)
    w_in_t = jnp.swapaxes(w_in[0], 0, 1).astype(BF16)
    wu, wq, wk, wv, wg, walr, wga, wgb = [w_in_t[offs[i]:offs[i + 1]] for i in range(8)]
    walr = jnp.pad(walr, ((0, LANES - GLA_LOWRANK), (0, 0)))
    walpha = jnp.pad(w_alpha[0].astype(BF16), ((0, LANES - GLA_LOWRANK), (0, 0)))
    weights = (norm1_g, wu, wq, wk, wv, wg, walr, wga, wgb,
               walpha, b_alpha, w_pool[0].astype(BF16), pool_scale, gla_norm_g,
               w_pa[0].astype(BF16), w_pb[0].astype(BF16), w_out[0].astype(BF16))

    w1 = w_ff1[0].astype(BF16)
    w2 = w_ff2[0].astype(BF16)
    gf = final_g.reshape(1, D_MODEL)

    y_p, st_p, hist_p = _prompt_layer(x_prompt[0], mod_p, weights, (norm2_g, w1, w2, gf))
    x1_s, st_s, hist_s = _sample_mixer(x_sample.reshape(n_dec * n_seq_s, D_MODEL), mod_s[0:3],
                                       state_gla[0], cache_pool[0], weights)
    y_s = _ffn_final(x1_s, mod_s[3:6], norm2_g, w1, w2, gf)

    return (y_p[None], y_s.reshape(n_dec, n_seq_s, D_MODEL), st_p[None, None],
            hist_p[None, None, 1:], st_s[None], hist_s[None])
```

```python
import jax
import jax.numpy as jnp
from jax import lax
from jax.experimental import pallas as pl
from jax.experimental.pallas import tpu as pltpu

D_MODEL = 1024
PAST_LEN = 4096
POOL_WIDTH = 512
POOL_WINDOWS = (2, 4, 8, 16)
POOL_GD = 128
POOL_HIST = 15
GLA_HEADS = 4
GLA_DK = 512
GLA_DV = 1024
GLA_HK = 128
GLA_HV = 256
GLA_LOWRANK = 16
GLA_GATE_NORM = 16.0
D_FF = 4096
EPS = 1e-6
LOG2_E = 1.4426950408889634
LANES = 128
IN_SIZES = (POOL_WIDTH, GLA_DK, GLA_DK, GLA_DV, GLA_DV, GLA_LOWRANK, D_MODEL, D_MODEL)

SUB = 8
CHUNK = 64
SLAB = 64
HIST_BASE = 32
TL_MIX = 256
FF_CHUNK = 1024
ADA_BLOCK = 1536
SEQ_GROUP = 4
CONV_ROWS = 256
VMEM_LIMIT = 48 * 1024 * 1024
VMEM_LIMIT_LAYER = 60 * 1024 * 1024

F32 = jnp.float32
BF16 = jnp.bfloat16


def _bdot(a, b):
    return jnp.dot(a, b, preferred_element_type=F32)


def _proj(h, wt_ref):
    return lax.dot_general(h, wt_ref[...], (((1,), (1,)), ((), ())), preferred_element_type=F32)


def _rms(xf):
    return xf * lax.rsqrt(jnp.mean(xf * xf, axis=-1, keepdims=True) + EPS)


def _sigmoid(x):
    return 0.5 * jnp.tanh(0.5 * x) + 0.5


def _log_sigmoid(z):
    return jnp.minimum(z, 0.0) - jnp.log(1.0 + jnp.exp(-jnp.abs(z)))


def _rows_per_seq(m, n_rows):
    n_seq = m.shape[0]
    if n_seq == 1:
        return m
    rep = n_rows // n_seq
    return jnp.concatenate([jnp.broadcast_to(m[i:i + 1], (rep, m.shape[1])) for i in range(n_seq)],
                           axis=0)


def _norm_mod(x, g, shift, scale):
    n = x.shape[0]
    return (_rms(x) * g * (1.0 + _rows_per_seq(scale, n)) + _rows_per_seq(shift, n)).astype(BF16)


def _store_blocks(ref, val):
    nblk, rows = ref.shape[0], ref.shape[1]
    for i in range(nblk):
        ref[i] = val[i * rows:(i + 1) * rows].astype(ref.dtype)


def _load_blocks(ref):
    return jnp.concatenate([ref[i] for i in range(ref.shape[0])], axis=0)


def _cumsum_rows(a, period):
    n = a.shape[0]
    ri = lax.broadcasted_iota(jnp.int32, (n, n), 0)
    ci = lax.broadcasted_iota(jnp.int32, (n, n), 1)
    shift = period.bit_length() - 1
    same = jnp.right_shift(ri, shift) == jnp.right_shift(ci, shift)
    tri = jnp.where((ci <= ri) & same, 1.0, 0.0).astype(BF16)
    hi = a.astype(BF16)
    r1 = a - hi.astype(F32)
    mid = r1.astype(BF16)
    lo = (r1 - mid.astype(F32)).astype(BF16)
    return _bdot(tri, hi) + _bdot(tri, mid) + _bdot(tri, lo)


def _gla_scores(q, k, b, k_row, b_row, nsub):
    L = SUB * nsub
    blast = b_row(L - 1)
    qt = (q * jnp.exp2(b)).astype(BF16)
    kt = (k * jnp.exp2(blast - b)).astype(BF16)

    lane = lax.broadcasted_iota(jnp.int32, (SUB, L), 1)
    row = lax.broadcasted_iota(jnp.int32, (SUB, L), 0)
    qs = [q[SUB * i:SUB * (i + 1)] for i in range(nsub)]
    ks = [k[SUB * i:SUB * (i + 1)] for i in range(nsub)]
    bs = [b[SUB * i:SUB * (i + 1)] for i in range(nsub)]

    diag = []
    for i in range(nsub):
        acc = jnp.zeros((SUB, L), F32)
        for j in range(SUB):
            r = SUB * i + j
            dec = jnp.exp2(bs[i] - b_row(r))
            col = jnp.sum(qs[i] * dec * k_row(r), axis=1, keepdims=True)
            acc = jnp.where(lane == r, col, acc)
        diag.append(jnp.where(lane <= row + SUB * i, acc, 0.0))
    p = diag[0] if nsub == 1 else jnp.concatenate(diag, axis=0)

    p_off = None
    if nsub > 1:
        zero = jnp.zeros((SUB, GLA_HK), F32)
        lhs, rhs = [], []
        for j in range(nsub - 1):
            bend = b_row(SUB * j + SUB - 1)
            lrows = [zero if i <= j else qs[i] * jnp.exp2(bs[i] - bend) for i in range(nsub)]
            rrows = [ks[j] * jnp.exp2(bend - bs[j]) if i == j else zero for i in range(nsub)]
            lhs.append(jnp.concatenate(lrows, axis=0).astype(BF16))
            rhs.append(jnp.concatenate(rrows, axis=0).astype(BF16))
        lhs = jnp.concatenate(lhs, axis=1)
        rhs = jnp.concatenate(rhs, axis=1)
        p_off = lax.dot_general(lhs, rhs, (((1,), (1,)), ((), ())), preferred_element_type=F32)
    return qt, kt, p, p_off, blast


def _gla_apply(scores, v, s):
    qt, kt, p, p_off, blast = scores
    if p_off is not None:
        p = p + p_off
    o = _bdot(jnp.concatenate([qt, p.astype(BF16)], axis=1),
              jnp.concatenate([s.astype(BF16), v], axis=0))

    ri = lax.broadcasted_iota(jnp.int32, (GLA_HK, GLA_HK), 0)
    ci = lax.broadcasted_iota(jnp.int32, (GLA_HK, GLA_HK), 1)
    erow = jnp.broadcast_to(jnp.exp2(blast), (GLA_HK, GLA_HK))
    ecol = jnp.sum(jnp.where(ri == ci, erow, 0.0), axis=1, keepdims=True)
    s_new = s * ecol + lax.dot_general(kt, v, (((0,), (0,)), ((), ())), preferred_element_type=F32)
    return o, s_new


def _gla_head_scores(blk, h, q_s, k_s, b_s):
    ksl = slice(h * GLA_HK, (h + 1) * GLA_HK)
    k_row = lambda r: k_s[blk, r:r + 1, ksl]
    b_row = lambda r: b_s[blk, r:r + 1, ksl]
    return _gla_scores(q_s[blk, :, ksl], k_s[blk, :, ksl], b_s[blk, :, ksl], k_row, b_row,
                       q_s.shape[1] // SUB)


def _gla_head_apply(blk, h, scores, v_s, sg_s, gng, bo_s, get_state, put_state):
    vsl = slice(h * GLA_HV, (h + 1) * GLA_HV)
    o, s_new = _gla_apply(scores, v_s[blk, :, vsl].astype(BF16), get_state(h))
    put_state(h, s_new)
    o = _rms(o) * gng
    bo_s[blk, :, vsl] = (o * sg_s[blk, :, vsl]).astype(BF16)


def _pool_delta(u_b, s2_b, s4_b, s8_b, n, pos0):
    r = HIST_BASE + n
    gd = POOL_GD
    s2_b[16:r, :] = u_b[16:r, :] + u_b[15:r - 1, :]
    s4_b[16:r, :] = s2_b[16:r, gd:] + s2_b[14:r - 2, gd:]
    s8_b[24:r, :] = s4_b[24:r, gd:] + s4_b[20:r - 4, gd:]
    s16 = s8_b[HIST_BASE:r, gd:] + s8_b[HIST_BASE - 8:r - 8, gd:]
    sums = (s2_b[HIST_BASE:r, 0:gd], s4_b[HIST_BASE:r, 0:gd], s8_b[HIST_BASE:r, 0:gd], s16)
    pos1 = pos0 + lax.broadcasted_iota(jnp.int32, (n, 1), 0) + 1
    out = []
    for gi, w in enumerate(POOL_WINDOWS):
        cnt = jnp.minimum(pos1, w).astype(F32)
        out.append(sums[gi] / cnt - u_b[HIST_BASE:r, gi * POOL_GD:(gi + 1) * POOL_GD])
    return out


def _pool_mix(d, wpool, pscale):
    mixed = [_bdot(d[gi].astype(BF16), wpool[gi]) for gi in range(len(POOL_WINDOWS))]
    return (jnp.concatenate(mixed, axis=1) * pscale).astype(BF16)


def _project_gla(x, mod_ref, g1, wq, wk, wv, wg, walr, walpha, balpha, period,
                 h_s, q_s, k_s, v_s, sg_s, b_s):
    h = _norm_mod(x, g1[...], mod_ref[0], mod_ref[1])
    h_s[...] = h
    _store_blocks(q_s, _proj(h, wq) * (GLA_HK ** -0.5))
    _store_blocks(k_s, _proj(h, wk))
    _store_blocks(v_s, _proj(h, wv))
    g = _proj(h, wg)
    _store_blocks(sg_s, g * _sigmoid(g))
    alr = _proj(h, walr)
    z = _bdot(alr.astype(BF16), walpha[...]) + balpha[...]
    a = _log_sigmoid(z) * (LOG2_E / GLA_GATE_NORM)
    b = [_cumsum_rows(a[r:r + SLAB], period) for r in range(0, a.shape[0], SLAB)]
    _store_blocks(b_s, jnp.concatenate(b, axis=0))


def _gated_pool(h_s, aout, wga, wpa):
    return _sigmoid(_proj(h_s[...], wga)) * _bdot(aout, wpa[...])


def _gate_b(h_s, wgb):
    return _sigmoid(_proj(h_s[...], wgb))


def _merge_out(x, gate1, merged_a, gate_b, bo, wpb, wout):
    merged = merged_a + gate_b * _bdot(bo, wpb[...])
    y = _bdot(merged.astype(BF16), wout[...])
    return x + _rows_per_seq(gate1, x.shape[0]) * y


def _ffn_tile(x, shift, scale, gate, g2, w1, w2, gf, o_ref):
    h2 = _norm_mod(x, g2[...], shift, scale)
    acc = jnp.zeros(x.shape, F32)
    for c in range(D_FF // FF_CHUNK):
        sl = slice(c * FF_CHUNK, (c + 1) * FF_CHUNK)
        y = jnp.maximum(_bdot(h2, w1[:, sl]), 0.0)
        acc = acc + _bdot((y * y).astype(BF16), w2[sl, :])
    x2 = x + _rows_per_seq(gate, x.shape[0]) * acc
    o_ref[...] = _rms(x2) * gf[...]


def _adaln_kernel(c_ref, w_ref, b_ref, o_ref):
    c = c_ref[...]
    sc = (c * _sigmoid(c)).astype(BF16)
    o_ref[...] = _bdot(sc, w_ref[...].astype(BF16)) + b_ref[...]


def _prompt_layer_kernel(x_ref, xn_ref, mod_ref, g1, wu, wq, wk, wv, wg, walr, wga, wgb, walpha, balpha,
                         wpool, pscale, gng, wpa, wpb, wout, g2, w1_hbm, w2_hbm, gf,
                         y_ref, st_ref, hist_ref,
                         u_b, s2_b, s4_b, s8_b, hh_s, q_s, k_s, v_s, sg_s, b_s, bo_s,
                         st_s, x1_s, h2_s, y_s, acc_s, w1, w2, stage, conv_sem):
    t = pl.program_id(0)
    n_tiles = pl.num_programs(0) - 1
    tl = x_ref.shape[0]
    n_chunks = tl // CHUNK
    n_piece = GLA_HEADS
    piece = D_MODEL // n_piece
    assert n_chunks * FF_CHUNK == w1.shape[1] and FF_CHUNK == n_piece * piece

    conv_rows, conv_cols = stage.shape[1], stage.shape[2]
    plan = [(w1_hbm, w1, r, c) for r in range(0, D_MODEL, conv_rows)
            for c in range(0, D_FF, conv_cols)]
    plan += [(w2_hbm, w2, r, 0) for r in range(0, D_FF, conv_rows)]
    n_points = n_chunks * GLA_HEADS
    assert len(plan) % n_points == 0 and conv_cols == D_MODEL

    def conv_copy(i):
        src, _, r, c = plan[i]
        return pltpu.make_async_copy(src.at[pl.ds(r, conv_rows), pl.ds(c, conv_cols)],
                                     stage.at[i % 2], conv_sem.at[i % 2])

    def conv_start(i):
        if i < len(plan):
            conv_copy(i).start()

    def conv_finish(i):
        conv_copy(i).wait()
        _, dst, r, c = plan[i]
        dst[r:r + conv_rows, c:c + conv_cols] = stage[i % 2].astype(BF16)
        conv_start(i + 2)

    @pl.when(t == 0)
    def _():
        st_s[...] = jnp.zeros_like(st_s)
        u_b[0:HIST_BASE, :] = jnp.zeros((HIST_BASE, POOL_WIDTH), F32)
        s2_b[0:16, :] = jnp.zeros((16, POOL_WIDTH), F32)
        hh_s[0] = _norm_mod(x_ref[...], g1[...], mod_ref[0], mod_ref[1])

    def step(first):
        slot = lax.rem(t, 2)
        if first:
            conv_start(0)
            conv_start(1)
        alr = _proj(hh_s[slot], walr)
        _store_blocks(k_s, _proj(hh_s[slot], wk))
        if not first:
            h2_s[...] = _norm_mod(x1_s[...], g2[...], mod_ref[3], mod_ref[4])
            acc_s[...] = jnp.zeros_like(acc_s)
        z =_bdot(alr.astype(BF16), walpha[...]) + balpha[...]
        a = _log_sigmoid(z) * (LOG2_E / GLA_GATE_NORM)
        _store_blocks(q_s, _proj(hh_s[slot], wq) * (GLA_HK ** -0.5))
        b = [_cumsum_rows(a[r:r + SLAB], CHUNK) for r in range(0, tl, SLAB)]
        _store_blocks(b_s, jnp.concatenate(b, axis=0))
        _store_blocks(v_s, _proj(hh_s[slot], wv))
        g = _proj(hh_s[slot], wg)
        _store_blocks(sg_s, g * _sigmoid(g))

        def get_state(h):
            return st_s[h]

        def put_state(h, s_new):
            st_s[h] = s_new

        def chunk_body(c):
            def scores(h):
                return _gla_head_scores(c, h, q_s, k_s, b_s)

            def apply(h, sc):
                _gla_head_apply(c, h, sc, v_s, sg_s, gng[...], bo_s, get_state, put_state)
                if first:
                    per_point = len(plan) // n_points
                    p0 = (c * GLA_HEADS + h) * per_point
                    for i in range(p0, p0 + per_point):
                        conv_finish(i)

            def ffn_up(j):
                if first:
                    return
                cols = slice(j * piece, (j + 1) * piece)
                w1_cols = slice(c * FF_CHUNK + j * piece, c * FF_CHUNK + (j + 1) * piece)
                y = jnp.maximum(_bdot(h2_s[...], w1[:, w1_cols]), 0.0)
                y_s[:, cols] = (y * y).astype(BF16)

            def ffn_down(j):
                if first:
                    return
                cols = slice(j * piece, (j + 1) * piece)
                acc_s[:, cols] += _bdot(y_s[...], w2[c * FF_CHUNK:(c + 1) * FF_CHUNK, cols])

            sc0 = scores(0)
            ffn_up(0)
            sc1 = scores(1)
            ffn_up(1)
            apply(0, sc0)
            ffn_up(2)
            sc2 = scores(2)
            ffn_up(3)
            apply(1, sc1)
            ffn_down(0)
            sc3 = scores(3)
            ffn_down(1)
            apply(2, sc2)
            ffn_down(2)
            apply(3, sc3)
            ffn_down(3)

        for c in range(n_chunks):
            chunk_body(c)

        if not first:
            x2 = x1_s[...] + mod_ref[5] * acc_s[...]
            y_ref[...] = _rms(x2) * gf[...]

        hh_s[1 - slot] = _norm_mod(xn_ref[...], g1[...], mod_ref[0], mod_ref[1])

        u_b[HIST_BASE:HIST_BASE + tl, :] = _proj(hh_s[slot], wu)
        gate_a = _sigmoid(_proj(hh_s[slot], wga))
        gate_b = _sigmoid(_proj(hh_s[slot], wgb))
        d = _pool_delta(u_b, s2_b, s4_b, s8_b, tl, t * tl)
        aout = _pool_mix(d, wpool, pscale[...])
        u_b[16:HIST_BASE, :] = u_b[16 + tl:HIST_BASE + tl, :]
        merged = gate_a * _bdot(aout, wpa[...]) + gate_b * _bdot(_load_blocks(bo_s), wpb[...])
        x1 = x_ref[...] + mod_ref[2] * _bdot(merged.astype(BF16), wout[...])
        x1_s[...] = x1

    pl.when(t == 0)(lambda: step(True))
    pl.when((t > 0) & (t < n_tiles))(lambda: step(False))

    @pl.when(t == n_tiles - 1)
    def _():
        st_ref[...] = st_s[...]
        hist_ref[...] = u_b[16:HIST_BASE, :]

    @pl.when(t == n_tiles)
    def _():
        _ffn_tile(x1_s[...], mod_ref[3], mod_ref[4], mod_ref[5], g2, w1, w2, gf, y_ref)


def _sample_mixer_kernel(x_ref, mod_ref, s0_ref, cache_ref, g1, wu, wq, wk, wv, wg, walr, wga, wgb,
                         walpha, balpha, wpool, pscale, gng, wpa, wpb, wout,
                         x1_ref, st_ref, hist_ref,
                         u_b, s2_b, s4_b, s8_b, h_s, q_s, k_s, v_s, sg_s, b_s, bo_s, u_s, ao_s):
    s = pl.program_id(0)
    seq = q_s.shape[1]
    group = s0_ref.shape[0]

    @pl.when(s == 0)
    def _():
        u_b[0:HIST_BASE, :] = jnp.zeros((HIST_BASE, POOL_WIDTH), F32)
        s2_b[0:16, :] = jnp.zeros((16, POOL_WIDTH), F32)
        _project_gla(x_ref[...], mod_ref, g1, wq, wk, wv, wg, walr, walpha, balpha, seq,
                     h_s, q_s, k_s, v_s, sg_s, b_s)
        _store_blocks(u_s, _proj(h_s[...], wu))

    scores = [[_gla_head_scores(s * group + i, h, q_s, k_s, b_s) for h in range(GLA_HEADS)]
              for i in range(group)]

    for i in range(group):
        blk = s * group + i
        u_b[HIST_BASE - POOL_HIST:HIST_BASE, :] = cache_ref[i]
        u_b[HIST_BASE:HIST_BASE + seq, :] = u_s[blk]
        d = _pool_delta(u_b, s2_b, s4_b, s8_b, seq, PAST_LEN)
        ao_s[blk] = _pool_mix(d, wpool, pscale[...])
        hist_ref[i] = u_b[HIST_BASE + seq - POOL_HIST:HIST_BASE + seq, :]

    for i in range(group):
        def get_state(h, i=i):
            return s0_ref[i, h]

        def put_state(h, s_new, i=i):
            st_ref[i, h] = s_new

        for h in range(GLA_HEADS):
            _gla_head_apply(s * group + i, h, scores[i][h], v_s, sg_s, gng[...], bo_s,
                            get_state, put_state)

    @pl.when(s == pl.num_programs(0) - 1)
    def _():
        merged_a = _gated_pool(h_s, _load_blocks(ao_s), wga, wpa)
        x1_ref[...] = _merge_out(x_ref[...], mod_ref[2], merged_a, _gate_b(h_s, wgb),
                                 _load_blocks(bo_s), wpb, wout)


def _ffn_kernel(x_ref, mod_ref, g2, w1_blk, w2_blk, gf, o_ref, h2_s, acc_s):
    c = pl.program_id(0)

    @pl.when(c == 0)
    def _():
        h2_s[...] = _norm_mod(x_ref[...], g2[...], mod_ref[0], mod_ref[1])
        acc_s[...] = jnp.zeros_like(acc_s)

    y = jnp.maximum(_bdot(h2_s[...], w1_blk[...].astype(BF16)), 0.0)
    acc_s[...] += _bdot((y * y).astype(BF16), w2_blk[...].astype(BF16))

    @pl.when(c == pl.num_programs(0) - 1)
    def _():
        x = x_ref[...]
        x2 = x + _rows_per_seq(mod_ref[2], x.shape[0]) * acc_s[...]
        o_ref[...] = _rms(x2) * gf[...]


def _const_spec(shape):
    nd = len(shape)
    return pl.BlockSpec(shape, lambda *_: (0,) * nd, pipeline_mode=pl.Buffered(1))


def _params(vmem_limit=None):
    return pltpu.CompilerParams(dimension_semantics=("arbitrary",),
                                vmem_limit_bytes=vmem_limit or VMEM_LIMIT)


def _mixer_scratch(nblk, rows, hist_rows, h_slots=None):
    h_shape = (nblk * rows, D_MODEL) if h_slots is None else (h_slots, nblk * rows, D_MODEL)
    return [
        pltpu.VMEM((hist_rows, POOL_WIDTH), F32),
        pltpu.VMEM((hist_rows, POOL_WIDTH), F32),
        pltpu.VMEM((hist_rows, POOL_WIDTH - POOL_GD), F32),
        pltpu.VMEM((hist_rows, POOL_WIDTH - 2 * POOL_GD), F32),
        pltpu.VMEM(h_shape, BF16),
        pltpu.VMEM((nblk, rows, GLA_DK), F32),
        pltpu.VMEM((nblk, rows, GLA_DK), F32),
        pltpu.VMEM((nblk, rows, GLA_DV), F32),
        pltpu.VMEM((nblk, rows, GLA_DV), F32),
        pltpu.VMEM((nblk, rows, GLA_DK), F32),
        pltpu.VMEM((nblk, rows, GLA_DV), BF16),
    ]


def _adaln(c_all, w_ada, b_ada):
    rows = c_all.shape[0]
    n = w_ada.shape[1]
    return pl.pallas_call(
        _adaln_kernel,
        grid=(n // ADA_BLOCK,),
        in_specs=[pl.BlockSpec((rows, D_MODEL), lambda j: (0, 0)),
                  pl.BlockSpec((D_MODEL, ADA_BLOCK), lambda j: (0, j)),
                  pl.BlockSpec((1, ADA_BLOCK), lambda j: (0, j))],
        out_specs=pl.BlockSpec((rows, ADA_BLOCK), lambda j: (0, j)),
        out_shape=jax.ShapeDtypeStruct((rows, n), F32),
        compiler_params=_params(),
        name="adaln_mod",
    )(c_all, w_ada, b_ada)


def _prompt_layer(x, mod, weights, ffn_weights):
    n_tok = x.shape[0]
    tl = TL_MIX
    n_tiles = n_tok // tl
    g2, w1_f32, w2_f32, gf = ffn_weights
    consts = (mod,) + tuple(weights) + (g2, w1_f32, w2_f32, gf)
    hbm_resident = (w1_f32, w2_f32)
    const_specs = [pl.BlockSpec(memory_space=pl.ANY) if any(w is h for h in hbm_resident)
                   else _const_spec(w.shape) for w in consts]
    return pl.pallas_call(
        _prompt_layer_kernel,
        grid=(n_tiles + 1,),
        in_specs=[pl.BlockSpec((tl, D_MODEL), lambda t: (jnp.minimum(t, n_tiles - 1), 0)),
                  pl.BlockSpec((tl, D_MODEL), lambda t: (jnp.minimum(t + 1, n_tiles - 1), 0))]
        + const_specs,
        out_specs=[pl.BlockSpec((tl, D_MODEL), lambda t: (jnp.maximum(t - 1, 0), 0)),
                   pl.BlockSpec((GLA_HEADS, GLA_HK, GLA_HV), lambda t: (0, 0, 0)),
                   pl.BlockSpec((16, POOL_WIDTH), lambda t: (0, 0))],
        out_shape=[jax.ShapeDtypeStruct((n_tok, D_MODEL), F32),
                   jax.ShapeDtypeStruct((GLA_HEADS, GLA_HK, GLA_HV), F32),
                   jax.ShapeDtypeStruct((16, POOL_WIDTH), F32)],
        scratch_shapes=_mixer_scratch(tl // CHUNK, CHUNK, HIST_BASE + tl, h_slots=2) + [
            pltpu.VMEM((GLA_HEADS, GLA_HK, GLA_HV), F32),
            pltpu.VMEM((tl, D_MODEL), F32),
            pltpu.VMEM((tl, D_MODEL), BF16),
            pltpu.VMEM((tl, FF_CHUNK), BF16),
            pltpu.VMEM((tl, D_MODEL), F32),
            pltpu.VMEM((D_MODEL, D_FF), BF16),
            pltpu.VMEM((D_FF, D_MODEL), BF16),
            pltpu.VMEM((2, CONV_ROWS, D_MODEL), F32),
            pltpu.SemaphoreType.DMA((2,)),
        ],
        compiler_params=_params(VMEM_LIMIT_LAYER),
        name="prompt_layer",
    )(x, x, *consts)


def _sample_mixer(x, mod, s0, cache, weights):
    n_tok = x.shape[0]
    n_seq = s0.shape[0]
    seq = n_tok // n_seq
    group = SEQ_GROUP
    w_specs = [_const_spec(w.shape) for w in weights]
    st_spec = pl.BlockSpec((group, GLA_HEADS, GLA_HK, GLA_HV), lambda s: (s, 0, 0, 0))
    hist_spec = pl.BlockSpec((group, POOL_HIST, POOL_WIDTH), lambda s: (s, 0, 0))
    return pl.pallas_call(
        _sample_mixer_kernel,
        grid=(n_seq // group,),
        in_specs=[_const_spec(x.shape), _const_spec(mod.shape), st_spec, hist_spec] + w_specs,
        out_specs=[pl.BlockSpec((n_tok, D_MODEL), lambda s: (0, 0)), st_spec, hist_spec],
        out_shape=[jax.ShapeDtypeStruct((n_tok, D_MODEL), F32),
                   jax.ShapeDtypeStruct(s0.shape, F32),
                   jax.ShapeDtypeStruct(cache.shape, F32)],
        scratch_shapes=_mixer_scratch(n_seq, seq, HIST_BASE + seq) + [
            pltpu.VMEM((n_seq, seq, POOL_WIDTH), F32),
            pltpu.VMEM((n_seq, seq, POOL_WIDTH), BF16),
        ],
        compiler_params=_params(),
        name="sample_mixer",
    )(x, mod, s0, cache, *weights)


def _ffn_final(x, mod, g2, w1, w2, gf):
    n_tok = x.shape[0]
    return pl.pallas_call(
        _ffn_kernel,
        grid=(D_FF // FF_CHUNK,),
        in_specs=[_const_spec(x.shape), _const_spec(mod.shape), _const_spec(g2.shape),
                  pl.BlockSpec((D_MODEL, FF_CHUNK), lambda c: (0, c)),
                  pl.BlockSpec((FF_CHUNK, D_MODEL), lambda c: (c, 0)),
                  _const_spec(gf.shape)],
        out_specs=pl.BlockSpec((n_tok, D_MODEL), lambda c: (0, 0)),
        out_shape=jax.ShapeDtypeStruct((n_tok, D_MODEL), F32),
        scratch_shapes=[pltpu.VMEM((n_tok, D_MODEL), BF16),
                        pltpu.VMEM((n_tok, D_MODEL), F32)],
        compiler_params=_params(),
        name="ffn_final",
    )(x, mod, g2, w1, w2, gf)


def kernel(x_prompt, x_sample, c_prompt, c_sample, state_gla, cache_pool, w_ada, b_ada, norm1_g,
           w_in, w_alpha, b_alpha, w_pool, pool_scale, gla_norm_g, w_pa, w_pb, w_out, norm2_g,
           w_ff1, w_ff2, final_g):
    n_batch, n_seq_p, _ = x_prompt.shape
    n_dec, n_seq_s, _ = x_sample.shape
    assert n_batch == 1 and w_ada.shape[0] == 1
    assert n_seq_p % TL_MIX == 0 and SLAB % n_seq_s == 0 and n_seq_s % SUB == 0
    assert n_dec % SEQ_GROUP == 0

    n_c = n_batch + n_dec
    pad = (-n_c) % 8
    c_all = jnp.concatenate([c_prompt, c_sample, jnp.zeros((pad, D_MODEL), F32)], axis=0)
    mod = _adaln(c_all, w_ada[0], b_ada)
    mod = mod.reshape(n_c + pad, 6, D_MODEL).transpose(1, 0, 2)
    mod_p = mod[:, 0:1]
    mod_s = mod[:, n_batch:n_c]

    offs = [0]
    for sz in IN_SIZES:
        offs.append(offs[-1] + sz)
    w_in_t = jnp.swapaxes(w_in[0], 0, 1).astype(BF16)
    wu, wq, wk, wv, wg, walr, wga, wgb = [w_in_t[offs[i]:offs[i + 1]] for i in range(8)]
    walr = jnp.pad(walr, ((0, LANES - GLA_LOWRANK), (0, 0)))
    walpha = jnp.pad(w_alpha[0].astype(BF16), ((0, LANES - GLA_LOWRANK), (0, 0)))
    weights = (norm1_g, wu, wq, wk, wv, wg, walr, wga, wgb,
               walpha, b_alpha, w_pool[0].astype(BF16), pool_scale, gla_norm_g,
               w_pa[0].astype(BF16), w_pb[0].astype(BF16), w_out[0].astype(BF16))

    w1 = w_ff1[0]
    w2 = w_ff2[0]
    gf = final_g.reshape(1, D_MODEL)

    y_p, st_p, hist_p = _prompt_layer(x_prompt[0], mod_p, weights, (norm2_g, w1, w2, gf))
    x1_s, st_s, hist_s = _sample_mixer(x_sample.reshape(n_dec * n_seq_s, D_MODEL), mod_s[0:3],
                                       state_gla[0], cache_pool[0], weights)
    y_s = _ffn_final(x1_s, mod_s[3:6], norm2_g, w1, w2, gf)

    return (y_p[None], y_s.reshape(n_dec, n_seq_s, D_MODEL), st_p[None, None],
            hist_p[None, None, 1:], st_s[None], hist_s[None])
```

```python
import jax
import jax.numpy as jnp
from jax import lax
from jax.experimental import pallas as pl
from jax.experimental.pallas import tpu as pltpu

D_MODEL = 1024
PAST_LEN = 4096
POOL_WIDTH = 512
POOL_WINDOWS = (2, 4, 8, 16)
POOL_GD = 128
POOL_HIST = 15
GLA_HEADS = 4
GLA_DK = 512
GLA_DV = 1024
GLA_HK = 128
GLA_HV = 256
GLA_LOWRANK = 16
GLA_GATE_NORM = 16.0
D_FF = 4096
EPS = 1e-6
LOG2_E = 1.4426950408889634
LANES = 128
IN_SIZES = (POOL_WIDTH, GLA_DK, GLA_DK, GLA_DV, GLA_DV, GLA_LOWRANK, D_MODEL, D_MODEL)

SUB = 8
CHUNK = 64
SLAB = 64
HIST_BASE = 32
TL_MIX = 256
TL_FFN = 512
FF_CHUNK = 1024
ADA_BLOCK = 1536
SEQ_GROUP = 4
VMEM_LIMIT = 48 * 1024 * 1024
VMEM_LIMIT_LAYER = 60 * 1024 * 1024

F32 = jnp.float32
BF16 = jnp.bfloat16


def _bdot(a, b):
    return jnp.dot(a, b, preferred_element_type=F32)


def _proj(h, wt_ref):
    return lax.dot_general(h, wt_ref[...], (((1,), (1,)), ((), ())), preferred_element_type=F32)


def _rms(xf):
    return xf * lax.rsqrt(jnp.mean(xf * xf, axis=-1, keepdims=True) + EPS)


def _sigmoid(x):
    return 0.5 * jnp.tanh(0.5 * x) + 0.5


def _log_sigmoid(z):
    return jnp.minimum(z, 0.0) - jnp.log(1.0 + jnp.exp(-jnp.abs(z)))


def _rows_per_seq(m, n_rows):
    n_seq = m.shape[0]
    if n_seq == 1:
        return m
    rep = n_rows // n_seq
    return jnp.concatenate([jnp.broadcast_to(m[i:i + 1], (rep, m.shape[1])) for i in range(n_seq)],
                           axis=0)


def _norm_mod(x, g, shift, scale):
    n = x.shape[0]
    return (_rms(x) * g * (1.0 + _rows_per_seq(scale, n)) + _rows_per_seq(shift, n)).astype(BF16)


def _store_blocks(ref, val):
    nblk, rows = ref.shape[0], ref.shape[1]
    for i in range(nblk):
        ref[i] = val[i * rows:(i + 1) * rows].astype(ref.dtype)


def _load_blocks(ref):
    return jnp.concatenate([ref[i] for i in range(ref.shape[0])], axis=0)


def _cumsum_rows(a, period):
    n = a.shape[0]
    ri = lax.broadcasted_iota(jnp.int32, (n, n), 0)
    ci = lax.broadcasted_iota(jnp.int32, (n, n), 1)
    shift = period.bit_length() - 1
    same = jnp.right_shift(ri, shift) == jnp.right_shift(ci, shift)
    tri = jnp.where((ci <= ri) & same, 1.0, 0.0).astype(BF16)
    hi = a.astype(BF16)
    r1 = a - hi.astype(F32)
    mid = r1.astype(BF16)
    lo = (r1 - mid.astype(F32)).astype(BF16)
    return _bdot(tri, hi) + _bdot(tri, mid) + _bdot(tri, lo)


def _gla_scores(q, k, b, k_row, b_row, nsub):
    L = SUB * nsub
    blast = b_row(L - 1)
    qt = (q * jnp.exp2(b)).astype(BF16)
    kt = (k * jnp.exp2(blast - b)).astype(BF16)

    lane = lax.broadcasted_iota(jnp.int32, (SUB, L), 1)
    row = lax.broadcasted_iota(jnp.int32, (SUB, L), 0)
    qs = [q[SUB * i:SUB * (i + 1)] for i in range(nsub)]
    ks = [k[SUB * i:SUB * (i + 1)] for i in range(nsub)]
    bs = [b[SUB * i:SUB * (i + 1)] for i in range(nsub)]

    diag = []
    for i in range(nsub):
        acc = jnp.zeros((SUB, L), F32)
        for j in range(SUB):
            r = SUB * i + j
            dec = jnp.exp2(bs[i] - b_row(r))
            col = jnp.sum(qs[i] * dec * k_row(r), axis=1, keepdims=True)
            acc = jnp.where(lane == r, col, acc)
        diag.append(jnp.where(lane <= row + SUB * i, acc, 0.0))
    p = diag[0] if nsub == 1 else jnp.concatenate(diag, axis=0)

    p_off = None
    if nsub > 1:
        zero = jnp.zeros((SUB, GLA_HK), F32)
        lhs, rhs = [], []
        for j in range(nsub - 1):
            bend = b_row(SUB * j + SUB - 1)
            lrows = [zero if i <= j else qs[i] * jnp.exp2(bs[i] - bend) for i in range(nsub)]
            rrows = [ks[j] * jnp.exp2(bend - bs[j]) if i == j else zero for i in range(nsub)]
            lhs.append(jnp.concatenate(lrows, axis=0).astype(BF16))
            rhs.append(jnp.concatenate(rrows, axis=0).astype(BF16))
        lhs = jnp.concatenate(lhs, axis=1)
        rhs = jnp.concatenate(rhs, axis=1)
        p_off = lax.dot_general(lhs, rhs, (((1,), (1,)), ((), ())), preferred_element_type=F32)
    return qt, kt, p, p_off, blast


def _gla_apply(scores, v, s):
    qt, kt, p, p_off, blast = scores
    if p_off is not None:
        p = p + p_off
    o = _bdot(jnp.concatenate([qt, p.astype(BF16)], axis=1),
              jnp.concatenate([s.astype(BF16), v], axis=0))

    ri = lax.broadcasted_iota(jnp.int32, (GLA_HK, GLA_HK), 0)
    ci = lax.broadcasted_iota(jnp.int32, (GLA_HK, GLA_HK), 1)
    erow = jnp.broadcast_to(jnp.exp2(blast), (GLA_HK, GLA_HK))
    ecol = jnp.sum(jnp.where(ri == ci, erow, 0.0), axis=1, keepdims=True)
    s_new = s * ecol + lax.dot_general(kt, v, (((0,), (0,)), ((), ())), preferred_element_type=F32)
    return o, s_new


def _gla_head_scores(blk, h, q_s, k_s, b_s):
    ksl = slice(h * GLA_HK, (h + 1) * GLA_HK)
    k_row = lambda r: k_s[blk, r:r + 1, ksl]
    b_row = lambda r: b_s[blk, r:r + 1, ksl]
    return _gla_scores(q_s[blk, :, ksl], k_s[blk, :, ksl], b_s[blk, :, ksl], k_row, b_row,
                       q_s.shape[1] // SUB)


def _gla_head_apply(blk, h, scores, v_s, sg_s, gng, bo_s, get_state, put_state):
    vsl = slice(h * GLA_HV, (h + 1) * GLA_HV)
    o, s_new = _gla_apply(scores, v_s[blk, :, vsl].astype(BF16), get_state(h))
    put_state(h, s_new)
    o = _rms(o) * gng
    bo_s[blk, :, vsl] = (o * sg_s[blk, :, vsl]).astype(BF16)


def _pool_delta(u_b, s2_b, s4_b, s8_b, n, pos0):
    r = HIST_BASE + n
    gd = POOL_GD
    s2_b[16:r, :] = u_b[16:r, :] + u_b[15:r - 1, :]
    s4_b[16:r, :] = s2_b[16:r, gd:] + s2_b[14:r - 2, gd:]
    s8_b[24:r, :] = s4_b[24:r, gd:] + s4_b[20:r - 4, gd:]
    s16 = s8_b[HIST_BASE:r, gd:] + s8_b[HIST_BASE - 8:r - 8, gd:]
    sums = (s2_b[HIST_BASE:r, 0:gd], s4_b[HIST_BASE:r, 0:gd], s8_b[HIST_BASE:r, 0:gd], s16)
    pos1 = pos0 + lax.broadcasted_iota(jnp.int32, (n, 1), 0) + 1
    out = []
    for gi, w in enumerate(POOL_WINDOWS):
        cnt = jnp.minimum(pos1, w).astype(F32)
        out.append(sums[gi] / cnt - u_b[HIST_BASE:r, gi * POOL_GD:(gi + 1) * POOL_GD])
    return out


def _pool_mix(d, wpool, pscale):
    mixed = [_bdot(d[gi].astype(BF16), wpool[gi]) for gi in range(len(POOL_WINDOWS))]
    return (jnp.concatenate(mixed, axis=1) * pscale).astype(BF16)


def _project_gla(x, mod_ref, g1, wq, wk, wv, wg, walr, walpha, balpha, period,
                 h_s, q_s, k_s, v_s, sg_s, b_s):
    h = _norm_mod(x, g1[...], mod_ref[0], mod_ref[1])
    h_s[...] = h
    _store_blocks(q_s, _proj(h, wq) * (GLA_HK ** -0.5))
    _store_blocks(k_s, _proj(h, wk))
    _store_blocks(v_s, _proj(h, wv))
    g = _proj(h, wg)
    _store_blocks(sg_s, g * _sigmoid(g))
    alr = _proj(h, walr)
    z = _bdot(alr.astype(BF16), walpha[...]) + balpha[...]
    a = _log_sigmoid(z) * (LOG2_E / GLA_GATE_NORM)
    b = [_cumsum_rows(a[r:r + SLAB], period) for r in range(0, a.shape[0], SLAB)]
    _store_blocks(b_s, jnp.concatenate(b, axis=0))


def _gated_pool(h_s, aout, wga, wpa):
    return _sigmoid(_proj(h_s[...], wga)) * _bdot(aout, wpa[...])


def _gate_b(h_s, wgb):
    return _sigmoid(_proj(h_s[...], wgb))


def _merge_out(x, gate1, merged_a, gate_b, bo, wpb, wout):
    merged = merged_a + gate_b * _bdot(bo, wpb[...])
    y = _bdot(merged.astype(BF16), wout[...])
    return x + _rows_per_seq(gate1, x.shape[0]) * y


def _ffn_tile(x, shift, scale, gate, g2, w1, w2, gf, o_ref):
    h2 = _norm_mod(x, g2[...], shift, scale)
    acc = jnp.zeros(x.shape, F32)
    for c in range(D_FF // FF_CHUNK):
        sl = slice(c * FF_CHUNK, (c + 1) * FF_CHUNK)
        y = jnp.maximum(_bdot(h2, w1[:, sl]), 0.0)
        acc = acc + _bdot((y * y).astype(BF16), w2[sl, :])
    x2 = x + _rows_per_seq(gate, x.shape[0]) * acc
    o_ref[...] = _rms(x2) * gf[...]


def _adaln_kernel(c_ref, w_ref, b_ref, o_ref):
    c = c_ref[...]
    sc = (c * _sigmoid(c)).astype(BF16)
    o_ref[...] = _bdot(sc, w_ref[...].astype(BF16)) + b_ref[...]


def _prompt_layer_kernel(x_ref, xn_ref, mod_ref, g1, wu, wq, wk, wv, wg, walr, wga, wgb, walpha, balpha,
                         wpool, pscale, gng, wpa, wpb, wout, g2, w1, w2, gf,
                         y_ref, st_ref, hist_ref,
                         u_b, s2_b, s4_b, s8_b, hh_s, q_s, k_s, v_s, sg_s, b_s, bo_s,
                         st_s, x1_s, h2_s, y_s, acc_s):
    t = pl.program_id(0)
    n_tiles = pl.num_programs(0) - 1
    tl = x_ref.shape[0]
    n_chunks = tl // CHUNK
    n_piece = GLA_HEADS
    piece = D_MODEL // n_piece
    assert n_chunks * FF_CHUNK == w1.shape[1] and FF_CHUNK == n_piece * piece

    @pl.when(t == 0)
    def _():
        st_s[...] = jnp.zeros_like(st_s)
        u_b[0:HIST_BASE, :] = jnp.zeros((HIST_BASE, POOL_WIDTH), F32)
        s2_b[0:16, :] = jnp.zeros((16, POOL_WIDTH), F32)
        x1_s[...] = jnp.zeros_like(x1_s)
        hh_s[0] = _norm_mod(x_ref[...], g1[...], mod_ref[0], mod_ref[1])

    @pl.when(t < n_tiles)
    def _():
        slot = lax.rem(t, 2)
        alr = _proj(hh_s[slot], walr)
        _store_blocks(k_s, _proj(hh_s[slot], wk))
        h2_s[...] = _norm_mod(x1_s[...], g2[...], mod_ref[3], mod_ref[4])
        z = _bdot(alr.astype(BF16), walpha[...]) + balpha[...]
        a = _log_sigmoid(z) * (LOG2_E / GLA_GATE_NORM)
        _store_blocks(q_s, _proj(hh_s[slot], wq) * (GLA_HK ** -0.5))
        b = [_cumsum_rows(a[r:r + SLAB], CHUNK) for r in range(0, tl, SLAB)]
        _store_blocks(b_s, jnp.concatenate(b, axis=0))
        _store_blocks(v_s, _proj(hh_s[slot], wv))
        g = _proj(hh_s[slot], wg)
        _store_blocks(sg_s, g * _sigmoid(g))

        def get_state(h):
            return st_s[h]

        def put_state(h, s_new):
            st_s[h] = s_new

        acc_s[...] = jnp.zeros_like(acc_s)

        def chunk_body(c):
            def scores(h):
                return _gla_head_scores(c, h, q_s, k_s, b_s)

            def apply(h, sc):
                _gla_head_apply(c, h, sc, v_s, sg_s, gng[...], bo_s, get_state, put_state)

            def ffn_up(j):
                cols = slice(j * piece, (j + 1) * piece)
                w1_cols = slice(c * FF_CHUNK + j * piece, c * FF_CHUNK + (j + 1) * piece)
                y = jnp.maximum(_bdot(h2_s[...], w1[:, w1_cols]), 0.0)
                y_s[:, cols] = (y * y).astype(BF16)

            def ffn_down(j):
                cols = slice(j * piece, (j + 1) * piece)
                acc_s[:, cols] += _bdot(y_s[...], w2[c * FF_CHUNK:(c + 1) * FF_CHUNK, cols])

            sc0 = scores(0)
            sc1 = scores(1)
            ffn_up(0)
            sc2 = scores(2)
            sc3 = scores(3)
            ffn_up(1)
            apply(0, sc0)
            ffn_up(2)
            apply(1, sc1)
            ffn_up(3)
            apply(2, sc2)
            ffn_down(0)
            apply(3, sc3)
            ffn_down(1)
            ffn_down(2)
            ffn_down(3)

        for c in range(n_chunks):
            chunk_body(c)

        x2 = x1_s[...] + mod_ref[5] * acc_s[...]
        y_ref[...] = _rms(x2) * gf[...]

        hh_s[1 - slot] = _norm_mod(xn_ref[...], g1[...], mod_ref[0], mod_ref[1])

        u_b[HIST_BASE:HIST_BASE + tl, :] = _proj(hh_s[slot], wu)
        gate_a = _sigmoid(_proj(hh_s[slot], wga))
        gate_b = _sigmoid(_proj(hh_s[slot], wgb))
        d = _pool_delta(u_b, s2_b, s4_b, s8_b, tl, t * tl)
        aout = _pool_mix(d, wpool, pscale[...])
        u_b[16:HIST_BASE, :] = u_b[16 + tl:HIST_BASE + tl, :]
        merged = gate_a * _bdot(aout, wpa[...]) + gate_b * _bdot(_load_blocks(bo_s), wpb[...])
        x1 = x_ref[...] + mod_ref[2] * _bdot(merged.astype(BF16), wout[...])
        x1_s[...] = x1

    @pl.when(t == n_tiles - 1)
    def _():
        st_ref[...] = st_s[...]
        hist_ref[...] = u_b[16:HIST_BASE, :]

    @pl.when(t == n_tiles)
    def _():
        _ffn_tile(x1_s[...], mod_ref[3], mod_ref[4], mod_ref[5], g2, w1, w2, gf, y_ref)


def _sample_mixer_kernel(x_ref, mod_ref, s0_ref, cache_ref, g1, wu, wq, wk, wv, wg, walr, wga, wgb,
                         walpha, balpha, wpool, pscale, gng, wpa, wpb, wout,
                         x1_ref, st_ref, hist_ref,
                         u_b, s2_b, s4_b, s8_b, h_s, q_s, k_s, v_s, sg_s, b_s, bo_s, u_s, ao_s):
    s = pl.program_id(0)
    seq = q_s.shape[1]
    group = s0_ref.shape[0]

    @pl.when(s == 0)
    def _():
        u_b[0:HIST_BASE, :] = jnp.zeros((HIST_BASE, POOL_WIDTH), F32)
        s2_b[0:16, :] = jnp.zeros((16, POOL_WIDTH), F32)
        _project_gla(x_ref[...], mod_ref, g1, wq, wk, wv, wg, walr, walpha, balpha, seq,
                     h_s, q_s, k_s, v_s, sg_s, b_s)
        _store_blocks(u_s, _proj(h_s[...], wu))

    scores = [[_gla_head_scores(s * group + i, h, q_s, k_s, b_s) for h in range(GLA_HEADS)]
              for i in range(group)]

    for i in range(group):
        blk = s * group + i
        u_b[HIST_BASE - POOL_HIST:HIST_BASE, :] = cache_ref[i]
        u_b[HIST_BASE:HIST_BASE + seq, :] = u_s[blk]
        d = _pool_delta(u_b, s2_b, s4_b, s8_b, seq, PAST_LEN)
        ao_s[blk] = _pool_mix(d, wpool, pscale[...])
        hist_ref[i] = u_b[HIST_BASE + seq - POOL_HIST:HIST_BASE + seq, :]

    for i in range(group):
        def get_state(h, i=i):
            return s0_ref[i, h]

        def put_state(h, s_new, i=i):
            st_ref[i, h] = s_new

        for h in range(GLA_HEADS):
            _gla_head_apply(s * group + i, h, scores[i][h], v_s, sg_s, gng[...], bo_s,
                            get_state, put_state)

    @pl.when(s == pl.num_programs(0) - 1)
    def _():
        merged_a = _gated_pool(h_s, _load_blocks(ao_s), wga, wpa)
        x1_ref[...] = _merge_out(x_ref[...], mod_ref[2], merged_a, _gate_b(h_s, wgb),
                                 _load_blocks(bo_s), wpb, wout)


def _ffn_kernel(x_ref, mod_ref, g2, w1, w2, gf, o_ref):
    _ffn_tile(x_ref[...], mod_ref[0], mod_ref[1], mod_ref[2], g2, w1, w2, gf, o_ref)


def _const_spec(shape):
    nd = len(shape)
    return pl.BlockSpec(shape, lambda *_: (0,) * nd, pipeline_mode=pl.Buffered(1))


def _params(vmem_limit=None):
    return pltpu.CompilerParams(dimension_semantics=("arbitrary",),
                                vmem_limit_bytes=vmem_limit or VMEM_LIMIT)


def _mixer_scratch(nblk, rows, hist_rows, h_slots=None):
    h_shape = (nblk * rows, D_MODEL) if h_slots is None else (h_slots, nblk * rows, D_MODEL)
    return [
        pltpu.VMEM((hist_rows, POOL_WIDTH), F32),
        pltpu.VMEM((hist_rows, POOL_WIDTH), F32),
        pltpu.VMEM((hist_rows, POOL_WIDTH - POOL_GD), F32),
        pltpu.VMEM((hist_rows, POOL_WIDTH - 2 * POOL_GD), F32),
        pltpu.VMEM(h_shape, BF16),
        pltpu.VMEM((nblk, rows, GLA_DK), F32),
        pltpu.VMEM((nblk, rows, GLA_DK), F32),
        pltpu.VMEM((nblk, rows, GLA_DV), F32),
        pltpu.VMEM((nblk, rows, GLA_DV), F32),
        pltpu.VMEM((nblk, rows, GLA_DK), F32),
        pltpu.VMEM((nblk, rows, GLA_DV), BF16),
    ]


def _adaln(c_all, w_ada, b_ada):
    rows = c_all.shape[0]
    n = w_ada.shape[1]
    return pl.pallas_call(
        _adaln_kernel,
        grid=(n // ADA_BLOCK,),
        in_specs=[pl.BlockSpec((rows, D_MODEL), lambda j: (0, 0)),
                  pl.BlockSpec((D_MODEL, ADA_BLOCK), lambda j: (0, j)),
                  pl.BlockSpec((1, ADA_BLOCK), lambda j: (0, j))],
        out_specs=pl.BlockSpec((rows, ADA_BLOCK), lambda j: (0, j)),
        out_shape=jax.ShapeDtypeStruct((rows, n), F32),
        compiler_params=_params(),
        name="adaln_mod",
    )(c_all, w_ada, b_ada)


def _prompt_layer(x, mod, weights, ffn_weights):
    n_tok = x.shape[0]
    tl = TL_MIX
    n_tiles = n_tok // tl
    consts = (mod,) + tuple(weights) + tuple(ffn_weights)
    return pl.pallas_call(
        _prompt_layer_kernel,
        grid=(n_tiles + 1,),
        in_specs=[pl.BlockSpec((tl, D_MODEL), lambda t: (jnp.minimum(t, n_tiles - 1), 0)),
                  pl.BlockSpec((tl, D_MODEL), lambda t: (jnp.minimum(t + 1, n_tiles - 1), 0))]
        + [_const_spec(w.shape) for w in consts],
        out_specs=[pl.BlockSpec((tl, D_MODEL), lambda t: (jnp.maximum(t - 1, 0), 0)),
                   pl.BlockSpec((GLA_HEADS, GLA_HK, GLA_HV), lambda t: (0, 0, 0)),
                   pl.BlockSpec((16, POOL_WIDTH), lambda t: (0, 0))],
        out_shape=[jax.ShapeDtypeStruct((n_tok, D_MODEL), F32),
                   jax.ShapeDtypeStruct((GLA_HEADS, GLA_HK, GLA_HV), F32),
                   jax.ShapeDtypeStruct((16, POOL_WIDTH), F32)],
        scratch_shapes=_mixer_scratch(tl // CHUNK, CHUNK, HIST_BASE + tl, h_slots=2) + [
            pltpu.VMEM((GLA_HEADS, GLA_HK, GLA_HV), F32),
            pltpu.VMEM((tl, D_MODEL), F32),
            pltpu.VMEM((tl, D_MODEL), BF16),
            pltpu.VMEM((tl, FF_CHUNK), BF16),
            pltpu.VMEM((tl, D_MODEL), F32),
        ],
        compiler_params=_params(VMEM_LIMIT_LAYER),
        name="prompt_layer",
    )(x, x, *consts)


def _sample_mixer(x, mod, s0, cache, weights):
    n_tok = x.shape[0]
    n_seq = s0.shape[0]
    seq = n_tok // n_seq
    group = SEQ_GROUP
    w_specs = [_const_spec(w.shape) for w in weights]
    st_spec = pl.BlockSpec((group, GLA_HEADS, GLA_HK, GLA_HV), lambda s: (s, 0, 0, 0))
    hist_spec = pl.BlockSpec((group, POOL_HIST, POOL_WIDTH), lambda s: (s, 0, 0))
    return pl.pallas_call(
        _sample_mixer_kernel,
        grid=(n_seq // group,),
        in_specs=[_const_spec(x.shape), _const_spec(mod.shape), st_spec, hist_spec] + w_specs,
        out_specs=[pl.BlockSpec((n_tok, D_MODEL), lambda s: (0, 0)), st_spec, hist_spec],
        out_shape=[jax.ShapeDtypeStruct((n_tok, D_MODEL), F32),
                   jax.ShapeDtypeStruct(s0.shape, F32),
                   jax.ShapeDtypeStruct(cache.shape, F32)],
        scratch_shapes=_mixer_scratch(n_seq, seq, HIST_BASE + seq) + [
            pltpu.VMEM((n_seq, seq, POOL_WIDTH), F32),
            pltpu.VMEM((n_seq, seq, POOL_WIDTH), BF16),
        ],
        compiler_params=_params(),
        name="sample_mixer",
    )(x, mod, s0, cache, *weights)


def _ffn_final(x, mod, g2, w1, w2, gf):
    n_tok = x.shape[0]
    tl = min(TL_FFN, n_tok)
    assert mod.shape[1] == 1 or n_tok == tl
    return pl.pallas_call(
        _ffn_kernel,
        grid=(n_tok // tl,),
        in_specs=[pl.BlockSpec((tl, D_MODEL), lambda t: (t, 0)), _const_spec(mod.shape),
                  _const_spec(g2.shape), _const_spec(w1.shape), _const_spec(w2.shape),
                  _const_spec(gf.shape)],
        out_specs=pl.BlockSpec((tl, D_MODEL), lambda t: (t, 0)),
        out_shape=jax.ShapeDtypeStruct((n_tok, D_MODEL), F32),
        compiler_params=_params(),
        name="ffn_final",
    )(x, mod, g2, w1, w2, gf)


def kernel(x_prompt, x_sample, c_prompt, c_sample, state_gla, cache_pool, w_ada, b_ada, norm1_g,
           w_in, w_alpha, b_alpha, w_pool, pool_scale, gla_norm_g, w_pa, w_pb, w_out, norm2_g,
           w_ff1, w_ff2, final_g):
    n_batch, n_seq_p, _ = x_prompt.shape
    n_dec, n_seq_s, _ = x_sample.shape
    assert n_batch == 1 and w_ada.shape[0] == 1
    assert n_seq_p % TL_MIX == 0 and SLAB % n_seq_s == 0 and n_seq_s % SUB == 0
    assert n_dec % SEQ_GROUP == 0

    n_c = n_batch + n_dec
    pad = (-n_c) % 8
    c_all = jnp.concatenate([c_prompt, c_sample, jnp.zeros((pad, D_MODEL), F32)], axis=0)
    mod = _adaln(c_all, w_ada[0], b_ada)
    mod = mod.reshape(n_c + pad, 6, D_MODEL).transpose(1, 0, 2)
    mod_p = mod[:, 0:1]
    mod_s = mod[:, n_batch:n_c]

    offs = [0]
    for sz in IN_SIZES:
        offs.append(offs[-1] + sz)
    w_in_t = jnp.swapaxes(w_in[0], 0, 1).astype(BF16)
    wu, wq, wk, wv, wg, walr, wga, wgb = [w_in_t[offs[i]:offs[i + 1]] for i in range(8)]
    walr = jnp.pad(walr, ((0, LANES - GLA_LOWRANK), (0, 0)))
    walpha = jnp.pad(w_alpha[0].astype(BF16), ((0, LANES - GLA_LOWRANK), (0, 0)))
    weights = (norm1_g, wu, wq, wk, wv, wg, walr, wga, wgb,
               walpha, b_alpha, w_pool[0].astype(BF16), pool_scale, gla_norm_g,
               w_pa[0].astype(BF16), w_pb[0].astype(BF16), w_out[0].astype(BF16))

    w1 = w_ff1[0].astype(BF16)
    w2 = w_ff2[0].astype(BF16)
    gf = final_g.reshape(1, D_MODEL)

    y_p, st_p, hist_p = _prompt_layer(x_prompt[0], mod_p, weights, (norm2_g, w1, w2, gf))
    x1_s, st_s, hist_s = _sample_mixer(x_sample.reshape(n_dec * n_seq_s, D_MODEL), mod_s[0:3],
                                       state_gla[0], cache_pool[0], weights)
    y_s = _ffn_final(x1_s, mod_s[3:6], norm2_g, w1, w2, gf)

    return (y_p[None], y_s.reshape(n_dec, n_seq_s, D_MODEL), st_p[None, None],
            hist_p[None, None, 1:], st_s[None], hist_s[None])
```

```python
import jax
import jax.numpy as jnp
from jax import lax
from jax.experimental import pallas as pl
from jax.experimental.pallas import tpu as pltpu

D_MODEL = 1024
PAST_LEN = 4096
POOL_WIDTH = 512
POOL_WINDOWS = (2, 4, 8, 16)
POOL_GD = 128
POOL_HIST = 15
GLA_HEADS = 4
GLA_DK = 512
GLA_DV = 1024
GLA_HK = 128
GLA_HV = 256
GLA_LOWRANK = 16
GLA_GATE_NORM = 16.0
D_FF = 4096
EPS = 1e-6
LOG2_E = 1.4426950408889634
IN_SIZES = (POOL_WIDTH, GLA_DK, GLA_DK, GLA_DV, GLA_DV, GLA_LOWRANK, D_MODEL, D_MODEL)
MIX_SIZES = (POOL_WIDTH, GLA_DV, D_MODEL)

SUB = 8
CHUNK = 64
SLAB = 64
HIST_BASE = 32
TL_MIX = 256
TL_FFN = 512
FF_CHUNK = 1024
ADA_BLOCK = 1536
SEQ_GROUP = 4
VMEM_LIMIT = 48 * 1024 * 1024
VMEM_LIMIT_LAYER = 60 * 1024 * 1024

F32 = jnp.float32
BF16 = jnp.bfloat16


def _bdot(a, b):
    return jnp.dot(a, b, preferred_element_type=F32)


def _proj(h, wt_ref):
    return lax.dot_general(h, wt_ref[...], (((1,), (1,)), ((), ())), preferred_element_type=F32)


def _split_rows(ref, sizes):
    views, lo = [], 0
    for n in sizes:
        views.append(ref.at[pl.ds(lo, n)])
        lo += n
    return views


def _split_w_in(win):
    return _split_rows(win, IN_SIZES)


def _rms(xf):
    return xf * lax.rsqrt(jnp.mean(xf * xf, axis=-1, keepdims=True) + EPS)


def _sigmoid(x):
    return 0.5 * jnp.tanh(0.5 * x) + 0.5


def _log_sigmoid(z):
    return jnp.minimum(z, 0.0) - jnp.log(1.0 + jnp.exp(-jnp.abs(z)))


def _rows_per_seq(m, n_rows):
    n_seq = m.shape[0]
    if n_seq == 1:
        return m
    rep = n_rows // n_seq
    return jnp.concatenate([jnp.broadcast_to(m[i:i + 1], (rep, m.shape[1])) for i in range(n_seq)],
                           axis=0)


def _norm_mod(x, g, shift, scale):
    n = x.shape[0]
    return (_rms(x) * g * (1.0 + _rows_per_seq(scale, n)) + _rows_per_seq(shift, n)).astype(BF16)


def _store_blocks(ref, val):
    nblk, rows = ref.shape[0], ref.shape[1]
    for i in range(nblk):
        ref[i] = val[i * rows:(i + 1) * rows].astype(ref.dtype)


def _load_blocks(ref):
    return jnp.concatenate([ref[i] for i in range(ref.shape[0])], axis=0)


def _cumsum_rows(a, period):
    n = a.shape[0]
    ri = lax.broadcasted_iota(jnp.int32, (n, n), 0)
    ci = lax.broadcasted_iota(jnp.int32, (n, n), 1)
    shift = period.bit_length() - 1
    same = jnp.right_shift(ri, shift) == jnp.right_shift(ci, shift)
    tri = jnp.where((ci <= ri) & same, 1.0, 0.0).astype(BF16)
    hi = a.astype(BF16)
    r1 = a - hi.astype(F32)
    mid = r1.astype(BF16)
    lo = (r1 - mid.astype(F32)).astype(BF16)
    return _bdot(tri, hi) + _bdot(tri, mid) + _bdot(tri, lo)


def _gla_scores(q, k, b, k_row, b_row, nsub):
    L = SUB * nsub
    blast = b_row(L - 1)
    qt = (q * jnp.exp2(b)).astype(BF16)
    kt = (k * jnp.exp2(blast - b)).astype(BF16)

    lane = lax.broadcasted_iota(jnp.int32, (SUB, L), 1)
    row = lax.broadcasted_iota(jnp.int32, (SUB, L), 0)
    qs = [q[SUB * i:SUB * (i + 1)] for i in range(nsub)]
    ks = [k[SUB * i:SUB * (i + 1)] for i in range(nsub)]
    bs = [b[SUB * i:SUB * (i + 1)] for i in range(nsub)]

    diag = []
    for i in range(nsub):
        acc = jnp.zeros((SUB, L), F32)
        for j in range(SUB):
            r = SUB * i + j
            dec = jnp.exp2(bs[i] - b_row(r))
            col = jnp.sum(qs[i] * dec * k_row(r), axis=1, keepdims=True)
            acc = jnp.where(lane == r, col, acc)
        diag.append(jnp.where(lane <= row + SUB * i, acc, 0.0))
    p = diag[0] if nsub == 1 else jnp.concatenate(diag, axis=0)

    p_off = None
    if nsub > 1:
        zero = jnp.zeros((SUB, GLA_HK), F32)
        lhs, rhs = [], []
        for j in range(nsub - 1):
            bend = b_row(SUB * j + SUB - 1)
            lrows = [zero if i <= j else qs[i] * jnp.exp2(bs[i] - bend) for i in range(nsub)]
            rrows = [ks[j] * jnp.exp2(bend - bs[j]) if i == j else zero for i in range(nsub)]
            lhs.append(jnp.concatenate(lrows, axis=0).astype(BF16))
            rhs.append(jnp.concatenate(rrows, axis=0).astype(BF16))
        lhs = jnp.concatenate(lhs, axis=1)
        rhs = jnp.concatenate(rhs, axis=1)
        p_off = lax.dot_general(lhs, rhs, (((1,), (1,)), ((), ())), preferred_element_type=F32)
    return qt, kt, p, p_off, blast


def _gla_apply(scores, v, s):
    qt, kt, p, p_off, blast = scores
    if p_off is not None:
        p = p + p_off
    o = _bdot(jnp.concatenate([qt, p.astype(BF16)], axis=1),
              jnp.concatenate([s.astype(BF16), v], axis=0))

    ri = lax.broadcasted_iota(jnp.int32, (GLA_HK, GLA_HK), 0)
    ci = lax.broadcasted_iota(jnp.int32, (GLA_HK, GLA_HK), 1)
    erow = jnp.broadcast_to(jnp.exp2(blast), (GLA_HK, GLA_HK))
    ecol = jnp.sum(jnp.where(ri == ci, erow, 0.0), axis=1, keepdims=True)
    s_new = s * ecol + lax.dot_general(kt, v, (((0,), (0,)), ((), ())), preferred_element_type=F32)
    return o, s_new


def _gla_head_scores(blk, h, q_s, k_s, b_s):
    ksl = slice(h * GLA_HK, (h + 1) * GLA_HK)
    k_row = lambda r: k_s[blk, r:r + 1, ksl]
    b_row = lambda r: b_s[blk, r:r + 1, ksl]
    return _gla_scores(q_s[blk, :, ksl], k_s[blk, :, ksl], b_s[blk, :, ksl], k_row, b_row,
                       q_s.shape[1] // SUB)


def _gla_head_apply(blk, h, scores, v_s, sg_s, gng, bo_s, get_state, put_state):
    vsl = slice(h * GLA_HV, (h + 1) * GLA_HV)
    o, s_new = _gla_apply(scores, v_s[blk, :, vsl].astype(BF16), get_state(h))
    put_state(h, s_new)
    o = _rms(o) * gng
    bo_s[blk, :, vsl] = (o * sg_s[blk, :, vsl]).astype(BF16)


def _pool_delta(u_b, s2_b, s4_b, s8_b, n, pos0):
    r = HIST_BASE + n
    gd = POOL_GD
    s2_b[16:r, :] = u_b[16:r, :] + u_b[15:r - 1, :]
    s4_b[16:r, :] = s2_b[16:r, gd:] + s2_b[14:r - 2, gd:]
    s8_b[24:r, :] = s4_b[24:r, gd:] + s4_b[20:r - 4, gd:]
    s16 = s8_b[HIST_BASE:r, gd:] + s8_b[HIST_BASE - 8:r - 8, gd:]
    sums = (s2_b[HIST_BASE:r, 0:gd], s4_b[HIST_BASE:r, 0:gd], s8_b[HIST_BASE:r, 0:gd], s16)
    pos1 = pos0 + lax.broadcasted_iota(jnp.int32, (n, 1), 0) + 1
    out = []
    for gi, w in enumerate(POOL_WINDOWS):
        cnt = jnp.minimum(pos1, w).astype(F32)
        out.append(sums[gi] / cnt - u_b[HIST_BASE:r, gi * POOL_GD:(gi + 1) * POOL_GD])
    return out


def _pool_mix(d, wpool, pscale):
    mixed = [_bdot(d[gi].astype(BF16), wpool[gi]) for gi in range(len(POOL_WINDOWS))]
    return (jnp.concatenate(mixed, axis=1) * pscale).astype(BF16)


def _project_gla(x, mod_ref, g1, wq, wk, wv, wg, walr, walpha, balpha, period,
                 h_s, q_s, k_s, v_s, sg_s, b_s):
    h = _norm_mod(x, g1[...], mod_ref[0], mod_ref[1])
    h_s[...] = h
    _store_blocks(q_s, _proj(h, wq) * (GLA_HK ** -0.5))
    _store_blocks(k_s, _proj(h, wk))
    _store_blocks(v_s, _proj(h, wv))
    g = _proj(h, wg)
    _store_blocks(sg_s, g * _sigmoid(g))
    alr = _proj(h, walr)
    z = _bdot(alr.astype(BF16), walpha[...]) + balpha[...]
    a = _log_sigmoid(z) * (LOG2_E / GLA_GATE_NORM)
    b = [_cumsum_rows(a[r:r + SLAB], period) for r in range(0, a.shape[0], SLAB)]
    _store_blocks(b_s, jnp.concatenate(b, axis=0))


def _gated_pool(h_s, aout, wga, wpa):
    return _sigmoid(_proj(h_s[...], wga)) * _bdot(aout, wpa[...])


def _gate_b(h_s, wgb):
    return _sigmoid(_proj(h_s[...], wgb))


def _merge_out(x, gate1, merged_a, gate_b, bo, wpb, wout):
    merged = merged_a + gate_b * _bdot(bo, wpb[...])
    y = _bdot(merged.astype(BF16), wout[...])
    return x + _rows_per_seq(gate1, x.shape[0]) * y


def _ffn_tile(x, shift, scale, gate, g2, w1, w2, gf, o_ref):
    h2 = _norm_mod(x, g2[...], shift, scale)
    acc = jnp.zeros(x.shape, F32)
    for c in range(D_FF // FF_CHUNK):
        sl = slice(c * FF_CHUNK, (c + 1) * FF_CHUNK)
        y = jnp.maximum(_bdot(h2, w1[:, sl]), 0.0)
        acc = acc + _bdot((y * y).astype(BF16), w2[sl, :])
    x2 = x + _rows_per_seq(gate, x.shape[0]) * acc
    o_ref[...] = _rms(x2) * gf[...]


def _adaln_kernel(c_ref, w_ref, b_ref, o_ref):
    c = c_ref[...]
    sc = (c * _sigmoid(c)).astype(BF16)
    o_ref[...] = _bdot(sc, w_ref[...].astype(BF16)) + b_ref[...]


def _prompt_layer_kernel(x_ref, xn_ref, mod_ref, g1, win, walpha, balpha,
                         wpool, pscale, gng, wmix, g2, w1, w2, gf,
                         y_ref, st_ref, hist_ref,
                         u_b, s2_b, s4_b, s8_b, hh_s, q_s, k_s, v_s, sg_s, b_s, bo_s,
                         st_s, x1_s, h2_s, y_s, acc_s):
    wu, wq, wk, wv, wg, walr, wga, wgb = _split_w_in(win)
    wpa, wpb, wout = _split_rows(wmix, MIX_SIZES)
    t = pl.program_id(0)
    n_tiles = pl.num_programs(0) - 1
    tl = x_ref.shape[0]
    n_chunks = tl // CHUNK
    n_piece = GLA_HEADS
    piece = D_MODEL // n_piece
    assert n_chunks * FF_CHUNK == w1.shape[1] and FF_CHUNK == n_piece * piece

    @pl.when(t == 0)
    def _():
        st_s[...] = jnp.zeros_like(st_s)
        u_b[0:HIST_BASE, :] = jnp.zeros((HIST_BASE, POOL_WIDTH), F32)
        s2_b[0:16, :] = jnp.zeros((16, POOL_WIDTH), F32)
        x1_s[...] = jnp.zeros_like(x1_s)
        hh_s[0] = _norm_mod(x_ref[...], g1[...], mod_ref[0], mod_ref[1])

    @pl.when(t < n_tiles)
    def _():
        slot = lax.rem(t, 2)
        alr = _proj(hh_s[slot], walr)
        _store_blocks(k_s, _proj(hh_s[slot], wk))
        h2_s[...] = _norm_mod(x1_s[...], g2[...], mod_ref[3], mod_ref[4])
        z = _bdot(alr.astype(BF16), walpha[...]) + balpha[...]
        a = _log_sigmoid(z) * (LOG2_E / GLA_GATE_NORM)
        _store_blocks(q_s, _proj(hh_s[slot], wq) * (GLA_HK ** -0.5))
        b = [_cumsum_rows(a[r:r + SLAB], CHUNK) for r in range(0, tl, SLAB)]
        _store_blocks(b_s, jnp.concatenate(b, axis=0))
        _store_blocks(v_s, _proj(hh_s[slot], wv))
        g = _proj(hh_s[slot], wg)
        _store_blocks(sg_s, g * _sigmoid(g))

        def get_state(h):
            return st_s[h]

        def put_state(h, s_new):
            st_s[h] = s_new

        acc_s[...] = jnp.zeros_like(acc_s)

        def chunk_body(c):
            def scores(h):
                return _gla_head_scores(c, h, q_s, k_s, b_s)

            def apply(h, sc):
                _gla_head_apply(c, h, sc, v_s, sg_s, gng[...], bo_s, get_state, put_state)

            def ffn_up(j):
                cols = slice(j * piece, (j + 1) * piece)
                w1_cols = slice(c * FF_CHUNK + j * piece, c * FF_CHUNK + (j + 1) * piece)
                y = jnp.maximum(_bdot(h2_s[...], w1[:, w1_cols]), 0.0)
                y_s[:, cols] = (y * y).astype(BF16)

            def ffn_down(j):
                cols = slice(j * piece, (j + 1) * piece)
                acc_s[:, cols] += _bdot(y_s[...], w2[c * FF_CHUNK:(c + 1) * FF_CHUNK, cols])

            sc0 = scores(0)
            sc1 = scores(1)
            ffn_up(0)
            sc2 = scores(2)
            sc3 = scores(3)
            ffn_up(1)
            apply(0, sc0)
            ffn_up(2)
            apply(1, sc1)
            ffn_up(3)
            apply(2, sc2)
            ffn_down(0)
            apply(3, sc3)
            ffn_down(1)
            ffn_down(2)
            ffn_down(3)

        for c in range(n_chunks):
            chunk_body(c)

        x2 = x1_s[...] + mod_ref[5] * acc_s[...]
        y_ref[...] = _rms(x2) * gf[...]

        hh_s[1 - slot] = _norm_mod(xn_ref[...], g1[...], mod_ref[0], mod_ref[1])

        u_b[HIST_BASE:HIST_BASE + tl, :] = _proj(hh_s[slot], wu)
        gate_a = _sigmoid(_proj(hh_s[slot], wga))
        gate_b = _sigmoid(_proj(hh_s[slot], wgb))
        d = _pool_delta(u_b, s2_b, s4_b, s8_b, tl, t * tl)
        aout = _pool_mix(d, wpool, pscale[...])
        u_b[16:HIST_BASE, :] = u_b[16 + tl:HIST_BASE + tl, :]
        merged = gate_a * _bdot(aout, wpa[...]) + gate_b * _bdot(_load_blocks(bo_s), wpb[...])
        x1 = x_ref[...] + mod_ref[2] * _bdot(merged.astype(BF16), wout[...])
        x1_s[...] = x1

    @pl.when(t == n_tiles - 1)
    def _():
        st_ref[...] = st_s[...]
        hist_ref[...] = u_b[16:HIST_BASE, :]

    @pl.when(t == n_tiles)
    def _():
        _ffn_tile(x1_s[...], mod_ref[3], mod_ref[4], mod_ref[5], g2, w1, w2, gf, y_ref)


def _sample_mixer_kernel(x_ref, mod_ref, s0_ref, cache_ref, g1, win,
                         walpha, balpha, wpool, pscale, gng, wmix,
                         x1_ref, st_ref, hist_ref,
                         u_b, s2_b, s4_b, s8_b, h_s, q_s, k_s, v_s, sg_s, b_s, bo_s, u_s, ao_s):
    wu, wq, wk, wv, wg, walr, wga, wgb = _split_w_in(win)
    wpa, wpb, wout = _split_rows(wmix, MIX_SIZES)
    s = pl.program_id(0)
    seq = q_s.shape[1]
    group = s0_ref.shape[0]

    @pl.when(s == 0)
    def _():
        u_b[0:HIST_BASE, :] = jnp.zeros((HIST_BASE, POOL_WIDTH), F32)
        s2_b[0:16, :] = jnp.zeros((16, POOL_WIDTH), F32)
        _project_gla(x_ref[...], mod_ref, g1, wq, wk, wv, wg, walr, walpha, balpha, seq,
                     h_s, q_s, k_s, v_s, sg_s, b_s)
        _store_blocks(u_s, _proj(h_s[...], wu))

    scores = [[_gla_head_scores(s * group + i, h, q_s, k_s, b_s) for h in range(GLA_HEADS)]
              for i in range(group)]

    for i in range(group):
        blk = s * group + i
        u_b[HIST_BASE - POOL_HIST:HIST_BASE, :] = cache_ref[i]
        u_b[HIST_BASE:HIST_BASE + seq, :] = u_s[blk]
        d = _pool_delta(u_b, s2_b, s4_b, s8_b, seq, PAST_LEN)
        ao_s[blk] = _pool_mix(d, wpool, pscale[...])
        hist_ref[i] = u_b[HIST_BASE + seq - POOL_HIST:HIST_BASE + seq, :]

    for i in range(group):
        def get_state(h, i=i):
            return s0_ref[i, h]

        def put_state(h, s_new, i=i):
            st_ref[i, h] = s_new

        for h in range(GLA_HEADS):
            _gla_head_apply(s * group + i, h, scores[i][h], v_s, sg_s, gng[...], bo_s,
                            get_state, put_state)

    @pl.when(s == pl.num_programs(0) - 1)
    def _():
        merged_a = _gated_pool(h_s, _load_blocks(ao_s), wga, wpa)
        x1_ref[...] = _merge_out(x_ref[...], mod_ref[2], merged_a, _gate_b(h_s, wgb),
                                 _load_blocks(bo_s), wpb, wout)


def _ffn_kernel(x_ref, mod_ref, g2, w1, w2, gf, o_ref):
    _ffn_tile(x_ref[...], mod_ref[0], mod_ref[1], mod_ref[2], g2, w1, w2, gf, o_ref)


def _const_spec(shape):
    nd = len(shape)
    return pl.BlockSpec(shape, lambda *_: (0,) * nd, pipeline_mode=pl.Buffered(1))


def _params(vmem_limit=None):
    return pltpu.CompilerParams(dimension_semantics=("arbitrary",),
                                vmem_limit_bytes=vmem_limit or VMEM_LIMIT)


def _mixer_scratch(nblk, rows, hist_rows, h_slots=None):
    h_shape = (nblk * rows, D_MODEL) if h_slots is None else (h_slots, nblk * rows, D_MODEL)
    return [
        pltpu.VMEM((hist_rows, POOL_WIDTH), F32),
        pltpu.VMEM((hist_rows, POOL_WIDTH), F32),
        pltpu.VMEM((hist_rows, POOL_WIDTH - POOL_GD), F32),
        pltpu.VMEM((hist_rows, POOL_WIDTH - 2 * POOL_GD), F32),
        pltpu.VMEM(h_shape, BF16),
        pltpu.VMEM((nblk, rows, GLA_DK), F32),
        pltpu.VMEM((nblk, rows, GLA_DK), F32),
        pltpu.VMEM((nblk, rows, GLA_DV), F32),
        pltpu.VMEM((nblk, rows, GLA_DV), F32),
        pltpu.VMEM((nblk, rows, GLA_DK), F32),
        pltpu.VMEM((nblk, rows, GLA_DV), BF16),
    ]


def _adaln(c_all, w_ada, b_ada):
    rows = c_all.shape[0]
    n = w_ada.shape[1]
    return pl.pallas_call(
        _adaln_kernel,
        grid=(n // ADA_BLOCK,),
        in_specs=[pl.BlockSpec((rows, D_MODEL), lambda j: (0, 0)),
                  pl.BlockSpec((D_MODEL, ADA_BLOCK), lambda j: (0, j)),
                  pl.BlockSpec((1, ADA_BLOCK), lambda j: (0, j))],
        out_specs=pl.BlockSpec((rows, ADA_BLOCK), lambda j: (0, j)),
        out_shape=jax.ShapeDtypeStruct((rows, n), F32),
        compiler_params=_params(),
        name="adaln_mod",
    )(c_all, w_ada, b_ada)


def _prompt_layer(x, mod, weights, ffn_weights):
    n_tok = x.shape[0]
    tl = TL_MIX
    n_tiles = n_tok // tl
    consts = (mod,) + tuple(weights) + tuple(ffn_weights)
    return pl.pallas_call(
        _prompt_layer_kernel,
        grid=(n_tiles + 1,),
        in_specs=[pl.BlockSpec((tl, D_MODEL), lambda t: (jnp.minimum(t, n_tiles - 1), 0)),
                  pl.BlockSpec((tl, D_MODEL), lambda t: (jnp.minimum(t + 1, n_tiles - 1), 0))]
        + [_const_spec(w.shape) for w in consts],
        out_specs=[pl.BlockSpec((tl, D_MODEL), lambda t: (jnp.maximum(t - 1, 0), 0)),
                   pl.BlockSpec((GLA_HEADS, GLA_HK, GLA_HV), lambda t: (0, 0, 0)),
                   pl.BlockSpec((16, POOL_WIDTH), lambda t: (0, 0))],
        out_shape=[jax.ShapeDtypeStruct((n_tok, D_MODEL), F32),
                   jax.ShapeDtypeStruct((GLA_HEADS, GLA_HK, GLA_HV), F32),
                   jax.ShapeDtypeStruct((16, POOL_WIDTH), F32)],
        scratch_shapes=_mixer_scratch(tl // CHUNK, CHUNK, HIST_BASE + tl, h_slots=2) + [
            pltpu.VMEM((GLA_HEADS, GLA_HK, GLA_HV), F32),
            pltpu.VMEM((tl, D_MODEL), F32),
            pltpu.VMEM((tl, D_MODEL), BF16),
            pltpu.VMEM((tl, FF_CHUNK), BF16),
            pltpu.VMEM((tl, D_MODEL), F32),
        ],
        compiler_params=_params(VMEM_LIMIT_LAYER),
        name="prompt_layer",
    )(x, x, *consts)


def _sample_mixer(x, mod, s0, cache, weights):
    n_tok = x.shape[0]
    n_seq = s0.shape[0]
    seq = n_tok // n_seq
    group = SEQ_GROUP
    w_specs = [_const_spec(w.shape) for w in weights]
    st_spec = pl.BlockSpec((group, GLA_HEADS, GLA_HK, GLA_HV), lambda s: (s, 0, 0, 0))
    hist_spec = pl.BlockSpec((group, POOL_HIST, POOL_WIDTH), lambda s: (s, 0, 0))
    return pl.pallas_call(
        _sample_mixer_kernel,
        grid=(n_seq // group,),
        in_specs=[_const_spec(x.shape), _const_spec(mod.shape), st_spec, hist_spec] + w_specs,
        out_specs=[pl.BlockSpec((n_tok, D_MODEL), lambda s: (0, 0)), st_spec, hist_spec],
        out_shape=[jax.ShapeDtypeStruct((n_tok, D_MODEL), F32),
                   jax.ShapeDtypeStruct(s0.shape, F32),
                   jax.ShapeDtypeStruct(cache.shape, F32)],
        scratch_shapes=_mixer_scratch(n_seq, seq, HIST_BASE + seq) + [
            pltpu.VMEM((n_seq, seq, POOL_WIDTH), F32),
            pltpu.VMEM((n_seq, seq, POOL_WIDTH), BF16),
        ],
        compiler_params=_params(),
        name="sample_mixer",
    )(x, mod, s0, cache, *weights)


def _ffn_final(x, mod, g2, w1, w2, gf):
    n_tok = x.shape[0]
    tl = min(TL_FFN, n_tok)
    assert mod.shape[1] == 1 or n_tok == tl
    return pl.pallas_call(
        _ffn_kernel,
        grid=(n_tok // tl,),
        in_specs=[pl.BlockSpec((tl, D_MODEL), lambda t: (t, 0)), _const_spec(mod.shape),
                  _const_spec(g2.shape), _const_spec(w1.shape), _const_spec(w2.shape),
                  _const_spec(gf.shape)],
        out_specs=pl.BlockSpec((tl, D_MODEL), lambda t: (t, 0)),
        out_shape=jax.ShapeDtypeStruct((n_tok, D_MODEL), F32),
        compiler_params=_params(),
        name="ffn_final",
    )(x, mod, g2, w1, w2, gf)


def kernel(x_prompt, x_sample, c_prompt, c_sample, state_gla, cache_pool, w_ada, b_ada, norm1_g,
           w_in, w_alpha, b_alpha, w_pool, pool_scale, gla_norm_g, w_pa, w_pb, w_out, norm2_g,
           w_ff1, w_ff2, final_g):
    n_batch, n_seq_p, _ = x_prompt.shape
    n_dec, n_seq_s, _ = x_sample.shape
    assert n_batch == 1 and w_ada.shape[0] == 1
    assert n_seq_p % TL_MIX == 0 and SLAB % n_seq_s == 0 and n_seq_s % SUB == 0
    assert n_dec % SEQ_GROUP == 0

    n_c = n_batch + n_dec
    pad = (-n_c) % 8
    c_all = jnp.concatenate([c_prompt, c_sample, jnp.zeros((pad, D_MODEL), F32)], axis=0)
    mod = _adaln(c_all, w_ada[0], b_ada)
    mod = mod.reshape(n_c + pad, 6, D_MODEL).transpose(1, 0, 2)
    mod_p = mod[:, 0:1]
    mod_s = mod[:, n_batch:n_c]

    w_in_t = jnp.swapaxes(w_in[0], 0, 1).astype(BF16)
    weights = (norm1_g, w_in_t, w_alpha[0].astype(BF16), b_alpha, w_pool[0].astype(BF16), pool_scale, gla_norm_g,
               jnp.concatenate([w_pa[0], w_pb[0], w_out[0]], axis=0).astype(BF16))

    w1 = w_ff1[0].astype(BF16)
    w2 = w_ff2[0].astype(BF16)
    gf = final_g.reshape(1, D_MODEL)

    y_p, st_p, hist_p = _prompt_layer(x_prompt[0], mod_p, weights, (norm2_g, w1, w2, gf))
    x1_s, st_s, hist_s = _sample_mixer(x_sample.reshape(n_dec * n_seq_s, D_MODEL), mod_s[0:3],
                                       state_gla[0], cache_pool[0], weights)
    y_s = _ffn_final(x1_s, mod_s[3:6], norm2_g, w1, w2, gf)

    return (y_p[None], y_s.reshape(n_dec, n_seq_s, D_MODEL), st_p[None, None],
            hist_p[None, None, 1:], st_s[None], hist_s[None])
```

```python
import jax
import jax.numpy as jnp
from jax import lax
from jax.experimental import pallas as pl
from jax.experimental.pallas import tpu as pltpu

D_MODEL = 1024
PAST_LEN = 4096
POOL_WIDTH = 512
POOL_WINDOWS = (2, 4, 8, 16)
POOL_GD = 128
POOL_HIST = 15
GLA_HEADS = 4
GLA_DK = 512
GLA_DV = 1024
GLA_HK = 128
GLA_HV = 256
GLA_LOWRANK = 16
GLA_GATE_NORM = 16.0
D_FF = 4096
EPS = 1e-6
LOG2_E = 1.4426950408889634
IN_SIZES = (POOL_WIDTH, GLA_DK, GLA_DK, GLA_DV, GLA_DV, GLA_LOWRANK, D_MODEL, D_MODEL)
MIX_SIZES = (POOL_WIDTH, GLA_DV, D_MODEL)

SUB = 8
CHUNK = 64
SLAB = 64
HIST_BASE = 32
TL_MIX = 256
TL_FFN = 512
FF_CHUNK = 1024
ADA_BLOCK = 1536
SEQ_GROUP = 4
VMEM_LIMIT = 48 * 1024 * 1024
VMEM_LIMIT_LAYER = 60 * 1024 * 1024

F32 = jnp.float32
BF16 = jnp.bfloat16


def _bdot(a, b):
    return jnp.dot(a, b, preferred_element_type=F32)


def _proj(h, wt_ref):
    return lax.dot_general(h, wt_ref[...], (((1,), (1,)), ((), ())), preferred_element_type=F32)


def _split_rows(ref, sizes):
    views, lo = [], 0
    for n in sizes:
        views.append(ref.at[pl.ds(lo, n)])
        lo += n
    return views


def _split_w_in(win):
    return _split_rows(win, IN_SIZES)


def _rms(xf):
    return xf * lax.rsqrt(jnp.mean(xf * xf, axis=-1, keepdims=True) + EPS)


def _sigmoid(x):
    return 0.5 * jnp.tanh(0.5 * x) + 0.5


def _log_sigmoid(z):
    return jnp.minimum(z, 0.0) - jnp.log(1.0 + jnp.exp(-jnp.abs(z)))


def _rows_per_seq(m, n_rows):
    n_seq = m.shape[0]
    if n_seq == 1:
        return m
    rep = n_rows // n_seq
    return jnp.concatenate([jnp.broadcast_to(m[i:i + 1], (rep, m.shape[1])) for i in range(n_seq)],
                           axis=0)


def _norm_mod(x, g, shift, scale):
    n = x.shape[0]
    return (_rms(x) * g * (1.0 + _rows_per_seq(scale, n)) + _rows_per_seq(shift, n)).astype(BF16)


def _store_blocks(ref, val):
    nblk, rows = ref.shape[0], ref.shape[1]
    for i in range(nblk):
        ref[i] = val[i * rows:(i + 1) * rows].astype(ref.dtype)


def _load_blocks(ref):
    return jnp.concatenate([ref[i] for i in range(ref.shape[0])], axis=0)


def _cumsum_rows(a, period):
    n = a.shape[0]
    ri = lax.broadcasted_iota(jnp.int32, (n, n), 0)
    ci = lax.broadcasted_iota(jnp.int32, (n, n), 1)
    shift = period.bit_length() - 1
    same = jnp.right_shift(ri, shift) == jnp.right_shift(ci, shift)
    tri = jnp.where((ci <= ri) & same, 1.0, 0.0).astype(BF16)
    hi = a.astype(BF16)
    r1 = a - hi.astype(F32)
    mid = r1.astype(BF16)
    lo = (r1 - mid.astype(F32)).astype(BF16)
    return _bdot(tri, hi) + _bdot(tri, mid) + _bdot(tri, lo)


def _gla_scores(q, k, b, k_row, b_row, nsub):
    L = SUB * nsub
    blast = b_row(L - 1)
    qt = (q * jnp.exp2(b)).astype(BF16)
    kt = (k * jnp.exp2(blast - b)).astype(BF16)

    lane = lax.broadcasted_iota(jnp.int32, (SUB, L), 1)
    row = lax.broadcasted_iota(jnp.int32, (SUB, L), 0)
    qs = [q[SUB * i:SUB * (i + 1)] for i in range(nsub)]
    ks = [k[SUB * i:SUB * (i + 1)] for i in range(nsub)]
    bs = [b[SUB * i:SUB * (i + 1)] for i in range(nsub)]

    diag = []
    for i in range(nsub):
        acc = jnp.zeros((SUB, L), F32)
        for j in range(SUB):
            r = SUB * i + j
            dec = jnp.exp2(bs[i] - b_row(r))
            col = jnp.sum(qs[i] * dec * k_row(r), axis=1, keepdims=True)
            acc = jnp.where(lane == r, col, acc)
        diag.append(jnp.where(lane <= row + SUB * i, acc, 0.0))
    p = diag[0] if nsub == 1 else jnp.concatenate(diag, axis=0)

    p_off = None
    if nsub > 1:
        assert nsub % 2 == 0
        n_pair = nsub // 2
        zero = jnp.zeros((SUB, GLA_HK), F32)
        nt = (((1,), (1,)), ((), ()))
        lrows, rrows = [], []
        for m in range(n_pair):
            bmid = b_row(2 * SUB * m + SUB - 1)
            lrows += [zero, qs[2 * m + 1] * jnp.exp2(bs[2 * m + 1] - bmid)]
            rrows += [ks[2 * m] * jnp.exp2(bmid - bs[2 * m]), zero]
        p_in = lax.dot_general(jnp.concatenate(lrows, axis=0).astype(BF16),
                               jnp.concatenate(rrows, axis=0).astype(BF16), nt,
                               preferred_element_type=F32)
        shift = (2 * SUB).bit_length() - 1
        rl = lax.broadcasted_iota(jnp.int32, (L, L), 0)
        cl = lax.broadcasted_iota(jnp.int32, (L, L), 1)
        p_off = jnp.where(jnp.right_shift(rl, shift) == jnp.right_shift(cl, shift), p_in, 0.0)
        if n_pair > 1:
            lhs, rhs = [], []
            for j in range(n_pair - 1):
                bend = b_row(2 * SUB * j + 2 * SUB - 1)
                lr, rr = [], []
                for m in range(n_pair):
                    for i in (2 * m, 2 * m + 1):
                        lr.append(zero if m <= j else qs[i] * jnp.exp2(bs[i] - bend))
                        rr.append(ks[i] * jnp.exp2(bend - bs[i]) if m == j else zero)
                lhs.append(jnp.concatenate(lr, axis=0).astype(BF16))
                rhs.append(jnp.concatenate(rr, axis=0).astype(BF16))
            p_off = p_off + lax.dot_general(jnp.concatenate(lhs, axis=1),
                                            jnp.concatenate(rhs, axis=1), nt,
                                            preferred_element_type=F32)
    return qt, kt, p, p_off, blast


def _gla_apply(scores, v, s):
    qt, kt, p, p_off, blast = scores
    if p_off is not None:
        p = p + p_off
    o = _bdot(jnp.concatenate([qt, p.astype(BF16)], axis=1),
              jnp.concatenate([s.astype(BF16), v], axis=0))

    ri = lax.broadcasted_iota(jnp.int32, (GLA_HK, GLA_HK), 0)
    ci = lax.broadcasted_iota(jnp.int32, (GLA_HK, GLA_HK), 1)
    erow = jnp.broadcast_to(jnp.exp2(blast), (GLA_HK, GLA_HK))
    ecol = jnp.sum(jnp.where(ri == ci, erow, 0.0), axis=1, keepdims=True)
    s_new = s * ecol + lax.dot_general(kt, v, (((0,), (0,)), ((), ())), preferred_element_type=F32)
    return o, s_new


def _gla_head_scores(blk, h, q_s, k_s, b_s):
    ksl = slice(h * GLA_HK, (h + 1) * GLA_HK)
    k_row = lambda r: k_s[blk, r:r + 1, ksl]
    b_row = lambda r: b_s[blk, r:r + 1, ksl]
    return _gla_scores(q_s[blk, :, ksl], k_s[blk, :, ksl], b_s[blk, :, ksl], k_row, b_row,
                       q_s.shape[1] // SUB)


def _gla_head_apply(blk, h, scores, v_s, sg_s, gng, bo_s, get_state, put_state):
    vsl = slice(h * GLA_HV, (h + 1) * GLA_HV)
    o, s_new = _gla_apply(scores, v_s[blk, :, vsl].astype(BF16), get_state(h))
    put_state(h, s_new)
    o = _rms(o) * gng
    bo_s[blk, :, vsl] = (o * sg_s[blk, :, vsl]).astype(BF16)


def _pool_delta(u_b, s2_b, s4_b, s8_b, n, pos0):
    r = HIST_BASE + n
    gd = POOL_GD
    s2_b[16:r, :] = u_b[16:r, :] + u_b[15:r - 1, :]
    s4_b[16:r, :] = s2_b[16:r, gd:] + s2_b[14:r - 2, gd:]
    s8_b[24:r, :] = s4_b[24:r, gd:] + s4_b[20:r - 4, gd:]
    s16 = s8_b[HIST_BASE:r, gd:] + s8_b[HIST_BASE - 8:r - 8, gd:]
    sums = (s2_b[HIST_BASE:r, 0:gd], s4_b[HIST_BASE:r, 0:gd], s8_b[HIST_BASE:r, 0:gd], s16)
    pos1 = pos0 + lax.broadcasted_iota(jnp.int32, (n, 1), 0) + 1
    out = []
    for gi, w in enumerate(POOL_WINDOWS):
        cnt = jnp.minimum(pos1, w).astype(F32)
        out.append(sums[gi] / cnt - u_b[HIST_BASE:r, gi * POOL_GD:(gi + 1) * POOL_GD])
    return out


def _pool_mix(d, wpool, pscale):
    mixed = [_bdot(d[gi].astype(BF16), wpool[gi]) for gi in range(len(POOL_WINDOWS))]
    return (jnp.concatenate(mixed, axis=1) * pscale).astype(BF16)


def _project_gla(x, mod_ref, g1, wq, wk, wv, wg, walr, walpha, balpha, period,
                 h_s, q_s, k_s, v_s, sg_s, b_s):
    h = _norm_mod(x, g1[...], mod_ref[0], mod_ref[1])
    h_s[...] = h
    _store_blocks(q_s, _proj(h, wq) * (GLA_HK ** -0.5))
    _store_blocks(k_s, _proj(h, wk))
    _store_blocks(v_s, _proj(h, wv))
    g = _proj(h, wg)
    _store_blocks(sg_s, g * _sigmoid(g))
    alr = _proj(h, walr)
    z = _bdot(alr.astype(BF16), walpha[...]) + balpha[...]
    a = _log_sigmoid(z) * (LOG2_E / GLA_GATE_NORM)
    b = [_cumsum_rows(a[r:r + SLAB], period) for r in range(0, a.shape[0], SLAB)]
    _store_blocks(b_s, jnp.concatenate(b, axis=0))


def _gated_pool(h_s, aout, wga, wpa):
    return _sigmoid(_proj(h_s[...], wga)) * _bdot(aout, wpa[...])


def _gate_b(h_s, wgb):
    return _sigmoid(_proj(h_s[...], wgb))


def _merge_out(x, gate1, merged_a, gate_b, bo, wpb, wout):
    merged = merged_a + gate_b * _bdot(bo, wpb[...])
    y = _bdot(merged.astype(BF16), wout[...])
    return x + _rows_per_seq(gate1, x.shape[0]) * y


def _ffn_tile(x, shift, scale, gate, g2, w1, w2, gf, o_ref):
    h2 = _norm_mod(x, g2[...], shift, scale)
    acc = jnp.zeros(x.shape, F32)
    for c in range(D_FF // FF_CHUNK):
        sl = slice(c * FF_CHUNK, (c + 1) * FF_CHUNK)
        y = jnp.maximum(_bdot(h2, w1[:, sl]), 0.0)
        acc = acc + _bdot((y * y).astype(BF16), w2[sl, :])
    x2 = x + _rows_per_seq(gate, x.shape[0]) * acc
    o_ref[...] = _rms(x2) * gf[...]


def _adaln_kernel(c_ref, w_ref, b_ref, o_ref):
    c = c_ref[...]
    sc = (c * _sigmoid(c)).astype(BF16)
    o_ref[...] = _bdot(sc, w_ref[...].astype(BF16)) + b_ref[...]


def _prompt_layer_kernel(x_ref, xn_ref, mod_ref, g1, win, walpha, balpha,
                         wpool, pscale, gng, wmix, g2, w1, w2, gf,
                         y_ref, st_ref, hist_ref,
                         u_b, s2_b, s4_b, s8_b, hh_s, q_s, k_s, v_s, sg_s, b_s, bo_s,
                         st_s, x1_s, h2_s, y_s, acc_s):
    wu, wq, wk, wv, wg, walr, wga, wgb = _split_w_in(win)
    wpa, wpb, wout = _split_rows(wmix, MIX_SIZES)
    t = pl.program_id(0)
    n_tiles = pl.num_programs(0) - 1
    tl = x_ref.shape[0]
    n_chunks = tl // CHUNK
    n_piece = GLA_HEADS
    piece = D_MODEL // n_piece
    assert n_chunks * FF_CHUNK == w1.shape[1] and FF_CHUNK == n_piece * piece

    @pl.when(t == 0)
    def _():
        st_s[...] = jnp.zeros_like(st_s)
        u_b[0:HIST_BASE, :] = jnp.zeros((HIST_BASE, POOL_WIDTH), F32)
        s2_b[0:16, :] = jnp.zeros((16, POOL_WIDTH), F32)
        x1_s[...] = jnp.zeros_like(x1_s)
        hh_s[0] = _norm_mod(x_ref[...], g1[...], mod_ref[0], mod_ref[1])

    @pl.when(t < n_tiles)
    def _():
        slot = lax.rem(t, 2)
        alr = _proj(hh_s[slot], walr)
        _store_blocks(k_s, _proj(hh_s[slot], wk))
        h2_s[...] = _norm_mod(x1_s[...], g2[...], mod_ref[3], mod_ref[4])
        z = _bdot(alr.astype(BF16), walpha[...]) + balpha[...]
        a = _log_sigmoid(z) * (LOG2_E / GLA_GATE_NORM)
        _store_blocks(q_s, _proj(hh_s[slot], wq) * (GLA_HK ** -0.5))
        b = [_cumsum_rows(a[r:r + SLAB], CHUNK) for r in range(0, tl, SLAB)]
        _store_blocks(b_s, jnp.concatenate(b, axis=0))
        _store_blocks(v_s, _proj(hh_s[slot], wv))
        g = _proj(hh_s[slot], wg)
        _store_blocks(sg_s, g * _sigmoid(g))

        def get_state(h):
            return st_s[h]

        def put_state(h, s_new):
            st_s[h] = s_new

        acc_s[...] = jnp.zeros_like(acc_s)

        def chunk_body(c):
            def scores(h):
                return _gla_head_scores(c, h, q_s, k_s, b_s)

            def apply(h, sc):
                _gla_head_apply(c, h, sc, v_s, sg_s, gng[...], bo_s, get_state, put_state)

            def ffn_up(j):
                cols = slice(j * piece, (j + 1) * piece)
                w1_cols = slice(c * FF_CHUNK + j * piece, c * FF_CHUNK + (j + 1) * piece)
                y = jnp.maximum(_bdot(h2_s[...], w1[:, w1_cols]), 0.0)
                y_s[:, cols] = (y * y).astype(BF16)

            def ffn_down(j):
                cols = slice(j * piece, (j + 1) * piece)
                acc_s[:, cols] += _bdot(y_s[...], w2[c * FF_CHUNK:(c + 1) * FF_CHUNK, cols])

            sc0 = scores(0)
            sc1 = scores(1)
            ffn_up(0)
            sc2 = scores(2)
            sc3 = scores(3)
            ffn_up(1)
            apply(0, sc0)
            ffn_up(2)
            apply(1, sc1)
            ffn_up(3)
            apply(2, sc2)
            ffn_down(0)
            apply(3, sc3)
            ffn_down(1)
            ffn_down(2)
            ffn_down(3)

        for c in range(n_chunks):
            chunk_body(c)

        x2 = x1_s[...] + mod_ref[5] * acc_s[...]
        y_ref[...] = _rms(x2) * gf[...]

        hh_s[1 - slot] = _norm_mod(xn_ref[...], g1[...], mod_ref[0], mod_ref[1])

        u_b[HIST_BASE:HIST_BASE + tl, :] = _proj(hh_s[slot], wu)
        gate_a = _sigmoid(_proj(hh_s[slot], wga))
        gate_b = _sigmoid(_proj(hh_s[slot], wgb))
        d = _pool_delta(u_b, s2_b, s4_b, s8_b, tl, t * tl)
        aout = _pool_mix(d, wpool, pscale[...])
        u_b[16:HIST_BASE, :] = u_b[16 + tl:HIST_BASE + tl, :]
        merged = gate_a * _bdot(aout, wpa[...]) + gate_b * _bdot(_load_blocks(bo_s), wpb[...])
        x1 = x_ref[...] + mod_ref[2] * _bdot(merged.astype(BF16), wout[...])
        x1_s[...] = x1

    @pl.when(t == n_tiles - 1)
    def _():
        st_ref[...] = st_s[...]
        hist_ref[...] = u_b[16:HIST_BASE, :]

    @pl.when(t == n_tiles)
    def _():
        _ffn_tile(x1_s[...], mod_ref[3], mod_ref[4], mod_ref[5], g2, w1, w2, gf, y_ref)


def _sample_mixer_kernel(x_ref, mod_ref, s0_ref, cache_ref, g1, win,
                         walpha, balpha, wpool, pscale, gng, wmix,
                         x1_ref, st_ref, hist_ref,
                         u_b, s2_b, s4_b, s8_b, h_s, q_s, k_s, v_s, sg_s, b_s, bo_s, u_s, ao_s):
    wu, wq, wk, wv, wg, walr, wga, wgb = _split_w_in(win)
    wpa, wpb, wout = _split_rows(wmix, MIX_SIZES)
    s = pl.program_id(0)
    seq = q_s.shape[1]
    group = s0_ref.shape[0]

    @pl.when(s == 0)
    def _():
        u_b[0:HIST_BASE, :] = jnp.zeros((HIST_BASE, POOL_WIDTH), F32)
        s2_b[0:16, :] = jnp.zeros((16, POOL_WIDTH), F32)
        _project_gla(x_ref[...], mod_ref, g1, wq, wk, wv, wg, walr, walpha, balpha, seq,
                     h_s, q_s, k_s, v_s, sg_s, b_s)
        _store_blocks(u_s, _proj(h_s[...], wu))

    scores = [[_gla_head_scores(s * group + i, h, q_s, k_s, b_s) for h in range(GLA_HEADS)]
              for i in range(group)]

    for i in range(group):
        blk = s * group + i
        u_b[HIST_BASE - POOL_HIST:HIST_BASE, :] = cache_ref[i]
        u_b[HIST_BASE:HIST_BASE + seq, :] = u_s[blk]
        d = _pool_delta(u_b, s2_b, s4_b, s8_b, seq, PAST_LEN)
        ao_s[blk] = _pool_mix(d, wpool, pscale[...])
        hist_ref[i] = u_b[HIST_BASE + seq - POOL_HIST:HIST_BASE + seq, :]

    for i in range(group):
        def get_state(h, i=i):
            return s0_ref[i, h]

        def put_state(h, s_new, i=i):
            st_ref[i, h] = s_new

        for h in range(GLA_HEADS):
            _gla_head_apply(s * group + i, h, scores[i][h], v_s, sg_s, gng[...], bo_s,
                            get_state, put_state)

    @pl.when(s == pl.num_programs(0) - 1)
    def _():
        merged_a = _gated_pool(h_s, _load_blocks(ao_s), wga, wpa)
        x1_ref[...] = _merge_out(x_ref[...], mod_ref[2], merged_a, _gate_b(h_s, wgb),
                                 _load_blocks(bo_s), wpb, wout)


def _ffn_kernel(x_ref, mod_ref, g2, w1, w2, gf, o_ref):
    _ffn_tile(x_ref[...], mod_ref[0], mod_ref[1], mod_ref[2], g2, w1, w2, gf, o_ref)


def _const_spec(shape):
    nd = len(shape)
    return pl.BlockSpec(shape, lambda *_: (0,) * nd, pipeline_mode=pl.Buffered(1))


def _params(vmem_limit=None):
    return pltpu.CompilerParams(dimension_semantics=("arbitrary",),
                                vmem_limit_bytes=vmem_limit or VMEM_LIMIT)


def _mixer_scratch(nblk, rows, hist_rows, h_slots=None):
    h_shape = (nblk * rows, D_MODEL) if h_slots is None else (h_slots, nblk * rows, D_MODEL)
    return [
        pltpu.VMEM((hist_rows, POOL_WIDTH), F32),
        pltpu.VMEM((hist_rows, POOL_WIDTH), F32),
        pltpu.VMEM((hist_rows, POOL_WIDTH - POOL_GD), F32),
        pltpu.VMEM((hist_rows, POOL_WIDTH - 2 * POOL_GD), F32),
        pltpu.VMEM(h_shape, BF16),
        pltpu.VMEM((nblk, rows, GLA_DK), F32),
        pltpu.VMEM((nblk, rows, GLA_DK), F32),
        pltpu.VMEM((nblk, rows, GLA_DV), F32),
        pltpu.VMEM((nblk, rows, GLA_DV), F32),
        pltpu.VMEM((nblk, rows, GLA_DK), F32),
        pltpu.VMEM((nblk, rows, GLA_DV), BF16),
    ]


def _adaln(c_all, w_ada, b_ada):
    rows = c_all.shape[0]
    n = w_ada.shape[1]
    return pl.pallas_call(
        _adaln_kernel,
        grid=(n // ADA_BLOCK,),
        in_specs=[pl.BlockSpec((rows, D_MODEL), lambda j: (0, 0)),
                  pl.BlockSpec((D_MODEL, ADA_BLOCK), lambda j: (0, j)),
                  pl.BlockSpec((1, ADA_BLOCK), lambda j: (0, j))],
        out_specs=pl.BlockSpec((rows, ADA_BLOCK), lambda j: (0, j)),
        out_shape=jax.ShapeDtypeStruct((rows, n), F32),
        compiler_params=_params(),
        name="adaln_mod",
    )(c_all, w_ada, b_ada)


def _prompt_layer(x, mod, weights, ffn_weights):
    n_tok = x.shape[0]
    tl = TL_MIX
    n_tiles = n_tok // tl
    consts = (mod,) + tuple(weights) + tuple(ffn_weights)
    return pl.pallas_call(
        _prompt_layer_kernel,
        grid=(n_tiles + 1,),
        in_specs=[pl.BlockSpec((tl, D_MODEL), lambda t: (jnp.minimum(t, n_tiles - 1), 0)),
                  pl.BlockSpec((tl, D_MODEL), lambda t: (jnp.minimum(t + 1, n_tiles - 1), 0))]
        + [_const_spec(w.shape) for w in consts],
        out_specs=[pl.BlockSpec((tl, D_MODEL), lambda t: (jnp.maximum(t - 1, 0), 0)),
                   pl.BlockSpec((GLA_HEADS, GLA_HK, GLA_HV), lambda t: (0, 0, 0)),
                   pl.BlockSpec((16, POOL_WIDTH), lambda t: (0, 0))],
        out_shape=[jax.ShapeDtypeStruct((n_tok, D_MODEL), F32),
                   jax.ShapeDtypeStruct((GLA_HEADS, GLA_HK, GLA_HV), F32),
                   jax.ShapeDtypeStruct((16, POOL_WIDTH), F32)],
        scratch_shapes=_mixer_scratch(tl // CHUNK, CHUNK, HIST_BASE + tl, h_slots=2) + [
            pltpu.VMEM((GLA_HEADS, GLA_HK, GLA_HV), F32),
            pltpu.VMEM((tl, D_MODEL), F32),
            pltpu.VMEM((tl, D_MODEL), BF16),
            pltpu.VMEM((tl, FF_CHUNK), BF16),
            pltpu.VMEM((tl, D_MODEL), F32),
        ],
        compiler_params=_params(VMEM_LIMIT_LAYER),
        name="prompt_layer",
    )(x, x, *consts)


def _sample_mixer(x, mod, s0, cache, weights):
    n_tok = x.shape[0]
    n_seq = s0.shape[0]
    seq = n_tok // n_seq
    group = SEQ_GROUP
    w_specs = [_const_spec(w.shape) for w in weights]
    st_spec = pl.BlockSpec((group, GLA_HEADS, GLA_HK, GLA_HV), lambda s: (s, 0, 0, 0))
    hist_spec = pl.BlockSpec((group, POOL_HIST, POOL_WIDTH), lambda s: (s, 0, 0))
    return pl.pallas_call(
        _sample_mixer_kernel,
        grid=(n_seq // group,),
        in_specs=[_const_spec(x.shape), _const_spec(mod.shape), st_spec, hist_spec] + w_specs,
        out_specs=[pl.BlockSpec((n_tok, D_MODEL), lambda s: (0, 0)), st_spec, hist_spec],
        out_shape=[jax.ShapeDtypeStruct((n_tok, D_MODEL), F32),
                   jax.ShapeDtypeStruct(s0.shape, F32),
                   jax.ShapeDtypeStruct(cache.shape, F32)],
        scratch_shapes=_mixer_scratch(n_seq, seq, HIST_BASE + seq) + [
            pltpu.VMEM((n_seq, seq, POOL_WIDTH), F32),
            pltpu.VMEM((n_seq, seq, POOL_WIDTH), BF16),
        ],
        compiler_params=_params(),
        name="sample_mixer",
    )(x, mod, s0, cache, *weights)


def _ffn_final(x, mod, g2, w1, w2, gf):
    n_tok = x.shape[0]
    tl = min(TL_FFN, n_tok)
    assert mod.shape[1] == 1 or n_tok == tl
    return pl.pallas_call(
        _ffn_kernel,
        grid=(n_tok // tl,),
        in_specs=[pl.BlockSpec((tl, D_MODEL), lambda t: (t, 0)), _const_spec(mod.shape),
                  _const_spec(g2.shape), _const_spec(w1.shape), _const_spec(w2.shape),
                  _const_spec(gf.shape)],
        out_specs=pl.BlockSpec((tl, D_MODEL), lambda t: (t, 0)),
        out_shape=jax.ShapeDtypeStruct((n_tok, D_MODEL), F32),
        compiler_params=_params(),
        name="ffn_final",
    )(x, mod, g2, w1, w2, gf)


def kernel(x_prompt, x_sample, c_prompt, c_sample, state_gla, cache_pool, w_ada, b_ada, norm1_g,
           w_in, w_alpha, b_alpha, w_pool, pool_scale, gla_norm_g, w_pa, w_pb, w_out, norm2_g,
           w_ff1, w_ff2, final_g):
    n_batch, n_seq_p, _ = x_prompt.shape
    n_dec, n_seq_s, _ = x_sample.shape
    assert n_batch == 1 and w_ada.shape[0] == 1
    assert n_seq_p % TL_MIX == 0 and SLAB % n_seq_s == 0 and n_seq_s % SUB == 0
    assert n_dec % SEQ_GROUP == 0

    n_c = n_batch + n_dec
    pad = (-n_c) % 8
    c_all = jnp.concatenate([c_prompt, c_sample, jnp.zeros((pad, D_MODEL), F32)], axis=0)
    mod = _adaln(c_all, w_ada[0], b_ada)
    mod = mod.reshape(n_c + pad, 6, D_MODEL).transpose(1, 0, 2)
    mod_p = mod[:, 0:1]
    mod_s = mod[:, n_batch:n_c]

    w_in_t = jnp.swapaxes(w_in[0], 0, 1).astype(BF16)
    weights = (norm1_g, w_in_t, w_alpha[0].astype(BF16), b_alpha, w_pool[0].astype(BF16), pool_scale, gla_norm_g,
               jnp.concatenate([w_pa[0], w_pb[0], w_out[0]], axis=0).astype(BF16))

    w1 = w_ff1[0].astype(BF16)
    w2 = w_ff2[0].astype(BF16)
    gf = final_g.reshape(1, D_MODEL)

    y_p, st_p, hist_p = _prompt_layer(x_prompt[0], mod_p, weights, (norm2_g, w1, w2, gf))
    x1_s, st_s, hist_s = _sample_mixer(x_sample.reshape(n_dec * n_seq_s, D_MODEL), mod_s[0:3],
                                       state_gla[0], cache_pool[0], weights)
    y_s = _ffn_final(x1_s, mod_s[3:6], norm2_g, w1, w2, gf)

    return (y_p[None], y_s.reshape(n_dec, n_seq_s, D_MODEL), st_p[None, None],
            hist_p[None, None, 1:], st_s[None], hist_s[None])
```

```python
import jax
import jax.numpy as jnp
from jax import lax
from jax.experimental import pallas as pl
from jax.experimental.pallas import tpu as pltpu

D_MODEL = 1024
PAST_LEN = 4096
POOL_WIDTH = 512
POOL_WINDOWS = (2, 4, 8, 16)
POOL_GD = 128
POOL_HIST = 15
GLA_HEADS = 4
GLA_DK = 512
GLA_DV = 1024
GLA_HK = 128
GLA_HV = 256
GLA_LOWRANK = 16
GLA_GATE_NORM = 16.0
D_FF = 4096
EPS = 1e-6
LOG2_E = 1.4426950408889634
IN_SIZES = (POOL_WIDTH, GLA_DK, GLA_DK, GLA_DV, GLA_DV, GLA_LOWRANK, D_MODEL, D_MODEL)
MIX_SIZES = (POOL_WIDTH, GLA_DV, D_MODEL)

SUB = 8
CHUNK = 64
SLAB = 64
HIST_BASE = 32
TL_MIX = 256
TL_FFN = 512
FF_CHUNK = 1024
ADA_BLOCK = 1536
SEQ_GROUP = 4
VMEM_LIMIT = 48 * 1024 * 1024
VMEM_LIMIT_LAYER = 60 * 1024 * 1024

F32 = jnp.float32
BF16 = jnp.bfloat16


def _bdot(a, b):
    return jnp.dot(a, b, preferred_element_type=F32)


def _proj(h, wt_ref):
    return lax.dot_general(h, wt_ref[...], (((1,), (1,)), ((), ())), preferred_element_type=F32)


def _split_rows(ref, sizes):
    views, lo = [], 0
    for n in sizes:
        views.append(ref.at[pl.ds(lo, n)])
        lo += n
    return views


VEC_G1, VEC_G2, VEC_GF, VEC_POOL, VEC_GNG, VEC_MOD, VEC_ROWS = 0, 1, 2, 3, 4, 5, 16


def _vec_views(vecs):
    def row(r, lo=0, n=D_MODEL):
        return vecs.at[pl.ds(r, 1), pl.ds(lo, n)]
    return (row(VEC_G1), row(VEC_G2), row(VEC_GF), row(VEC_POOL, 0, GLA_DK),
            row(VEC_POOL, GLA_DK, POOL_WIDTH), row(VEC_GNG, 0, GLA_HV))


def _split_w_in(win):
    return _split_rows(win, IN_SIZES)


def _rms(xf):
    return xf * lax.rsqrt(jnp.mean(xf * xf, axis=-1, keepdims=True) + EPS)


def _sigmoid(x):
    return 0.5 * jnp.tanh(0.5 * x) + 0.5


def _log_sigmoid(z):
    return jnp.minimum(z, 0.0) - jnp.log(1.0 + jnp.exp(-jnp.abs(z)))


def _rows_per_seq(m, n_rows):
    n_seq = m.shape[0]
    if n_seq == 1:
        return m
    rep = n_rows // n_seq
    return jnp.concatenate([jnp.broadcast_to(m[i:i + 1], (rep, m.shape[1])) for i in range(n_seq)],
                           axis=0)


def _norm_mod(x, g, shift, scale):
    n = x.shape[0]
    return (_rms(x) * g * (1.0 + _rows_per_seq(scale, n)) + _rows_per_seq(shift, n)).astype(BF16)


def _store_blocks(ref, val):
    nblk, rows = ref.shape[0], ref.shape[1]
    for i in range(nblk):
        ref[i] = val[i * rows:(i + 1) * rows].astype(ref.dtype)


def _load_blocks(ref):
    return jnp.concatenate([ref[i] for i in range(ref.shape[0])], axis=0)


def _cumsum_rows(a, period):
    n = a.shape[0]
    ri = lax.broadcasted_iota(jnp.int32, (n, n), 0)
    ci = lax.broadcasted_iota(jnp.int32, (n, n), 1)
    shift = period.bit_length() - 1
    same = jnp.right_shift(ri, shift) == jnp.right_shift(ci, shift)
    tri = jnp.where((ci <= ri) & same, 1.0, 0.0).astype(BF16)
    hi = a.astype(BF16)
    r1 = a - hi.astype(F32)
    mid = r1.astype(BF16)
    lo = (r1 - mid.astype(F32)).astype(BF16)
    return _bdot(tri, hi) + _bdot(tri, mid) + _bdot(tri, lo)


def _gla_scores(q, k, b, k_row, b_row, nsub):
    L = SUB * nsub
    blast = b_row(L - 1)
    qt = (q * jnp.exp2(b)).astype(BF16)
    kt = (k * jnp.exp2(blast - b)).astype(BF16)

    lane = lax.broadcasted_iota(jnp.int32, (SUB, L), 1)
    row = lax.broadcasted_iota(jnp.int32, (SUB, L), 0)
    qs = [q[SUB * i:SUB * (i + 1)] for i in range(nsub)]
    ks = [k[SUB * i:SUB * (i + 1)] for i in range(nsub)]
    bs = [b[SUB * i:SUB * (i + 1)] for i in range(nsub)]

    diag = []
    for i in range(nsub):
        acc = jnp.zeros((SUB, L), F32)
        for j in range(SUB):
            r = SUB * i + j
            dec = jnp.exp2(bs[i] - b_row(r))
            col = jnp.sum(qs[i] * dec * k_row(r), axis=1, keepdims=True)
            acc = jnp.where(lane == r, col, acc)
        diag.append(jnp.where(lane <= row + SUB * i, acc, 0.0))
    p = diag[0] if nsub == 1 else jnp.concatenate(diag, axis=0)

    p_off = None
    if nsub > 1:
        zero = jnp.zeros((SUB, GLA_HK), F32)
        lhs, rhs = [], []
        for j in range(nsub - 1):
            bend = b_row(SUB * j + SUB - 1)
            lrows = [zero if i <= j else qs[i] * jnp.exp2(bs[i] - bend) for i in range(nsub)]
            rrows = [ks[j] * jnp.exp2(bend - bs[j]) if i == j else zero for i in range(nsub)]
            lhs.append(jnp.concatenate(lrows, axis=0).astype(BF16))
            rhs.append(jnp.concatenate(rrows, axis=0).astype(BF16))
        lhs = jnp.concatenate(lhs, axis=1)
        rhs = jnp.concatenate(rhs, axis=1)
        p_off = lax.dot_general(lhs, rhs, (((1,), (1,)), ((), ())), preferred_element_type=F32)
    return qt, kt, p, p_off, blast


def _gla_apply(scores, v, s):
    qt, kt, p, p_off, blast = scores
    if p_off is not None:
        p = p + p_off
    o = _bdot(jnp.concatenate([qt, p.astype(BF16)], axis=1),
              jnp.concatenate([s.astype(BF16), v], axis=0))

    ri = lax.broadcasted_iota(jnp.int32, (GLA_HK, GLA_HK), 0)
    ci = lax.broadcasted_iota(jnp.int32, (GLA_HK, GLA_HK), 1)
    erow = jnp.broadcast_to(jnp.exp2(blast), (GLA_HK, GLA_HK))
    ecol = jnp.sum(jnp.where(ri == ci, erow, 0.0), axis=1, keepdims=True)
    s_new = s * ecol + lax.dot_general(kt, v, (((0,), (0,)), ((), ())), preferred_element_type=F32)
    return o, s_new


def _gla_head_scores(blk, h, q_s, k_s, b_s):
    ksl = slice(h * GLA_HK, (h + 1) * GLA_HK)
    k_row = lambda r: k_s[blk, r:r + 1, ksl]
    b_row = lambda r: b_s[blk, r:r + 1, ksl]
    return _gla_scores(q_s[blk, :, ksl], k_s[blk, :, ksl], b_s[blk, :, ksl], k_row, b_row,
                       q_s.shape[1] // SUB)


def _gla_head_apply(blk, h, scores, v_s, sg_s, gng, bo_s, get_state, put_state):
    vsl = slice(h * GLA_HV, (h + 1) * GLA_HV)
    o, s_new = _gla_apply(scores, v_s[blk, :, vsl].astype(BF16), get_state(h))
    put_state(h, s_new)
    o = _rms(o) * gng
    bo_s[blk, :, vsl] = (o * sg_s[blk, :, vsl]).astype(BF16)


def _pool_delta(u_b, s2_b, s4_b, s8_b, n, pos0):
    r = HIST_BASE + n
    gd = POOL_GD
    s2_b[16:r, :] = u_b[16:r, :] + u_b[15:r - 1, :]
    s4_b[16:r, :] = s2_b[16:r, gd:] + s2_b[14:r - 2, gd:]
    s8_b[24:r, :] = s4_b[24:r, gd:] + s4_b[20:r - 4, gd:]
    s16 = s8_b[HIST_BASE:r, gd:] + s8_b[HIST_BASE - 8:r - 8, gd:]
    sums = (s2_b[HIST_BASE:r, 0:gd], s4_b[HIST_BASE:r, 0:gd], s8_b[HIST_BASE:r, 0:gd], s16)
    pos1 = pos0 + lax.broadcasted_iota(jnp.int32, (n, 1), 0) + 1
    out = []
    for gi, w in enumerate(POOL_WINDOWS):
        cnt = jnp.minimum(pos1, w).astype(F32)
        out.append(sums[gi] / cnt - u_b[HIST_BASE:r, gi * POOL_GD:(gi + 1) * POOL_GD])
    return out


def _pool_mix(d, wpool, pscale):
    mixed = [_bdot(d[gi].astype(BF16), wpool[gi]) for gi in range(len(POOL_WINDOWS))]
    return (jnp.concatenate(mixed, axis=1) * pscale).astype(BF16)


def _project_gla(x, mod_ref, g1, wq, wk, wv, wg, walr, walpha, balpha, period,
                 h_s, q_s, k_s, v_s, sg_s, b_s):
    h = _norm_mod(x, g1[...], mod_ref[0], mod_ref[1])
    h_s[...] = h
    _store_blocks(q_s, _proj(h, wq) * (GLA_HK ** -0.5))
    _store_blocks(k_s, _proj(h, wk))
    _store_blocks(v_s, _proj(h, wv))
    g = _proj(h, wg)
    _store_blocks(sg_s, g * _sigmoid(g))
    alr = _proj(h, walr)
    z = _bdot(alr.astype(BF16), walpha[...]) + balpha[...]
    a = _log_sigmoid(z) * (LOG2_E / GLA_GATE_NORM)
    b = [_cumsum_rows(a[r:r + SLAB], period) for r in range(0, a.shape[0], SLAB)]
    _store_blocks(b_s, jnp.concatenate(b, axis=0))


def _gated_pool(h_s, aout, wga, wpa):
    return _sigmoid(_proj(h_s[...], wga)) * _bdot(aout, wpa[...])


def _gate_b(h_s, wgb):
    return _sigmoid(_proj(h_s[...], wgb))


def _merge_out(x, gate1, merged_a, gate_b, bo, wpb, wout):
    merged = merged_a + gate_b * _bdot(bo, wpb[...])
    y = _bdot(merged.astype(BF16), wout[...])
    return x + _rows_per_seq(gate1, x.shape[0]) * y


def _ffn_tile(x, shift, scale, gate, g2, w1, w2, gf, o_ref):
    h2 = _norm_mod(x, g2[...], shift, scale)
    acc = jnp.zeros(x.shape, F32)
    for c in range(D_FF // FF_CHUNK):
        sl = slice(c * FF_CHUNK, (c + 1) * FF_CHUNK)
        y = jnp.maximum(_bdot(h2, w1[:, sl]), 0.0)
        acc = acc + _bdot((y * y).astype(BF16), w2[sl, :])
    x2 = x + _rows_per_seq(gate, x.shape[0]) * acc
    o_ref[...] = _rms(x2) * gf[...]


def _adaln_kernel(c_ref, w_ref, b_ref, o_ref):
    c = c_ref[...]
    sc = (c * _sigmoid(c)).astype(BF16)
    o_ref[...] = _bdot(sc, w_ref[...].astype(BF16)) + b_ref[...]


def _prompt_layer_kernel(x_ref, xn_ref, vecs, win, walpha, wpool, wmix, w1, w2,
                         y_ref, st_ref, hist_ref,
                         u_b, s2_b, s4_b, s8_b, hh_s, q_s, k_s, v_s, sg_s, b_s, bo_s,
                         st_s, x1_s, h2_s, y_s, acc_s):
    wu, wq, wk, wv, wg, walr, wga, wgb = _split_w_in(win)
    wpa, wpb, wout = _split_rows(wmix, MIX_SIZES)
    g1, g2, gf, balpha, pscale, gng = _vec_views(vecs)
    mod_ref = [vecs[VEC_MOD + i:VEC_MOD + i + 1, :] for i in range(6)]
    t = pl.program_id(0)
    n_tiles = pl.num_programs(0) - 1
    tl = x_ref.shape[0]
    n_chunks = tl // CHUNK
    n_piece = GLA_HEADS
    piece = D_MODEL // n_piece
    assert n_chunks * FF_CHUNK == w1.shape[1] and FF_CHUNK == n_piece * piece

    @pl.when(t == 0)
    def _():
        st_s[...] = jnp.zeros_like(st_s)
        u_b[0:HIST_BASE, :] = jnp.zeros((HIST_BASE, POOL_WIDTH), F32)
        s2_b[0:16, :] = jnp.zeros((16, POOL_WIDTH), F32)
        x1_s[...] = jnp.zeros_like(x1_s)
        hh_s[0] = _norm_mod(x_ref[...], g1[...], mod_ref[0], mod_ref[1])

    @pl.when(t < n_tiles)
    def _():
        slot = lax.rem(t, 2)
        alr = _proj(hh_s[slot], walr)
        _store_blocks(k_s, _proj(hh_s[slot], wk))
        h2_s[...] = _norm_mod(x1_s[...], g2[...], mod_ref[3], mod_ref[4])
        z = _bdot(alr.astype(BF16), walpha[...]) + balpha[...]
        a = _log_sigmoid(z) * (LOG2_E / GLA_GATE_NORM)
        _store_blocks(q_s, _proj(hh_s[slot], wq) * (GLA_HK ** -0.5))
        b = [_cumsum_rows(a[r:r + SLAB], CHUNK) for r in range(0, tl, SLAB)]
        _store_blocks(b_s, jnp.concatenate(b, axis=0))
        _store_blocks(v_s, _proj(hh_s[slot], wv))
        g = _proj(hh_s[slot], wg)
        _store_blocks(sg_s, g * _sigmoid(g))

        def get_state(h):
            return st_s[h]

        def put_state(h, s_new):
            st_s[h] = s_new

        acc_s[...] = jnp.zeros_like(acc_s)

        def chunk_body(c):
            def scores(h):
                return _gla_head_scores(c, h, q_s, k_s, b_s)

            def apply(h, sc):
                _gla_head_apply(c, h, sc, v_s, sg_s, gng[...], bo_s, get_state, put_state)

            def ffn_up(j):
                cols = slice(j * piece, (j + 1) * piece)
                w1_cols = slice(c * FF_CHUNK + j * piece, c * FF_CHUNK + (j + 1) * piece)
                y = jnp.maximum(_bdot(h2_s[...], w1[:, w1_cols]), 0.0)
                y_s[:, cols] = (y * y).astype(BF16)

            def ffn_down(j):
                cols = slice(j * piece, (j + 1) * piece)
                acc_s[:, cols] += _bdot(y_s[...], w2[c * FF_CHUNK:(c + 1) * FF_CHUNK, cols])

            sc0 = scores(0)
            sc1 = scores(1)
            ffn_up(0)
            sc2 = scores(2)
            sc3 = scores(3)
            ffn_up(1)
            apply(0, sc0)
            ffn_up(2)
            apply(1, sc1)
            ffn_up(3)
            apply(2, sc2)
            ffn_down(0)
            apply(3, sc3)
            ffn_down(1)
            ffn_down(2)
            ffn_down(3)

        for c in range(n_chunks):
            chunk_body(c)

        x2 = x1_s[...] + mod_ref[5] * acc_s[...]
        y_ref[...] = _rms(x2) * gf[...]

        hh_s[1 - slot] = _norm_mod(xn_ref[...], g1[...], mod_ref[0], mod_ref[1])

        u_b[HIST_BASE:HIST_BASE + tl, :] = _proj(hh_s[slot], wu)
        gate_a = _sigmoid(_proj(hh_s[slot], wga))
        gate_b = _sigmoid(_proj(hh_s[slot], wgb))
        d = _pool_delta(u_b, s2_b, s4_b, s8_b, tl, t * tl)
        aout = _pool_mix(d, wpool, pscale[...])
        u_b[16:HIST_BASE, :] = u_b[16 + tl:HIST_BASE + tl, :]
        merged = gate_a * _bdot(aout, wpa[...]) + gate_b * _bdot(_load_blocks(bo_s), wpb[...])
        x1 = x_ref[...] + mod_ref[2] * _bdot(merged.astype(BF16), wout[...])
        x1_s[...] = x1

    @pl.when(t == n_tiles - 1)
    def _():
        st_ref[...] = st_s[...]
        hist_ref[...] = u_b[16:HIST_BASE, :]

    @pl.when(t == n_tiles)
    def _():
        _ffn_tile(x1_s[...], mod_ref[3], mod_ref[4], mod_ref[5], g2, w1, w2, gf, y_ref)


def _sample_mixer_kernel(x_ref, mod_ref, s0_ref, cache_ref, vecs, win, walpha, wpool, wmix,
                         x1_ref, st_ref, hist_ref,
                         u_b, s2_b, s4_b, s8_b, h_s, q_s, k_s, v_s, sg_s, b_s, bo_s, u_s, ao_s):
    wu, wq, wk, wv, wg, walr, wga, wgb = _split_w_in(win)
    wpa, wpb, wout = _split_rows(wmix, MIX_SIZES)
    g1, _, _, balpha, pscale, gng = _vec_views(vecs)
    s = pl.program_id(0)
    seq = q_s.shape[1]
    group = s0_ref.shape[0]

    @pl.when(s == 0)
    def _():
        u_b[0:HIST_BASE, :] = jnp.zeros((HIST_BASE, POOL_WIDTH), F32)
        s2_b[0:16, :] = jnp.zeros((16, POOL_WIDTH), F32)
        _project_gla(x_ref[...], mod_ref, g1, wq, wk, wv, wg, walr, walpha, balpha, seq,
                     h_s, q_s, k_s, v_s, sg_s, b_s)
        _store_blocks(u_s, _proj(h_s[...], wu))

    scores = [[_gla_head_scores(s * group + i, h, q_s, k_s, b_s) for h in range(GLA_HEADS)]
              for i in range(group)]

    for i in range(group):
        blk = s * group + i
        u_b[HIST_BASE - POOL_HIST:HIST_BASE, :] = cache_ref[i]
        u_b[HIST_BASE:HIST_BASE + seq, :] = u_s[blk]
        d = _pool_delta(u_b, s2_b, s4_b, s8_b, seq, PAST_LEN)
        ao_s[blk] = _pool_mix(d, wpool, pscale[...])
        hist_ref[i] = u_b[HIST_BASE + seq - POOL_HIST:HIST_BASE + seq, :]

    for i in range(group):
        def get_state(h, i=i):
            return s0_ref[i, h]

        def put_state(h, s_new, i=i):
            st_ref[i, h] = s_new

        for h in range(GLA_HEADS):
            _gla_head_apply(s * group + i, h, scores[i][h], v_s, sg_s, gng[...], bo_s,
                            get_state, put_state)

    @pl.when(s == pl.num_programs(0) - 1)
    def _():
        merged_a = _gated_pool(h_s, _load_blocks(ao_s), wga, wpa)
        x1_ref[...] = _merge_out(x_ref[...], mod_ref[2], merged_a, _gate_b(h_s, wgb),
                                 _load_blocks(bo_s), wpb, wout)


def _ffn_kernel(x_ref, mod_ref, vecs, w1, w2, o_ref):
    _, g2, gf, _, _, _ = _vec_views(vecs)
    _ffn_tile(x_ref[...], mod_ref[0], mod_ref[1], mod_ref[2], g2, w1, w2, gf, o_ref)


def _const_spec(shape):
    nd = len(shape)
    return pl.BlockSpec(shape, lambda *_: (0,) * nd, pipeline_mode=pl.Buffered(1))


def _params(vmem_limit=None):
    return pltpu.CompilerParams(dimension_semantics=("arbitrary",),
                                vmem_limit_bytes=vmem_limit or VMEM_LIMIT)


def _mixer_scratch(nblk, rows, hist_rows, h_slots=None):
    h_shape = (nblk * rows, D_MODEL) if h_slots is None else (h_slots, nblk * rows, D_MODEL)
    return [
        pltpu.VMEM((hist_rows, POOL_WIDTH), F32),
        pltpu.VMEM((hist_rows, POOL_WIDTH), F32),
        pltpu.VMEM((hist_rows, POOL_WIDTH - POOL_GD), F32),
        pltpu.VMEM((hist_rows, POOL_WIDTH - 2 * POOL_GD), F32),
        pltpu.VMEM(h_shape, BF16),
        pltpu.VMEM((nblk, rows, GLA_DK), F32),
        pltpu.VMEM((nblk, rows, GLA_DK), F32),
        pltpu.VMEM((nblk, rows, GLA_DV), F32),
        pltpu.VMEM((nblk, rows, GLA_DV), F32),
        pltpu.VMEM((nblk, rows, GLA_DK), F32),
        pltpu.VMEM((nblk, rows, GLA_DV), BF16),
    ]


def _adaln(c_all, w_ada, b_ada):
    rows = c_all.shape[0]
    n = w_ada.shape[1]
    return pl.pallas_call(
        _adaln_kernel,
        grid=(n // ADA_BLOCK,),
        in_specs=[pl.BlockSpec((rows, D_MODEL), lambda j: (0, 0)),
                  pl.BlockSpec((D_MODEL, ADA_BLOCK), lambda j: (0, j)),
                  pl.BlockSpec((1, ADA_BLOCK), lambda j: (0, j))],
        out_specs=pl.BlockSpec((rows, ADA_BLOCK), lambda j: (0, j)),
        out_shape=jax.ShapeDtypeStruct((rows, n), F32),
        compiler_params=_params(),
        name="adaln_mod",
    )(c_all, w_ada, b_ada)


def _prompt_layer(x, weights, ffn_weights):
    n_tok = x.shape[0]
    tl = TL_MIX
    n_tiles = n_tok // tl
    consts = tuple(weights) + tuple(ffn_weights)
    return pl.pallas_call(
        _prompt_layer_kernel,
        grid=(n_tiles + 1,),
        in_specs=[pl.BlockSpec((tl, D_MODEL), lambda t: (jnp.minimum(t, n_tiles - 1), 0)),
                  pl.BlockSpec((tl, D_MODEL), lambda t: (jnp.minimum(t + 1, n_tiles - 1), 0))]
        + [_const_spec(w.shape) for w in consts],
        out_specs=[pl.BlockSpec((tl, D_MODEL), lambda t: (jnp.maximum(t - 1, 0), 0)),
                   pl.BlockSpec((GLA_HEADS, GLA_HK, GLA_HV), lambda t: (0, 0, 0)),
                   pl.BlockSpec((16, POOL_WIDTH), lambda t: (0, 0))],
        out_shape=[jax.ShapeDtypeStruct((n_tok, D_MODEL), F32),
                   jax.ShapeDtypeStruct((GLA_HEADS, GLA_HK, GLA_HV), F32),
                   jax.ShapeDtypeStruct((16, POOL_WIDTH), F32)],
        scratch_shapes=_mixer_scratch(tl // CHUNK, CHUNK, HIST_BASE + tl, h_slots=2) + [
            pltpu.VMEM((GLA_HEADS, GLA_HK, GLA_HV), F32),
            pltpu.VMEM((tl, D_MODEL), F32),
            pltpu.VMEM((tl, D_MODEL), BF16),
            pltpu.VMEM((tl, FF_CHUNK), BF16),
            pltpu.VMEM((tl, D_MODEL), F32),
        ],
        compiler_params=_params(VMEM_LIMIT_LAYER),
        name="prompt_layer",
    )(x, x, *consts)


def _sample_mixer(x, mod, s0, cache, weights):
    n_tok = x.shape[0]
    n_seq = s0.shape[0]
    seq = n_tok // n_seq
    group = SEQ_GROUP
    w_specs = [_const_spec(w.shape) for w in weights]
    st_spec = pl.BlockSpec((group, GLA_HEADS, GLA_HK, GLA_HV), lambda s: (s, 0, 0, 0))
    hist_spec = pl.BlockSpec((group, POOL_HIST, POOL_WIDTH), lambda s: (s, 0, 0))
    return pl.pallas_call(
        _sample_mixer_kernel,
        grid=(n_seq // group,),
        in_specs=[_const_spec(x.shape), _const_spec(mod.shape), st_spec, hist_spec] + w_specs,
        out_specs=[pl.BlockSpec((n_tok, D_MODEL), lambda s: (0, 0)), st_spec, hist_spec],
        out_shape=[jax.ShapeDtypeStruct((n_tok, D_MODEL), F32),
                   jax.ShapeDtypeStruct(s0.shape, F32),
                   jax.ShapeDtypeStruct(cache.shape, F32)],
        scratch_shapes=_mixer_scratch(n_seq, seq, HIST_BASE + seq) + [
            pltpu.VMEM((n_seq, seq, POOL_WIDTH), F32),
            pltpu.VMEM((n_seq, seq, POOL_WIDTH), BF16),
        ],
        compiler_params=_params(),
        name="sample_mixer",
    )(x, mod, s0, cache, *weights)


def _ffn_final(x, mod, vecs, w1, w2):
    n_tok = x.shape[0]
    tl = min(TL_FFN, n_tok)
    assert mod.shape[1] == 1 or n_tok == tl
    return pl.pallas_call(
        _ffn_kernel,
        grid=(n_tok // tl,),
        in_specs=[pl.BlockSpec((tl, D_MODEL), lambda t: (t, 0)), _const_spec(mod.shape),
                  _const_spec(vecs.shape), _const_spec(w1.shape), _const_spec(w2.shape)],
        out_specs=pl.BlockSpec((tl, D_MODEL), lambda t: (t, 0)),
        out_shape=jax.ShapeDtypeStruct((n_tok, D_MODEL), F32),
        compiler_params=_params(),
        name="ffn_final",
    )(x, mod, vecs, w1, w2)


def kernel(x_prompt, x_sample, c_prompt, c_sample, state_gla, cache_pool, w_ada, b_ada, norm1_g,
           w_in, w_alpha, b_alpha, w_pool, pool_scale, gla_norm_g, w_pa, w_pb, w_out, norm2_g,
           w_ff1, w_ff2, final_g):
    n_batch, n_seq_p, _ = x_prompt.shape
    n_dec, n_seq_s, _ = x_sample.shape
    assert n_batch == 1 and w_ada.shape[0] == 1
    assert n_seq_p % TL_MIX == 0 and SLAB % n_seq_s == 0 and n_seq_s % SUB == 0
    assert n_dec % SEQ_GROUP == 0

    n_c = n_batch + n_dec
    pad = (-n_c) % 8
    c_all = jnp.concatenate([c_prompt, c_sample, jnp.zeros((pad, D_MODEL), F32)], axis=0)
    mod = _adaln(c_all, w_ada[0], b_ada)
    mod = mod.reshape(n_c + pad, 6, D_MODEL).transpose(1, 0, 2)
    mod_s = mod[:, n_batch:n_c]

    rows = [norm1_g[0], norm2_g[0], final_g, jnp.concatenate([b_alpha[0], pool_scale[0]]),
            jnp.pad(gla_norm_g[0], (0, D_MODEL - GLA_HV))] + [mod[i, 0] for i in range(6)]
    assert len(rows) == VEC_MOD + 6
    vecs = jnp.pad(jnp.stack(rows), ((0, VEC_ROWS - len(rows)), (0, 0)))

    w_in_t = jnp.swapaxes(w_in[0], 0, 1).astype(BF16)
    weights = (vecs, w_in_t, w_alpha[0].astype(BF16), w_pool[0].astype(BF16),
               jnp.concatenate([w_pa[0], w_pb[0], w_out[0]], axis=0).astype(BF16))

    w1 = w_ff1[0].astype(BF16)
    w2 = w_ff2[0].astype(BF16)

    y_p, st_p, hist_p = _prompt_layer(x_prompt[0], weights, (w1, w2))
    x1_s, st_s, hist_s = _sample_mixer(x_sample.reshape(n_dec * n_seq_s, D_MODEL), mod_s[0:3],
                                       state_gla[0], cache_pool[0], weights)
    y_s = _ffn_final(x1_s, mod_s[3:6], vecs, w1, w2)

    return (y_p[None], y_s.reshape(n_dec, n_seq_s, D_MODEL), st_p[None, None],
            hist_p[None, None, 1:], st_s[None], hist_s[None])
```

```python
import jax
import jax.numpy as jnp
from jax import lax
from jax.experimental import pallas as pl
from jax.experimental.pallas import tpu as pltpu

D_MODEL = 1024
PAST_LEN = 4096
POOL_WIDTH = 512
POOL_WINDOWS = (2, 4, 8, 16)
POOL_GD = 128
POOL_HIST = 15
GLA_HEADS = 4
GLA_DK = 512
GLA_DV = 1024
GLA_HK = 128
GLA_HV = 256
GLA_LOWRANK = 16
GLA_GATE_NORM = 16.0
D_FF = 4096
EPS = 1e-6
LOG2_E = 1.4426950408889634
IN_SIZES = (POOL_WIDTH, GLA_DK, GLA_DK, GLA_DV, GLA_DV, GLA_LOWRANK, D_MODEL, D_MODEL)
MIX_SIZES = (POOL_WIDTH, GLA_DV, D_MODEL)

SUB = 8
CHUNK = 64
SLAB = 64
HIST_BASE = 32
TL_MIX = 256
TL_FFN = 512
FF_CHUNK = 1024
SEQ_GROUP = 4
VMEM_LIMIT = 48 * 1024 * 1024
VMEM_LIMIT_LAYER = 60 * 1024 * 1024

F32 = jnp.float32
BF16 = jnp.bfloat16


def _bdot(a, b):
    return jnp.dot(a, b, preferred_element_type=F32)


def _proj(h, wt_ref):
    return lax.dot_general(h, wt_ref[...], (((1,), (1,)), ((), ())), preferred_element_type=F32)


def _split_rows(ref, sizes):
    views, lo = [], 0
    for n in sizes:
        views.append(ref.at[pl.ds(lo, n)])
        lo += n
    return views


VEC_G1, VEC_G2, VEC_GF, VEC_POOL, VEC_GNG, VEC_MOD, VEC_ROWS = 0, 1, 2, 3, 4, 5, 16


def _vec_views(vecs):
    def row(r, lo=0, n=D_MODEL):
        return vecs.at[pl.ds(r, 1), pl.ds(lo, n)]
    return (row(VEC_G1), row(VEC_G2), row(VEC_GF), row(VEC_POOL, 0, GLA_DK),
            row(VEC_POOL, GLA_DK, POOL_WIDTH), row(VEC_GNG, 0, GLA_HV))


def _split_w_in(win):
    return _split_rows(win, IN_SIZES)


def _rms(xf):
    return xf * lax.rsqrt(jnp.mean(xf * xf, axis=-1, keepdims=True) + EPS)


def _sigmoid(x):
    return 0.5 * jnp.tanh(0.5 * x) + 0.5


def _log_sigmoid(z):
    return jnp.minimum(z, 0.0) - jnp.log(1.0 + jnp.exp(-jnp.abs(z)))


def _rows_per_seq(m, n_rows):
    n_seq = m.shape[0]
    if n_seq == 1:
        return m
    rep = n_rows // n_seq
    return jnp.concatenate([jnp.broadcast_to(m[i:i + 1], (rep, m.shape[1])) for i in range(n_seq)],
                           axis=0)


def _norm_mod(x, g, shift, scale):
    n = x.shape[0]
    return (_rms(x) * g * (1.0 + _rows_per_seq(scale, n)) + _rows_per_seq(shift, n)).astype(BF16)


def _store_blocks(ref, val):
    nblk, rows = ref.shape[0], ref.shape[1]
    for i in range(nblk):
        ref[i] = val[i * rows:(i + 1) * rows].astype(ref.dtype)


def _load_blocks(ref):
    return jnp.concatenate([ref[i] for i in range(ref.shape[0])], axis=0)


def _cumsum_rows(a, period):
    n = a.shape[0]
    ri = lax.broadcasted_iota(jnp.int32, (n, n), 0)
    ci = lax.broadcasted_iota(jnp.int32, (n, n), 1)
    shift = period.bit_length() - 1
    same = jnp.right_shift(ri, shift) == jnp.right_shift(ci, shift)
    tri = jnp.where((ci <= ri) & same, 1.0, 0.0).astype(BF16)
    hi = a.astype(BF16)
    r1 = a - hi.astype(F32)
    mid = r1.astype(BF16)
    lo = (r1 - mid.astype(F32)).astype(BF16)
    return _bdot(tri, hi) + _bdot(tri, mid) + _bdot(tri, lo)


def _gla_scores(q, k, b, k_row, b_row, nsub):
    L = SUB * nsub
    blast = b_row(L - 1)
    qt = (q * jnp.exp2(b)).astype(BF16)
    kt = (k * jnp.exp2(blast - b)).astype(BF16)

    lane = lax.broadcasted_iota(jnp.int32, (SUB, L), 1)
    row = lax.broadcasted_iota(jnp.int32, (SUB, L), 0)
    qs = [q[SUB * i:SUB * (i + 1)] for i in range(nsub)]
    ks = [k[SUB * i:SUB * (i + 1)] for i in range(nsub)]
    bs = [b[SUB * i:SUB * (i + 1)] for i in range(nsub)]

    diag = []
    for i in range(nsub):
        acc = jnp.zeros((SUB, L), F32)
        for j in range(SUB):
            r = SUB * i + j
            dec = jnp.exp2(bs[i] - b_row(r))
            col = jnp.sum(qs[i] * dec * k_row(r), axis=1, keepdims=True)
            acc = jnp.where(lane == r, col, acc)
        diag.append(jnp.where(lane <= row + SUB * i, acc, 0.0))
    p = diag[0] if nsub == 1 else jnp.concatenate(diag, axis=0)

    p_off = None
    if nsub > 1:
        zero = jnp.zeros((SUB, GLA_HK), F32)
        lhs, rhs = [], []
        for j in range(nsub - 1):
            bend = b_row(SUB * j + SUB - 1)
            lrows = [zero if i <= j else qs[i] * jnp.exp2(bs[i] - bend) for i in range(nsub)]
            rrows = [ks[j] * jnp.exp2(bend - bs[j]) if i == j else zero for i in range(nsub)]
            lhs.append(jnp.concatenate(lrows, axis=0).astype(BF16))
            rhs.append(jnp.concatenate(rrows, axis=0).astype(BF16))
        lhs = jnp.concatenate(lhs, axis=1)
        rhs = jnp.concatenate(rhs, axis=1)
        p_off = lax.dot_general(lhs, rhs, (((1,), (1,)), ((), ())), preferred_element_type=F32)
    return qt, kt, p, p_off, blast


def _gla_apply(scores, v, s):
    qt, kt, p, p_off, blast = scores
    if p_off is not None:
        p = p + p_off
    o = _bdot(jnp.concatenate([qt, p.astype(BF16)], axis=1),
              jnp.concatenate([s.astype(BF16), v], axis=0))

    ri = lax.broadcasted_iota(jnp.int32, (GLA_HK, GLA_HK), 0)
    ci = lax.broadcasted_iota(jnp.int32, (GLA_HK, GLA_HK), 1)
    erow = jnp.broadcast_to(jnp.exp2(blast), (GLA_HK, GLA_HK))
    ecol = jnp.sum(jnp.where(ri == ci, erow, 0.0), axis=1, keepdims=True)
    s_new = s * ecol + lax.dot_general(kt, v, (((0,), (0,)), ((), ())), preferred_element_type=F32)
    return o, s_new


def _gla_head_scores(blk, h, q_s, k_s, b_s):
    ksl = slice(h * GLA_HK, (h + 1) * GLA_HK)
    k_row = lambda r: k_s[blk, r:r + 1, ksl]
    b_row = lambda r: b_s[blk, r:r + 1, ksl]
    return _gla_scores(q_s[blk, :, ksl], k_s[blk, :, ksl], b_s[blk, :, ksl], k_row, b_row,
                       q_s.shape[1] // SUB)


def _gla_head_apply(blk, h, scores, v_s, sg_s, gng, bo_s, get_state, put_state):
    vsl = slice(h * GLA_HV, (h + 1) * GLA_HV)
    o, s_new = _gla_apply(scores, v_s[blk, :, vsl].astype(BF16), get_state(h))
    put_state(h, s_new)
    o = _rms(o) * gng
    bo_s[blk, :, vsl] = (o * sg_s[blk, :, vsl]).astype(BF16)


def _pool_delta(u_b, s2_b, s4_b, s8_b, n, pos0):
    r = HIST_BASE + n
    gd = POOL_GD
    s2_b[16:r, :] = u_b[16:r, :] + u_b[15:r - 1, :]
    s4_b[16:r, :] = s2_b[16:r, gd:] + s2_b[14:r - 2, gd:]
    s8_b[24:r, :] = s4_b[24:r, gd:] + s4_b[20:r - 4, gd:]
    s16 = s8_b[HIST_BASE:r, gd:] + s8_b[HIST_BASE - 8:r - 8, gd:]
    sums = (s2_b[HIST_BASE:r, 0:gd], s4_b[HIST_BASE:r, 0:gd], s8_b[HIST_BASE:r, 0:gd], s16)
    pos1 = pos0 + lax.broadcasted_iota(jnp.int32, (n, 1), 0) + 1
    out = []
    for gi, w in enumerate(POOL_WINDOWS):
        cnt = jnp.minimum(pos1, w).astype(F32)
        out.append(sums[gi] / cnt - u_b[HIST_BASE:r, gi * POOL_GD:(gi + 1) * POOL_GD])
    return out


def _pool_mix(d, wpool, pscale):
    mixed = [_bdot(d[gi].astype(BF16), wpool[gi]) for gi in range(len(POOL_WINDOWS))]
    return (jnp.concatenate(mixed, axis=1) * pscale).astype(BF16)


def _project_gla(x, mod_ref, g1, wq, wk, wv, wg, walr, walpha, balpha, period,
                 h_s, q_s, k_s, v_s, sg_s, b_s):
    h = _norm_mod(x, g1[...], mod_ref[0], mod_ref[1])
    h_s[...] = h
    _store_blocks(q_s, _proj(h, wq) * (GLA_HK ** -0.5))
    _store_blocks(k_s, _proj(h, wk))
    _store_blocks(v_s, _proj(h, wv))
    g = _proj(h, wg)
    _store_blocks(sg_s, g * _sigmoid(g))
    alr = _proj(h, walr)
    z = _bdot(alr.astype(BF16), walpha[...]) + balpha[...]
    a = _log_sigmoid(z) * (LOG2_E / GLA_GATE_NORM)
    b = [_cumsum_rows(a[r:r + SLAB], period) for r in range(0, a.shape[0], SLAB)]
    _store_blocks(b_s, jnp.concatenate(b, axis=0))


def _gated_pool(h_s, aout, wga, wpa):
    return _sigmoid(_proj(h_s[...], wga)) * _bdot(aout, wpa[...])


def _gate_b(h_s, wgb):
    return _sigmoid(_proj(h_s[...], wgb))


def _merge_out(x, gate1, merged_a, gate_b, bo, wpb, wout):
    merged = merged_a + gate_b * _bdot(bo, wpb[...])
    y = _bdot(merged.astype(BF16), wout[...])
    return x + _rows_per_seq(gate1, x.shape[0]) * y


def _ffn_tile(x, shift, scale, gate, g2, w1, w2, gf, o_ref):
    h2 = _norm_mod(x, g2[...], shift, scale)
    acc = jnp.zeros(x.shape, F32)
    for c in range(D_FF // FF_CHUNK):
        sl = slice(c * FF_CHUNK, (c + 1) * FF_CHUNK)
        y = jnp.maximum(_bdot(h2, w1[:, sl]), 0.0)
        acc = acc + _bdot((y * y).astype(BF16), w2[sl, :])
    x2 = x + _rows_per_seq(gate, x.shape[0]) * acc
    o_ref[...] = _rms(x2) * gf[...]


def _adaln_kernel(c_ref, w_ref, b_ref, o_ref):
    c = c_ref[...]
    sc = (c * _sigmoid(c)).astype(BF16)
    o_ref[0] = _bdot(sc, w_ref[...].astype(BF16)) + b_ref[...]


def _prompt_layer_kernel(x_ref, xn_ref, vecs, win, walpha, wpool, wmix, w1, w2,
                         y_ref, st_ref, hist_ref,
                         u_b, s2_b, s4_b, s8_b, hh_s, q_s, k_s, v_s, sg_s, b_s, bo_s,
                         st_s, x1_s, h2_s, y_s, acc_s):
    wu, wq, wk, wv, wg, walr, wga, wgb = _split_w_in(win)
    wpa, wpb, wout = _split_rows(wmix, MIX_SIZES)
    g1, g2, gf, balpha, pscale, gng = _vec_views(vecs)
    mod_ref = [vecs[VEC_MOD + i:VEC_MOD + i + 1, :] for i in range(6)]
    t = pl.program_id(0)
    n_tiles = pl.num_programs(0) - 1
    tl = x_ref.shape[0]
    n_chunks = tl // CHUNK
    n_piece = GLA_HEADS
    piece = D_MODEL // n_piece
    assert n_chunks * FF_CHUNK == w1.shape[1] and FF_CHUNK == n_piece * piece

    @pl.when(t == 0)
    def _():
        st_s[...] = jnp.zeros_like(st_s)
        u_b[0:HIST_BASE, :] = jnp.zeros((HIST_BASE, POOL_WIDTH), F32)
        s2_b[0:16, :] = jnp.zeros((16, POOL_WIDTH), F32)
        x1_s[...] = jnp.zeros_like(x1_s)
        hh_s[0] = _norm_mod(x_ref[...], g1[...], mod_ref[0], mod_ref[1])

    @pl.when(t < n_tiles)
    def _():
        slot = lax.rem(t, 2)
        alr = _proj(hh_s[slot], walr)
        _store_blocks(k_s, _proj(hh_s[slot], wk))
        h2_s[...] = _norm_mod(x1_s[...], g2[...], mod_ref[3], mod_ref[4])
        z = _bdot(alr.astype(BF16), walpha[...]) + balpha[...]
        a = _log_sigmoid(z) * (LOG2_E / GLA_GATE_NORM)
        _store_blocks(q_s, _proj(hh_s[slot], wq) * (GLA_HK ** -0.5))
        b = [_cumsum_rows(a[r:r + SLAB], CHUNK) for r in range(0, tl, SLAB)]
        _store_blocks(b_s, jnp.concatenate(b, axis=0))
        _store_blocks(v_s, _proj(hh_s[slot], wv))
        g = _proj(hh_s[slot], wg)
        _store_blocks(sg_s, g * _sigmoid(g))

        def get_state(h):
            return st_s[h]

        def put_state(h, s_new):
            st_s[h] = s_new

        acc_s[...] = jnp.zeros_like(acc_s)

        def chunk_body(c):
            def scores(h):
                return _gla_head_scores(c, h, q_s, k_s, b_s)

            def apply(h, sc):
                _gla_head_apply(c, h, sc, v_s, sg_s, gng[...], bo_s, get_state, put_state)

            def ffn_up(j):
                cols = slice(j * piece, (j + 1) * piece)
                w1_cols = slice(c * FF_CHUNK + j * piece, c * FF_CHUNK + (j + 1) * piece)
                y = jnp.maximum(_bdot(h2_s[...], w1[:, w1_cols]), 0.0)
                y_s[:, cols] = (y * y).astype(BF16)

            def ffn_down(j):
                cols = slice(j * piece, (j + 1) * piece)
                acc_s[:, cols] += _bdot(y_s[...], w2[c * FF_CHUNK:(c + 1) * FF_CHUNK, cols])

            sc0 = scores(0)
            sc1 = scores(1)
            ffn_up(0)
            sc2 = scores(2)
            sc3 = scores(3)
            ffn_up(1)
            apply(0, sc0)
            ffn_up(2)
            apply(1, sc1)
            ffn_up(3)
            apply(2, sc2)
            ffn_down(0)
            apply(3, sc3)
            ffn_down(1)
            ffn_down(2)
            ffn_down(3)

        for c in range(n_chunks):
            chunk_body(c)

        x2 = x1_s[...] + mod_ref[5] * acc_s[...]
        y_ref[...] = _rms(x2) * gf[...]

        hh_s[1 - slot] = _norm_mod(xn_ref[...], g1[...], mod_ref[0], mod_ref[1])

        u_b[HIST_BASE:HIST_BASE + tl, :] = _proj(hh_s[slot], wu)
        gate_a = _sigmoid(_proj(hh_s[slot], wga))
        gate_b = _sigmoid(_proj(hh_s[slot], wgb))
        d = _pool_delta(u_b, s2_b, s4_b, s8_b, tl, t * tl)
        aout = _pool_mix(d, wpool, pscale[...])
        u_b[16:HIST_BASE, :] = u_b[16 + tl:HIST_BASE + tl, :]
        merged = gate_a * _bdot(aout, wpa[...]) + gate_b * _bdot(_load_blocks(bo_s), wpb[...])
        x1 = x_ref[...] + mod_ref[2] * _bdot(merged.astype(BF16), wout[...])
        x1_s[...] = x1

    @pl.when(t == n_tiles - 1)
    def _():
        st_ref[...] = st_s[...]
        hist_ref[...] = u_b[16:HIST_BASE, :]

    @pl.when(t == n_tiles)
    def _():
        _ffn_tile(x1_s[...], mod_ref[3], mod_ref[4], mod_ref[5], g2, w1, w2, gf, y_ref)


def _sample_mixer_kernel(x_ref, mod_ref, s0_ref, cache_ref, vecs, win, walpha, wpool, wmix,
                         x1_ref, st_ref, hist_ref,
                         u_b, s2_b, s4_b, s8_b, h_s, q_s, k_s, v_s, sg_s, b_s, bo_s, u_s, ao_s):
    wu, wq, wk, wv, wg, walr, wga, wgb = _split_w_in(win)
    wpa, wpb, wout = _split_rows(wmix, MIX_SIZES)
    g1, _, _, balpha, pscale, gng = _vec_views(vecs)
    s = pl.program_id(0)
    seq = q_s.shape[1]
    group = s0_ref.shape[0]

    @pl.when(s == 0)
    def _():
        u_b[0:HIST_BASE, :] = jnp.zeros((HIST_BASE, POOL_WIDTH), F32)
        s2_b[0:16, :] = jnp.zeros((16, POOL_WIDTH), F32)
        _project_gla(x_ref[...], mod_ref, g1, wq, wk, wv, wg, walr, walpha, balpha, seq,
                     h_s, q_s, k_s, v_s, sg_s, b_s)
        _store_blocks(u_s, _proj(h_s[...], wu))

    scores = [[_gla_head_scores(s * group + i, h, q_s, k_s, b_s) for h in range(GLA_HEADS)]
              for i in range(group)]

    for i in range(group):
        blk = s * group + i
        u_b[HIST_BASE - POOL_HIST:HIST_BASE, :] = cache_ref[i]
        u_b[HIST_BASE:HIST_BASE + seq, :] = u_s[blk]
        d = _pool_delta(u_b, s2_b, s4_b, s8_b, seq, PAST_LEN)
        ao_s[blk] = _pool_mix(d, wpool, pscale[...])
        hist_ref[i] = u_b[HIST_BASE + seq - POOL_HIST:HIST_BASE + seq, :]

    for i in range(group):
        def get_state(h, i=i):
            return s0_ref[i, h]

        def put_state(h, s_new, i=i):
            st_ref[i, h] = s_new

        for h in range(GLA_HEADS):
            _gla_head_apply(s * group + i, h, scores[i][h], v_s, sg_s, gng[...], bo_s,
                            get_state, put_state)

    @pl.when(s == pl.num_programs(0) - 1)
    def _():
        merged_a = _gated_pool(h_s, _load_blocks(ao_s), wga, wpa)
        x1_ref[...] = _merge_out(x_ref[...], mod_ref[2], merged_a, _gate_b(h_s, wgb),
                                 _load_blocks(bo_s), wpb, wout)


def _ffn_kernel(x_ref, mod_ref, vecs, w1, w2, o_ref):
    _, g2, gf, _, _, _ = _vec_views(vecs)
    _ffn_tile(x_ref[...], mod_ref[0], mod_ref[1], mod_ref[2], g2, w1, w2, gf, o_ref)


def _const_spec(shape):
    nd = len(shape)
    return pl.BlockSpec(shape, lambda *_: (0,) * nd, pipeline_mode=pl.Buffered(1))


def _params(vmem_limit=None):
    return pltpu.CompilerParams(dimension_semantics=("arbitrary",),
                                vmem_limit_bytes=vmem_limit or VMEM_LIMIT)


def _mixer_scratch(nblk, rows, hist_rows, h_slots=None):
    h_shape = (nblk * rows, D_MODEL) if h_slots is None else (h_slots, nblk * rows, D_MODEL)
    return [
        pltpu.VMEM((hist_rows, POOL_WIDTH), F32),
        pltpu.VMEM((hist_rows, POOL_WIDTH), F32),
        pltpu.VMEM((hist_rows, POOL_WIDTH - POOL_GD), F32),
        pltpu.VMEM((hist_rows, POOL_WIDTH - 2 * POOL_GD), F32),
        pltpu.VMEM(h_shape, BF16),
        pltpu.VMEM((nblk, rows, GLA_DK), F32),
        pltpu.VMEM((nblk, rows, GLA_DK), F32),
        pltpu.VMEM((nblk, rows, GLA_DV), F32),
        pltpu.VMEM((nblk, rows, GLA_DV), F32),
        pltpu.VMEM((nblk, rows, GLA_DK), F32),
        pltpu.VMEM((nblk, rows, GLA_DV), BF16),
    ]


def _adaln(c_all, w_ada, b_ada):
    rows = c_all.shape[0]
    n = w_ada.shape[1]
    return pl.pallas_call(
        _adaln_kernel,
        grid=(n // D_MODEL,),
        in_specs=[pl.BlockSpec((rows, D_MODEL), lambda j: (0, 0)),
                  pl.BlockSpec((D_MODEL, D_MODEL), lambda j: (0, j)),
                  pl.BlockSpec((1, D_MODEL), lambda j: (0, j))],
        out_specs=pl.BlockSpec((1, rows, D_MODEL), lambda j: (j, 0, 0)),
        out_shape=jax.ShapeDtypeStruct((n // D_MODEL, rows, D_MODEL), F32),
        compiler_params=_params(),
        name="adaln_mod",
    )(c_all, w_ada, b_ada)


def _prompt_layer(x, weights, ffn_weights):
    n_tok = x.shape[0]
    tl = TL_MIX
    n_tiles = n_tok // tl
    consts = tuple(weights) + tuple(ffn_weights)
    return pl.pallas_call(
        _prompt_layer_kernel,
        grid=(n_tiles + 1,),
        in_specs=[pl.BlockSpec((tl, D_MODEL), lambda t: (jnp.minimum(t, n_tiles - 1), 0)),
                  pl.BlockSpec((tl, D_MODEL), lambda t: (jnp.minimum(t + 1, n_tiles - 1), 0))]
        + [_const_spec(w.shape) for w in consts],
        out_specs=[pl.BlockSpec((tl, D_MODEL), lambda t: (jnp.maximum(t - 1, 0), 0)),
                   pl.BlockSpec((GLA_HEADS, GLA_HK, GLA_HV), lambda t: (0, 0, 0)),
                   pl.BlockSpec((16, POOL_WIDTH), lambda t: (0, 0))],
        out_shape=[jax.ShapeDtypeStruct((n_tok, D_MODEL), F32),
                   jax.ShapeDtypeStruct((GLA_HEADS, GLA_HK, GLA_HV), F32),
                   jax.ShapeDtypeStruct((16, POOL_WIDTH), F32)],
        scratch_shapes=_mixer_scratch(tl // CHUNK, CHUNK, HIST_BASE + tl, h_slots=2) + [
            pltpu.VMEM((GLA_HEADS, GLA_HK, GLA_HV), F32),
            pltpu.VMEM((tl, D_MODEL), F32),
            pltpu.VMEM((tl, D_MODEL), BF16),
            pltpu.VMEM((tl, FF_CHUNK), BF16),
            pltpu.VMEM((tl, D_MODEL), F32),
        ],
        compiler_params=_params(VMEM_LIMIT_LAYER),
        name="prompt_layer",
    )(x, x, *consts)


def _sample_mixer(x, mod, s0, cache, weights):
    n_tok = x.shape[0]
    n_seq = s0.shape[0]
    seq = n_tok // n_seq
    group = SEQ_GROUP
    w_specs = [_const_spec(w.shape) for w in weights]
    st_spec = pl.BlockSpec((group, GLA_HEADS, GLA_HK, GLA_HV), lambda s: (s, 0, 0, 0))
    hist_spec = pl.BlockSpec((group, POOL_HIST, POOL_WIDTH), lambda s: (s, 0, 0))
    return pl.pallas_call(
        _sample_mixer_kernel,
        grid=(n_seq // group,),
        in_specs=[_const_spec(x.shape), _const_spec(mod.shape), st_spec, hist_spec] + w_specs,
        out_specs=[pl.BlockSpec((n_tok, D_MODEL), lambda s: (0, 0)), st_spec, hist_spec],
        out_shape=[jax.ShapeDtypeStruct((n_tok, D_MODEL), F32),
                   jax.ShapeDtypeStruct(s0.shape, F32),
                   jax.ShapeDtypeStruct(cache.shape, F32)],
        scratch_shapes=_mixer_scratch(n_seq, seq, HIST_BASE + seq) + [
            pltpu.VMEM((n_seq, seq, POOL_WIDTH), F32),
            pltpu.VMEM((n_seq, seq, POOL_WIDTH), BF16),
        ],
        compiler_params=_params(),
        name="sample_mixer",
    )(x, mod, s0, cache, *weights)


def _ffn_final(x, mod, vecs, w1, w2):
    n_tok = x.shape[0]
    tl = min(TL_FFN, n_tok)
    assert mod.shape[1] == 1 or n_tok == tl
    return pl.pallas_call(
        _ffn_kernel,
        grid=(n_tok // tl,),
        in_specs=[pl.BlockSpec((tl, D_MODEL), lambda t: (t, 0)), _const_spec(mod.shape),
                  _const_spec(vecs.shape), _const_spec(w1.shape), _const_spec(w2.shape)],
        out_specs=pl.BlockSpec((tl, D_MODEL), lambda t: (t, 0)),
        out_shape=jax.ShapeDtypeStruct((n_tok, D_MODEL), F32),
        compiler_params=_params(),
        name="ffn_final",
    )(x, mod, vecs, w1, w2)


def kernel(x_prompt, x_sample, c_prompt, c_sample, state_gla, cache_pool, w_ada, b_ada, norm1_g,
           w_in, w_alpha, b_alpha, w_pool, pool_scale, gla_norm_g, w_pa, w_pb, w_out, norm2_g,
           w_ff1, w_ff2, final_g):
    n_batch, n_seq_p, _ = x_prompt.shape
    n_dec, n_seq_s, _ = x_sample.shape
    assert n_batch == 1 and w_ada.shape[0] == 1
    assert n_seq_p % TL_MIX == 0 and SLAB % n_seq_s == 0 and n_seq_s % SUB == 0
    assert n_dec % SEQ_GROUP == 0

    n_c = n_batch + n_dec
    pad = (-n_c) % 8
    c_all = jnp.concatenate([c_prompt, c_sample, jnp.zeros((pad, D_MODEL), F32)], axis=0)
    mod = _adaln(c_all, w_ada[0], b_ada)
    mod_s = mod[:, n_batch:n_c]

    rows = [norm1_g[0], norm2_g[0], final_g, jnp.concatenate([b_alpha[0], pool_scale[0]]),
            jnp.pad(gla_norm_g[0], (0, D_MODEL - GLA_HV))] + [mod[i, 0] for i in range(6)]
    assert len(rows) == VEC_MOD + 6
    vecs = jnp.pad(jnp.stack(rows), ((0, VEC_ROWS - len(rows)), (0, 0)))

    w_in_t = jnp.swapaxes(w_in[0], 0, 1).astype(BF16)
    weights = (vecs, w_in_t, w_alpha[0].astype(BF16), w_pool[0].astype(BF16),
               jnp.concatenate([w_pa[0], w_pb[0], w_out[0]], axis=0).astype(BF16))

    w1 = w_ff1[0].astype(BF16)
    w2 = w_ff2[0].astype(BF16)

    y_p, st_p, hist_p = _prompt_layer(x_prompt[0], weights, (w1, w2))
    x1_s, st_s, hist_s = _sample_mixer(x_sample.reshape(n_dec * n_seq_s, D_MODEL), mod_s[0:3],
                                       state_gla[0], cache_pool[0], weights)
    y_s = _ffn_final(x1_s, mod_s[3:6], vecs, w1, w2)

    return (y_p[None], y_s.reshape(n_dec, n_seq_s, D_MODEL), st_p[None, None],
            hist_p[None, None, 1:], st_s[None], hist_s[None])
```

```python
import functools

import jax
import jax.numpy as jnp
from jax import lax
from jax.experimental import pallas as pl
from jax.experimental.pallas import tpu as pltpu

D_MODEL = 1024
PAST_LEN = 4096
POOL_WIDTH = 512
POOL_WINDOWS = (2, 4, 8, 16)
POOL_GD = 128
POOL_HIST = 15
GLA_HEADS = 4
GLA_DK = 512
GLA_DV = 1024
GLA_HK = 128
GLA_HV = 256
GLA_LOWRANK = 16
GLA_GATE_NORM = 16.0
D_FF = 4096
EPS = 1e-6
LOG2_E = 1.4426950408889634
IN_SIZES = (POOL_WIDTH, GLA_DK, GLA_DK, GLA_DV, GLA_DV, GLA_LOWRANK, D_MODEL, D_MODEL)
MIX_SIZES = (POOL_WIDTH, GLA_DV, D_MODEL)

SUB = 8
CHUNK = 64
SLAB = 64
HIST_BASE = 32
TL_MIX = 256
FF_CHUNK = 1024
ADA_BLOCK = 1536
SEQ_GROUP = 4
VMEM_LIMIT = 48 * 1024 * 1024
VMEM_LIMIT_LAYER = 60 * 1024 * 1024

F32 = jnp.float32
BF16 = jnp.bfloat16


def _bdot(a, b):
    return jnp.dot(a, b, preferred_element_type=F32)


def _proj(h, wt_ref):
    return lax.dot_general(h, wt_ref[...], (((1,), (1,)), ((), ())), preferred_element_type=F32)


def _split_rows(ref, sizes):
    views, lo = [], 0
    for n in sizes:
        views.append(ref.at[pl.ds(lo, n)])
        lo += n
    return views


VEC_G1, VEC_G2, VEC_GF, VEC_POOL, VEC_GNG, VEC_MOD, VEC_ROWS = 0, 1, 2, 3, 4, 5, 16


def _vec_views(vecs):
    def row(r, lo=0, n=D_MODEL):
        return vecs.at[pl.ds(r, 1), pl.ds(lo, n)]
    return (row(VEC_G1), row(VEC_G2), row(VEC_GF), row(VEC_POOL, 0, GLA_DK),
            row(VEC_POOL, GLA_DK, POOL_WIDTH), row(VEC_GNG, 0, GLA_HV))


def _split_w_in(win):
    return _split_rows(win, IN_SIZES)


def _rms(xf):
    return xf * lax.rsqrt(jnp.mean(xf * xf, axis=-1, keepdims=True) + EPS)


def _sigmoid(x):
    return 0.5 * jnp.tanh(0.5 * x) + 0.5


def _log_sigmoid(z):
    return jnp.minimum(z, 0.0) - jnp.log(1.0 + jnp.exp(-jnp.abs(z)))


def _rows_per_seq(m, n_rows):
    n_seq = m.shape[0]
    if n_seq == 1:
        return m
    rep = n_rows // n_seq
    return jnp.concatenate([jnp.broadcast_to(m[i:i + 1], (rep, m.shape[1])) for i in range(n_seq)],
                           axis=0)


def _norm_mod(x, g, shift, scale):
    n = x.shape[0]
    return (_rms(x) * g * (1.0 + _rows_per_seq(scale, n)) + _rows_per_seq(shift, n)).astype(BF16)


def _store_blocks(ref, val):
    nblk, rows = ref.shape[0], ref.shape[1]
    for i in range(nblk):
        ref[i] = val[i * rows:(i + 1) * rows].astype(ref.dtype)


def _load_blocks(ref):
    return jnp.concatenate([ref[i] for i in range(ref.shape[0])], axis=0)


def _cumsum_rows(a, period):
    n = a.shape[0]
    ri = lax.broadcasted_iota(jnp.int32, (n, n), 0)
    ci = lax.broadcasted_iota(jnp.int32, (n, n), 1)
    shift = period.bit_length() - 1
    same = jnp.right_shift(ri, shift) == jnp.right_shift(ci, shift)
    tri = jnp.where((ci <= ri) & same, 1.0, 0.0).astype(BF16)
    hi = a.astype(BF16)
    r1 = a - hi.astype(F32)
    mid = r1.astype(BF16)
    lo = (r1 - mid.astype(F32)).astype(BF16)
    return _bdot(tri, hi) + _bdot(tri, mid) + _bdot(tri, lo)


def _gla_scores(q, k, b, k_row, b_row, nsub):
    L = SUB * nsub
    blast = b_row(L - 1)
    qt = (q * jnp.exp2(b)).astype(BF16)
    kt = (k * jnp.exp2(blast - b)).astype(BF16)

    lane = lax.broadcasted_iota(jnp.int32, (SUB, L), 1)
    row = lax.broadcasted_iota(jnp.int32, (SUB, L), 0)
    qs = [q[SUB * i:SUB * (i + 1)] for i in range(nsub)]
    ks = [k[SUB * i:SUB * (i + 1)] for i in range(nsub)]
    bs = [b[SUB * i:SUB * (i + 1)] for i in range(nsub)]

    diag = []
    for i in range(nsub):
        acc = jnp.zeros((SUB, L), F32)
        for j in range(SUB):
            r = SUB * i + j
            dec = jnp.exp2(bs[i] - b_row(r))
            col = jnp.sum(qs[i] * dec * k_row(r), axis=1, keepdims=True)
            acc = jnp.where(lane == r, col, acc)
        diag.append(jnp.where(lane <= row + SUB * i, acc, 0.0))
    p = diag[0] if nsub == 1 else jnp.concatenate(diag, axis=0)

    p_off = None
    if nsub > 1:
        zero = jnp.zeros((SUB, GLA_HK), F32)
        lhs, rhs = [], []
        for j in range(nsub - 1):
            bend = b_row(SUB * j + SUB - 1)
            lrows = [zero if i <= j else qs[i] * jnp.exp2(bs[i] - bend) for i in range(nsub)]
            rrows = [ks[j] * jnp.exp2(bend - bs[j]) if i == j else zero for i in range(nsub)]
            lhs.append(jnp.concatenate(lrows, axis=0).astype(BF16))
            rhs.append(jnp.concatenate(rrows, axis=0).astype(BF16))
        lhs = jnp.concatenate(lhs, axis=1)
        rhs = jnp.concatenate(rhs, axis=1)
        p_off = lax.dot_general(lhs, rhs, (((1,), (1,)), ((), ())), preferred_element_type=F32)
    return qt, kt, p, p_off, blast


def _gla_apply(scores, v, s):
    qt, kt, p, p_off, blast = scores
    if p_off is not None:
        p = p + p_off
    o = _bdot(jnp.concatenate([qt, p.astype(BF16)], axis=1),
              jnp.concatenate([s.astype(BF16), v], axis=0))

    ri = lax.broadcasted_iota(jnp.int32, (GLA_HK, GLA_HK), 0)
    ci = lax.broadcasted_iota(jnp.int32, (GLA_HK, GLA_HK), 1)
    erow = jnp.broadcast_to(jnp.exp2(blast), (GLA_HK, GLA_HK))
    ecol = jnp.sum(jnp.where(ri == ci, erow, 0.0), axis=1, keepdims=True)
    s_new = s * ecol + lax.dot_general(kt, v, (((0,), (0,)), ((), ())), preferred_element_type=F32)
    return o, s_new


def _gla_head_scores(blk, h, q_s, k_s, b_s):
    ksl = slice(h * GLA_HK, (h + 1) * GLA_HK)
    k_row = lambda r: k_s[blk, r:r + 1, ksl]
    b_row = lambda r: b_s[blk, r:r + 1, ksl]
    return _gla_scores(q_s[blk, :, ksl], k_s[blk, :, ksl], b_s[blk, :, ksl], k_row, b_row,
                       q_s.shape[1] // SUB)


def _gla_head_apply(blk, h, scores, v_s, sg_s, gng, bo_s, get_state, put_state):
    vsl = slice(h * GLA_HV, (h + 1) * GLA_HV)
    o, s_new = _gla_apply(scores, v_s[blk, :, vsl].astype(BF16), get_state(h))
    put_state(h, s_new)
    o = _rms(o) * gng
    bo_s[blk, :, vsl] = (o * sg_s[blk, :, vsl]).astype(BF16)


def _pool_delta(u_b, s2_b, s4_b, s8_b, n, pos0):
    r = HIST_BASE + n
    gd = POOL_GD
    s2_b[16:r, :] = u_b[16:r, :] + u_b[15:r - 1, :]
    s4_b[16:r, :] = s2_b[16:r, gd:] + s2_b[14:r - 2, gd:]
    s8_b[24:r, :] = s4_b[24:r, gd:] + s4_b[20:r - 4, gd:]
    s16 = s8_b[HIST_BASE:r, gd:] + s8_b[HIST_BASE - 8:r - 8, gd:]
    sums = (s2_b[HIST_BASE:r, 0:gd], s4_b[HIST_BASE:r, 0:gd], s8_b[HIST_BASE:r, 0:gd], s16)
    pos1 = pos0 + lax.broadcasted_iota(jnp.int32, (n, 1), 0) + 1
    out = []
    for gi, w in enumerate(POOL_WINDOWS):
        cnt = jnp.minimum(pos1, w).astype(F32)
        out.append(sums[gi] / cnt - u_b[HIST_BASE:r, gi * POOL_GD:(gi + 1) * POOL_GD])
    return out


def _pool_mix(d, wpool, pscale):
    mixed = [_bdot(d[gi].astype(BF16), wpool[gi]) for gi in range(len(POOL_WINDOWS))]
    return (jnp.concatenate(mixed, axis=1) * pscale).astype(BF16)


def _project_gla(x, mod_ref, g1, wq, wk, wv, wg, walr, walpha, balpha, period,
                 h_s, q_s, k_s, v_s, sg_s, b_s):
    h = _norm_mod(x, g1[...], mod_ref[0], mod_ref[1])
    h_s[...] = h
    _store_blocks(q_s, _proj(h, wq) * (GLA_HK ** -0.5))
    _store_blocks(k_s, _proj(h, wk))
    _store_blocks(v_s, _proj(h, wv))
    g = _proj(h, wg)
    _store_blocks(sg_s, g * _sigmoid(g))
    alr = _proj(h, walr)
    z = _bdot(alr.astype(BF16), walpha[...]) + balpha[...]
    a = _log_sigmoid(z) * (LOG2_E / GLA_GATE_NORM)
    b = [_cumsum_rows(a[r:r + SLAB], period) for r in range(0, a.shape[0], SLAB)]
    _store_blocks(b_s, jnp.concatenate(b, axis=0))


def _gated_pool(h_s, aout, wga, wpa):
    return _sigmoid(_proj(h_s[...], wga)) * _bdot(aout, wpa[...])


def _gate_b(h_s, wgb):
    return _sigmoid(_proj(h_s[...], wgb))


def _merge_out(x, gate1, merged_a, gate_b, bo, wpb, wout):
    merged = merged_a + gate_b * _bdot(bo, wpb[...])
    y = _bdot(merged.astype(BF16), wout[...])
    return x + _rows_per_seq(gate1, x.shape[0]) * y


def _ffn_tile(x, shift, scale, gate, g2, w1, w2, gf, o_ref):
    h2 = _norm_mod(x, g2[...], shift, scale)
    acc = jnp.zeros(x.shape, F32)
    for c in range(D_FF // FF_CHUNK):
        sl = slice(c * FF_CHUNK, (c + 1) * FF_CHUNK)
        y = jnp.maximum(_bdot(h2, w1[:, sl]), 0.0)
        acc = acc + _bdot((y * y).astype(BF16), w2[sl, :])
    x2 = x + _rows_per_seq(gate, x.shape[0]) * acc
    o_ref[...] = _rms(x2) * gf[...]


def _adaln_kernel(c_ref, w_ref, b_ref, o_ref):
    c = c_ref[...]
    sc = (c * _sigmoid(c)).astype(BF16)
    o_ref[...] = _bdot(sc, w_ref[...].astype(BF16)) + b_ref[...]


def _prompt_layer_kernel(x0_ref, xn_ref, vecs, win, walpha, wpool, wmix, w1, w2,
                         y_ref, st_ref, hist_ref,
                         u_b, s2_b, s4_b, s8_b, hh_s, q_s, k_s, v_s, sg_s, b_s, bo_s,
                         st_s, x1_s, h2_s, y_s, acc_s, xc_s):
    wu, wq, wk, wv, wg, walr, wga, wgb = _split_w_in(win)
    wpa, wpb, wout = _split_rows(wmix, MIX_SIZES)
    g1, g2, gf, balpha, pscale, gng = _vec_views(vecs)
    mod_ref = [vecs[VEC_MOD + i:VEC_MOD + i + 1, :] for i in range(6)]
    t = pl.program_id(0)
    n_tiles = pl.num_programs(0) - 1
    tl = xn_ref.shape[0]
    n_chunks = tl // CHUNK
    n_piece = GLA_HEADS
    piece = D_MODEL // n_piece
    assert n_chunks * FF_CHUNK == w1.shape[1] and FF_CHUNK == n_piece * piece

    @pl.when(t == 0)
    def _():
        st_s[...] = jnp.zeros_like(st_s)
        u_b[0:HIST_BASE, :] = jnp.zeros((HIST_BASE, POOL_WIDTH), F32)
        s2_b[0:16, :] = jnp.zeros((16, POOL_WIDTH), F32)
        x1_s[...] = jnp.zeros_like(x1_s)
        xc_s[...] = x0_ref[...]
        hh_s[0] = _norm_mod(x0_ref[...], g1[...], mod_ref[0], mod_ref[1])

    @pl.when(t < n_tiles)
    def _():
        slot = lax.rem(t, 2)
        alr = _proj(hh_s[slot], walr)
        _store_blocks(k_s, _proj(hh_s[slot], wk))
        h2_s[...] = _norm_mod(x1_s[...], g2[...], mod_ref[3], mod_ref[4])
        z = _bdot(alr.astype(BF16), walpha[...]) + balpha[...]
        a = _log_sigmoid(z) * (LOG2_E / GLA_GATE_NORM)
        _store_blocks(q_s, _proj(hh_s[slot], wq) * (GLA_HK ** -0.5))
        b = [_cumsum_rows(a[r:r + SLAB], CHUNK) for r in range(0, tl, SLAB)]
        _store_blocks(b_s, jnp.concatenate(b, axis=0))
        _store_blocks(v_s, _proj(hh_s[slot], wv))
        g = _proj(hh_s[slot], wg)
        _store_blocks(sg_s, g * _sigmoid(g))

        def get_state(h):
            return st_s[h]

        def put_state(h, s_new):
            st_s[h] = s_new

        acc_s[...] = jnp.zeros_like(acc_s)

        def chunk_body(c):
            def scores(h):
                return _gla_head_scores(c, h, q_s, k_s, b_s)

            def apply(h, sc):
                _gla_head_apply(c, h, sc, v_s, sg_s, gng[...], bo_s, get_state, put_state)

            def ffn_up(j):
                cols = slice(j * piece, (j + 1) * piece)
                w1_cols = slice(c * FF_CHUNK + j * piece, c * FF_CHUNK + (j + 1) * piece)
                y = jnp.maximum(_bdot(h2_s[...], w1[:, w1_cols]), 0.0)
                y_s[:, cols] = (y * y).astype(BF16)

            def ffn_down(j):
                cols = slice(j * piece, (j + 1) * piece)
                acc_s[:, cols] += _bdot(y_s[...], w2[c * FF_CHUNK:(c + 1) * FF_CHUNK, cols])

            sc0 = scores(0)
            sc1 = scores(1)
            ffn_up(0)
            sc2 = scores(2)
            sc3 = scores(3)
            ffn_up(1)
            apply(0, sc0)
            ffn_up(2)
            apply(1, sc1)
            ffn_up(3)
            apply(2, sc2)
            ffn_down(0)
            apply(3, sc3)
            ffn_down(1)
            ffn_down(2)
            ffn_down(3)

        for c in range(n_chunks):
            chunk_body(c)

        x2 = x1_s[...] + mod_ref[5] * acc_s[...]
        y_ref[...] = _rms(x2) * gf[...]

        hh_s[1 - slot] = _norm_mod(xn_ref[...], g1[...], mod_ref[0], mod_ref[1])

        u_b[HIST_BASE:HIST_BASE + tl, :] = _proj(hh_s[slot], wu)
        gate_a = _sigmoid(_proj(hh_s[slot], wga))
        gate_b = _sigmoid(_proj(hh_s[slot], wgb))
        d = _pool_delta(u_b, s2_b, s4_b, s8_b, tl, t * tl)
        aout = _pool_mix(d, wpool, pscale[...])
        u_b[16:HIST_BASE, :] = u_b[16 + tl:HIST_BASE + tl, :]
        merged = gate_a * _bdot(aout, wpa[...]) + gate_b * _bdot(_load_blocks(bo_s), wpb[...])
        x1 = xc_s[...] + mod_ref[2] * _bdot(merged.astype(BF16), wout[...])
        x1_s[...] = x1
        xc_s[...] = xn_ref[...]

    @pl.when(t == n_tiles - 1)
    def _():
        st_ref[...] = st_s[...]
        hist_ref[...] = u_b[16:HIST_BASE, :]

    @pl.when(t == n_tiles)
    def _():
        _ffn_tile(x1_s[...], mod_ref[3], mod_ref[4], mod_ref[5], g2, w1, w2, gf, y_ref)


def _sample_layer_kernel(n_grp, x_ref, mod_ref, s0_ref, cache_ref, vecs, win, walpha, wpool, wmix,
                         w1_blk, w2_blk,
                         y_ref, st_ref, hist_ref,
                         u_b, s2_b, s4_b, s8_b, h_s, q_s, k_s, v_s, sg_s, b_s, bo_s, u_s, ao_s,
                         x1_s, h2_s, acc_s):
    wu, wq, wk, wv, wg, walr, wga, wgb = _split_w_in(win)
    wpa, wpb, wout = _split_rows(wmix, MIX_SIZES)
    g1, g2, gf, balpha, pscale, gng = _vec_views(vecs)
    s = pl.program_id(0)
    seq = q_s.shape[1]
    group = s0_ref.shape[0]

    @pl.when(s == 0)
    def _():
        u_b[0:HIST_BASE, :] = jnp.zeros((HIST_BASE, POOL_WIDTH), F32)
        s2_b[0:16, :] = jnp.zeros((16, POOL_WIDTH), F32)
        _project_gla(x_ref[...], mod_ref, g1, wq, wk, wv, wg, walr, walpha, balpha, seq,
                     h_s, q_s, k_s, v_s, sg_s, b_s)
        _store_blocks(u_s, _proj(h_s[...], wu))

    @pl.when(s < n_grp)
    def _():
        scores = [[_gla_head_scores(s * group + i, h, q_s, k_s, b_s) for h in range(GLA_HEADS)]
                  for i in range(group)]

        for i in range(group):
            blk = s * group + i
            u_b[HIST_BASE - POOL_HIST:HIST_BASE, :] = cache_ref[i]
            u_b[HIST_BASE:HIST_BASE + seq, :] = u_s[blk]
            d = _pool_delta(u_b, s2_b, s4_b, s8_b, seq, PAST_LEN)
            ao_s[blk] = _pool_mix(d, wpool, pscale[...])
            hist_ref[i] = u_b[HIST_BASE + seq - POOL_HIST:HIST_BASE + seq, :]

        for i in range(group):
            def get_state(h, i=i):
                return s0_ref[i, h]

            def put_state(h, s_new, i=i):
                st_ref[i, h] = s_new

            for h in range(GLA_HEADS):
                _gla_head_apply(s * group + i, h, scores[i][h], v_s, sg_s, gng[...], bo_s,
                                get_state, put_state)

    @pl.when(s == n_grp - 1)
    def _():
        merged_a = _gated_pool(h_s, _load_blocks(ao_s), wga, wpa)
        x1 = _merge_out(x_ref[...], mod_ref[2], merged_a, _gate_b(h_s, wgb),
                        _load_blocks(bo_s), wpb, wout)
        x1_s[...] = x1
        h2_s[...] = _norm_mod(x1, g2[...], mod_ref[3], mod_ref[4])
        acc_s[...] = jnp.zeros_like(acc_s)

    @pl.when(s >= n_grp)
    def _():
        y = jnp.maximum(_bdot(h2_s[...], w1_blk[...]), 0.0)
        acc_s[...] += _bdot((y * y).astype(BF16), w2_blk[...])

    @pl.when(s == pl.num_programs(0) - 1)
    def _():
        x1 = x1_s[...]
        x2 = x1 + _rows_per_seq(mod_ref[5], x1.shape[0]) * acc_s[...]
        y_ref[...] = _rms(x2) * gf[...]


def _const_spec(shape):
    nd = len(shape)
    return pl.BlockSpec(shape, lambda *_: (0,) * nd, pipeline_mode=pl.Buffered(1))


def _params(vmem_limit=None):
    return pltpu.CompilerParams(dimension_semantics=("arbitrary",),
                                vmem_limit_bytes=vmem_limit or VMEM_LIMIT)


def _mixer_scratch(nblk, rows, hist_rows, h_slots=None):
    h_shape = (nblk * rows, D_MODEL) if h_slots is None else (h_slots, nblk * rows, D_MODEL)
    return [
        pltpu.VMEM((hist_rows, POOL_WIDTH), F32),
        pltpu.VMEM((hist_rows, POOL_WIDTH), F32),
        pltpu.VMEM((hist_rows, POOL_WIDTH - POOL_GD), F32),
        pltpu.VMEM((hist_rows, POOL_WIDTH - 2 * POOL_GD), F32),
        pltpu.VMEM(h_shape, BF16),
        pltpu.VMEM((nblk, rows, GLA_DK), F32),
        pltpu.VMEM((nblk, rows, GLA_DK), F32),
        pltpu.VMEM((nblk, rows, GLA_DV), F32),
        pltpu.VMEM((nblk, rows, GLA_DV), F32),
        pltpu.VMEM((nblk, rows, GLA_DK), F32),
        pltpu.VMEM((nblk, rows, GLA_DV), BF16),
    ]


def _adaln(c_all, w_ada, b_ada):
    rows = c_all.shape[0]
    n = w_ada.shape[1]
    return pl.pallas_call(
        _adaln_kernel,
        grid=(n // ADA_BLOCK,),
        in_specs=[pl.BlockSpec((rows, D_MODEL), lambda j: (0, 0)),
                  pl.BlockSpec((D_MODEL, ADA_BLOCK), lambda j: (0, j)),
                  pl.BlockSpec((1, ADA_BLOCK), lambda j: (0, j))],
        out_specs=pl.BlockSpec((rows, ADA_BLOCK), lambda j: (0, j)),
        out_shape=jax.ShapeDtypeStruct((rows, n), F32),
        compiler_params=_params(),
        name="adaln_mod",
    )(c_all, w_ada, b_ada)


def _prompt_layer(x, weights, ffn_weights):
    n_tok = x.shape[0]
    tl = TL_MIX
    n_tiles = n_tok // tl
    consts = tuple(weights) + tuple(ffn_weights)
    return pl.pallas_call(
        _prompt_layer_kernel,
        grid=(n_tiles + 1,),
        in_specs=[pl.BlockSpec((tl, D_MODEL), lambda t: (0, 0), pipeline_mode=pl.Buffered(1)),
                  pl.BlockSpec((tl, D_MODEL), lambda t: (jnp.minimum(t + 1, n_tiles - 1), 0))]
        + [_const_spec(w.shape) for w in consts],
        out_specs=[pl.BlockSpec((tl, D_MODEL), lambda t: (jnp.maximum(t - 1, 0), 0)),
                   pl.BlockSpec((GLA_HEADS, GLA_HK, GLA_HV), lambda t: (0, 0, 0)),
                   pl.BlockSpec((16, POOL_WIDTH), lambda t: (0, 0))],
        out_shape=[jax.ShapeDtypeStruct((n_tok, D_MODEL), F32),
                   jax.ShapeDtypeStruct((GLA_HEADS, GLA_HK, GLA_HV), F32),
                   jax.ShapeDtypeStruct((16, POOL_WIDTH), F32)],
        scratch_shapes=_mixer_scratch(tl // CHUNK, CHUNK, HIST_BASE + tl, h_slots=2) + [
            pltpu.VMEM((GLA_HEADS, GLA_HK, GLA_HV), F32),
            pltpu.VMEM((tl, D_MODEL), F32),
            pltpu.VMEM((tl, D_MODEL), BF16),
            pltpu.VMEM((tl, FF_CHUNK), BF16),
            pltpu.VMEM((tl, D_MODEL), F32),
            pltpu.VMEM((tl, D_MODEL), F32),
        ],
        compiler_params=_params(VMEM_LIMIT_LAYER),
        name="prompt_layer",
    )(x, x, *consts)


def _sample_layer(x, mod, s0, cache, weights, w1, w2):
    n_tok = x.shape[0]
    n_seq = s0.shape[0]
    seq = n_tok // n_seq
    group = SEQ_GROUP
    n_grp = n_seq // group
    n_slab = D_FF // FF_CHUNK
    w_specs = [_const_spec(w.shape) for w in weights]
    grp = lambda s: jnp.minimum(s, n_grp - 1)
    slab = lambda s: jnp.clip(s - n_grp, 0, n_slab - 1)
    st_spec = pl.BlockSpec((group, GLA_HEADS, GLA_HK, GLA_HV), lambda s: (grp(s), 0, 0, 0))
    hist_spec = pl.BlockSpec((group, POOL_HIST, POOL_WIDTH), lambda s: (grp(s), 0, 0))
    return pl.pallas_call(
        functools.partial(_sample_layer_kernel, n_grp),
        grid=(n_grp + n_slab,),
        in_specs=[_const_spec(x.shape), _const_spec(mod.shape), st_spec, hist_spec] + w_specs
        + [pl.BlockSpec((D_MODEL, FF_CHUNK), lambda s: (0, slab(s))),
           pl.BlockSpec((FF_CHUNK, D_MODEL), lambda s: (slab(s), 0))],
        out_specs=[pl.BlockSpec((n_tok, D_MODEL), lambda s: (0, 0)), st_spec, hist_spec],
        out_shape=[jax.ShapeDtypeStruct((n_tok, D_MODEL), F32),
                   jax.ShapeDtypeStruct(s0.shape, F32),
                   jax.ShapeDtypeStruct(cache.shape, F32)],
        scratch_shapes=_mixer_scratch(n_seq, seq, HIST_BASE + seq) + [
            pltpu.VMEM((n_seq, seq, POOL_WIDTH), F32),
            pltpu.VMEM((n_seq, seq, POOL_WIDTH), BF16),
            pltpu.VMEM((n_tok, D_MODEL), F32),
            pltpu.VMEM((n_tok, D_MODEL), BF16),
            pltpu.VMEM((n_tok, D_MODEL), F32),
        ],
        compiler_params=_params(VMEM_LIMIT_LAYER),
        name="sample_layer",
    )(x, mod, s0, cache, *weights, w1, w2)


def kernel(x_prompt, x_sample, c_prompt, c_sample, state_gla, cache_pool, w_ada, b_ada, norm1_g,
           w_in, w_alpha, b_alpha, w_pool, pool_scale, gla_norm_g, w_pa, w_pb, w_out, norm2_g,
           w_ff1, w_ff2, final_g):
    n_batch, n_seq_p, _ = x_prompt.shape
    n_dec, n_seq_s, _ = x_sample.shape
    assert n_batch == 1 and w_ada.shape[0] == 1
    assert n_seq_p % TL_MIX == 0 and SLAB % n_seq_s == 0 and n_seq_s % SUB == 0
    assert n_dec % SEQ_GROUP == 0

    n_c = n_batch + n_dec
    pad = (-n_c) % 8
    c_all = jnp.concatenate([c_prompt, c_sample, jnp.zeros((pad, D_MODEL), F32)], axis=0)
    mod = _adaln(c_all, w_ada[0], b_ada)
    mod = mod.reshape(n_c + pad, 6, D_MODEL).transpose(1, 0, 2)
    mod_s = mod[:, n_batch:n_c]

    rows = [norm1_g[0], norm2_g[0], final_g, jnp.concatenate([b_alpha[0], pool_scale[0]]),
            jnp.pad(gla_norm_g[0], (0, D_MODEL - GLA_HV))] + [mod[i, 0] for i in range(6)]
    assert len(rows) == VEC_MOD + 6
    vecs = jnp.pad(jnp.stack(rows), ((0, VEC_ROWS - len(rows)), (0, 0)))

    w_in_t = jnp.swapaxes(w_in[0], 0, 1).astype(BF16)
    weights = (vecs, w_in_t, w_alpha[0].astype(BF16), w_pool[0].astype(BF16),
               jnp.concatenate([w_pa[0], w_pb[0], w_out[0]], axis=0).astype(BF16))

    w1 = w_ff1[0].astype(BF16)
    w2 = w_ff2[0].astype(BF16)

    y_p, st_p, hist_p = _prompt_layer(x_prompt[0], weights, (w1, w2))
    y_s, st_s, hist_s = _sample_layer(x_sample.reshape(n_dec * n_seq_s, D_MODEL), mod_s,
                                      state_gla[0], cache_pool[0], weights, w1, w2)

    return (y_p[None], y_s.reshape(n_dec, n_seq_s, D_MODEL), st_p[None, None],
            hist_p[None, None, 1:], st_s[None], hist_s[None])
```

```python
import jax
import jax.numpy as jnp
from jax import lax
from jax.experimental import pallas as pl
from jax.experimental.pallas import tpu as pltpu

D_MODEL = 1024
PAST_LEN = 4096
POOL_WIDTH = 512
POOL_WINDOWS = (2, 4, 8, 16)
POOL_GD = 128
POOL_HIST = 15
GLA_HEADS = 4
GLA_DK = 512
GLA_DV = 1024
GLA_HK = 128
GLA_HV = 256
GLA_LOWRANK = 16
GLA_GATE_NORM = 16.0
D_FF = 4096
EPS = 1e-6
LOG2_E = 1.4426950408889634
IN_SIZES = (POOL_WIDTH, GLA_DK, GLA_DK, GLA_DV, GLA_DV, GLA_LOWRANK, D_MODEL, D_MODEL)
MIX_SIZES = (POOL_WIDTH, GLA_DV, D_MODEL)

SUB = 8
CHUNK = 64
SLAB = 64
HIST_BASE = 32
TL_MIX = 256
TL_FFN = 512
FF_CHUNK = 1024
ADA_BLOCK = 1536
SEQ_GROUP = 8
VMEM_LIMIT = 48 * 1024 * 1024
VMEM_LIMIT_LAYER = 60 * 1024 * 1024

F32 = jnp.float32
BF16 = jnp.bfloat16


def _bdot(a, b):
    return jnp.dot(a, b, preferred_element_type=F32)


def _proj(h, wt_ref):
    return lax.dot_general(h, wt_ref[...], (((1,), (1,)), ((), ())), preferred_element_type=F32)


def _split_rows(ref, sizes):
    views, lo = [], 0
    for n in sizes:
        views.append(ref.at[pl.ds(lo, n)])
        lo += n
    return views


VEC_G1, VEC_G2, VEC_GF, VEC_POOL, VEC_GNG, VEC_MOD, VEC_ROWS = 0, 1, 2, 3, 4, 5, 16


def _vec_views(vecs):
    def row(r, lo=0, n=D_MODEL):
        return vecs.at[pl.ds(r, 1), pl.ds(lo, n)]
    return (row(VEC_G1), row(VEC_G2), row(VEC_GF), row(VEC_POOL, 0, GLA_DK),
            row(VEC_POOL, GLA_DK, POOL_WIDTH), row(VEC_GNG, 0, GLA_HV))


def _split_w_in(win):
    return _split_rows(win, IN_SIZES)


def _rms(xf):
    return xf * lax.rsqrt(jnp.mean(xf * xf, axis=-1, keepdims=True) + EPS)


def _sigmoid(x):
    return 0.5 * jnp.tanh(0.5 * x) + 0.5


def _log_sigmoid(z):
    return jnp.minimum(z, 0.0) - jnp.log(1.0 + jnp.exp(-jnp.abs(z)))


def _rows_per_seq(m, n_rows):
    n_seq = m.shape[0]
    if n_seq == 1:
        return m
    rep = n_rows // n_seq
    return jnp.concatenate([jnp.broadcast_to(m[i:i + 1], (rep, m.shape[1])) for i in range(n_seq)],
                           axis=0)


def _norm_mod(x, g, shift, scale):
    n = x.shape[0]
    return (_rms(x) * g * (1.0 + _rows_per_seq(scale, n)) + _rows_per_seq(shift, n)).astype(BF16)


def _store_blocks(ref, val):
    nblk, rows = ref.shape[0], ref.shape[1]
    for i in range(nblk):
        ref[i] = val[i * rows:(i + 1) * rows].astype(ref.dtype)


def _load_blocks(ref):
    return jnp.concatenate([ref[i] for i in range(ref.shape[0])], axis=0)


def _cumsum_rows(a, period):
    n = a.shape[0]
    ri = lax.broadcasted_iota(jnp.int32, (n, n), 0)
    ci = lax.broadcasted_iota(jnp.int32, (n, n), 1)
    shift = period.bit_length() - 1
    same = jnp.right_shift(ri, shift) == jnp.right_shift(ci, shift)
    tri = jnp.where((ci <= ri) & same, 1.0, 0.0).astype(BF16)
    hi = a.astype(BF16)
    r1 = a - hi.astype(F32)
    mid = r1.astype(BF16)
    lo = (r1 - mid.astype(F32)).astype(BF16)
    return _bdot(tri, hi) + _bdot(tri, mid) + _bdot(tri, lo)


def _gla_scores(q, k, b, k_row, b_row, nsub):
    L = SUB * nsub
    blast = b_row(L - 1)
    qt = (q * jnp.exp2(b)).astype(BF16)
    kt = (k * jnp.exp2(blast - b)).astype(BF16)

    lane = lax.broadcasted_iota(jnp.int32, (SUB, L), 1)
    row = lax.broadcasted_iota(jnp.int32, (SUB, L), 0)
    qs = [q[SUB * i:SUB * (i + 1)] for i in range(nsub)]
    ks = [k[SUB * i:SUB * (i + 1)] for i in range(nsub)]
    bs = [b[SUB * i:SUB * (i + 1)] for i in range(nsub)]

    diag = []
    for i in range(nsub):
        acc = jnp.zeros((SUB, L), F32)
        for j in range(SUB):
            r = SUB * i + j
            dec = jnp.exp2(bs[i] - b_row(r))
            col = jnp.sum(qs[i] * dec * k_row(r), axis=1, keepdims=True)
            acc = jnp.where(lane == r, col, acc)
        diag.append(jnp.where(lane <= row + SUB * i, acc, 0.0))
    p = diag[0] if nsub == 1 else jnp.concatenate(diag, axis=0)

    p_off = None
    if nsub > 1:
        zero = jnp.zeros((SUB, GLA_HK), F32)
        lhs, rhs = [], []
        for j in range(nsub - 1):
            bend = b_row(SUB * j + SUB - 1)
            lrows = [zero if i <= j else qs[i] * jnp.exp2(bs[i] - bend) for i in range(nsub)]
            rrows = [ks[j] * jnp.exp2(bend - bs[j]) if i == j else zero for i in range(nsub)]
            lhs.append(jnp.concatenate(lrows, axis=0).astype(BF16))
            rhs.append(jnp.concatenate(rrows, axis=0).astype(BF16))
        lhs = jnp.concatenate(lhs, axis=1)
        rhs = jnp.concatenate(rhs, axis=1)
        p_off = lax.dot_general(lhs, rhs, (((1,), (1,)), ((), ())), preferred_element_type=F32)
    return qt, kt, p, p_off, blast


def _gla_apply(scores, v, s):
    qt, kt, p, p_off, blast = scores
    if p_off is not None:
        p = p + p_off
    o = _bdot(jnp.concatenate([qt, p.astype(BF16)], axis=1),
              jnp.concatenate([s.astype(BF16), v], axis=0))

    ri = lax.broadcasted_iota(jnp.int32, (GLA_HK, GLA_HK), 0)
    ci = lax.broadcasted_iota(jnp.int32, (GLA_HK, GLA_HK), 1)
    erow = jnp.broadcast_to(jnp.exp2(blast), (GLA_HK, GLA_HK))
    ecol = jnp.sum(jnp.where(ri == ci, erow, 0.0), axis=1, keepdims=True)
    s_new = s * ecol + lax.dot_general(kt, v, (((0,), (0,)), ((), ())), preferred_element_type=F32)
    return o, s_new


def _gla_head_scores(blk, h, q_s, k_s, b_s):
    ksl = slice(h * GLA_HK, (h + 1) * GLA_HK)
    k_row = lambda r: k_s[blk, r:r + 1, ksl]
    b_row = lambda r: b_s[blk, r:r + 1, ksl]
    return _gla_scores(q_s[blk, :, ksl], k_s[blk, :, ksl], b_s[blk, :, ksl], k_row, b_row,
                       q_s.shape[1] // SUB)


def _gla_head_apply(blk, h, scores, v_s, sg_s, gng, bo_s, get_state, put_state):
    vsl = slice(h * GLA_HV, (h + 1) * GLA_HV)
    o, s_new = _gla_apply(scores, v_s[blk, :, vsl].astype(BF16), get_state(h))
    put_state(h, s_new)
    o = _rms(o) * gng
    bo_s[blk, :, vsl] = (o * sg_s[blk, :, vsl]).astype(BF16)


def _pool_delta(u_b, s2_b, s4_b, s8_b, n, pos0):
    r = HIST_BASE + n
    gd = POOL_GD
    s2_b[16:r, :] = u_b[16:r, :] + u_b[15:r - 1, :]
    s4_b[16:r, :] = s2_b[16:r, gd:] + s2_b[14:r - 2, gd:]
    s8_b[24:r, :] = s4_b[24:r, gd:] + s4_b[20:r - 4, gd:]
    s16 = s8_b[HIST_BASE:r, gd:] + s8_b[HIST_BASE - 8:r - 8, gd:]
    sums = (s2_b[HIST_BASE:r, 0:gd], s4_b[HIST_BASE:r, 0:gd], s8_b[HIST_BASE:r, 0:gd], s16)
    pos1 = pos0 + lax.broadcasted_iota(jnp.int32, (n, 1), 0) + 1
    out = []
    for gi, w in enumerate(POOL_WINDOWS):
        cnt = jnp.minimum(pos1, w).astype(F32)
        out.append(sums[gi] / cnt - u_b[HIST_BASE:r, gi * POOL_GD:(gi + 1) * POOL_GD])
    return out


def _pool_mix(d, wpool, pscale):
    mixed = [_bdot(d[gi].astype(BF16), wpool[gi]) for gi in range(len(POOL_WINDOWS))]
    return (jnp.concatenate(mixed, axis=1) * pscale).astype(BF16)


def _project_gla(x, mod_ref, g1, wq, wk, wv, wg, walr, walpha, balpha, period,
                 h_s, q_s, k_s, v_s, sg_s, b_s):
    h = _norm_mod(x, g1[...], mod_ref[0], mod_ref[1])
    h_s[...] = h
    _store_blocks(q_s, _proj(h, wq) * (GLA_HK ** -0.5))
    _store_blocks(k_s, _proj(h, wk))
    _store_blocks(v_s, _proj(h, wv))
    g = _proj(h, wg)
    _store_blocks(sg_s, g * _sigmoid(g))
    alr = _proj(h, walr)
    z = _bdot(alr.astype(BF16), walpha[...]) + balpha[...]
    a = _log_sigmoid(z) * (LOG2_E / GLA_GATE_NORM)
    b = [_cumsum_rows(a[r:r + SLAB], period) for r in range(0, a.shape[0], SLAB)]
    _store_blocks(b_s, jnp.concatenate(b, axis=0))


def _gated_pool(h_s, aout, wga, wpa):
    return _sigmoid(_proj(h_s[...], wga)) * _bdot(aout, wpa[...])


def _gate_b(h_s, wgb):
    return _sigmoid(_proj(h_s[...], wgb))


def _merge_out(x, gate1, merged_a, gate_b, bo, wpb, wout):
    merged = merged_a + gate_b * _bdot(bo, wpb[...])
    y = _bdot(merged.astype(BF16), wout[...])
    return x + _rows_per_seq(gate1, x.shape[0]) * y


def _ffn_tile(x, shift, scale, gate, g2, w1, w2, gf, o_ref):
    h2 = _norm_mod(x, g2[...], shift, scale)
    acc = jnp.zeros(x.shape, F32)
    for c in range(D_FF // FF_CHUNK):
        sl = slice(c * FF_CHUNK, (c + 1) * FF_CHUNK)
        y = jnp.maximum(_bdot(h2, w1[:, sl]), 0.0)
        acc = acc + _bdot((y * y).astype(BF16), w2[sl, :])
    x2 = x + _rows_per_seq(gate, x.shape[0]) * acc
    o_ref[...] = _rms(x2) * gf[...]


def _adaln_kernel(c_ref, w_ref, b_ref, o_ref):
    c = c_ref[...]
    sc = (c * _sigmoid(c)).astype(BF16)
    o_ref[...] = _bdot(sc, w_ref[...].astype(BF16)) + b_ref[...]


def _prompt_layer_kernel(x_ref, xn_ref, vecs, win, walpha, wpool, wmix, w1, w2,
                         y_ref, st_ref, hist_ref,
                         u_b, s2_b, s4_b, s8_b, hh_s, q_s, k_s, v_s, sg_s, b_s, bo_s,
                         st_s, x1_s, h2_s, y_s, acc_s):
    wu, wq, wk, wv, wg, walr, wga, wgb = _split_w_in(win)
    wpa, wpb, wout = _split_rows(wmix, MIX_SIZES)
    g1, g2, gf, balpha, pscale, gng = _vec_views(vecs)
    mod_ref = [vecs[VEC_MOD + i:VEC_MOD + i + 1, :] for i in range(6)]
    t = pl.program_id(0)
    n_tiles = pl.num_programs(0) - 1
    tl = x_ref.shape[0]
    n_chunks = tl // CHUNK
    n_piece = GLA_HEADS
    piece = D_MODEL // n_piece
    assert n_chunks * FF_CHUNK == w1.shape[1] and FF_CHUNK == n_piece * piece

    @pl.when(t == 0)
    def _():
        st_s[...] = jnp.zeros_like(st_s)
        u_b[0:HIST_BASE, :] = jnp.zeros((HIST_BASE, POOL_WIDTH), F32)
        s2_b[0:16, :] = jnp.zeros((16, POOL_WIDTH), F32)
        x1_s[...] = jnp.zeros_like(x1_s)
        hh_s[0] = _norm_mod(x_ref[...], g1[...], mod_ref[0], mod_ref[1])

    @pl.when(t < n_tiles)
    def _():
        slot = lax.rem(t, 2)
        alr = _proj(hh_s[slot], walr)
        _store_blocks(k_s, _proj(hh_s[slot], wk))
        h2_s[...] = _norm_mod(x1_s[...], g2[...], mod_ref[3], mod_ref[4])
        z = _bdot(alr.astype(BF16), walpha[...]) + balpha[...]
        a = _log_sigmoid(z) * (LOG2_E / GLA_GATE_NORM)
        _store_blocks(q_s, _proj(hh_s[slot], wq) * (GLA_HK ** -0.5))
        b = [_cumsum_rows(a[r:r + SLAB], CHUNK) for r in range(0, tl, SLAB)]
        _store_blocks(b_s, jnp.concatenate(b, axis=0))
        _store_blocks(v_s, _proj(hh_s[slot], wv))
        g = _proj(hh_s[slot], wg)
        _store_blocks(sg_s, g * _sigmoid(g))

        def get_state(h):
            return st_s[h]

        def put_state(h, s_new):
            st_s[h] = s_new

        acc_s[...] = jnp.zeros_like(acc_s)

        def chunk_body(c):
            def scores(h):
                return _gla_head_scores(c, h, q_s, k_s, b_s)

            def apply(h, sc):
                _gla_head_apply(c, h, sc, v_s, sg_s, gng[...], bo_s, get_state, put_state)

            def ffn_up(j):
                cols = slice(j * piece, (j + 1) * piece)
                w1_cols = slice(c * FF_CHUNK + j * piece, c * FF_CHUNK + (j + 1) * piece)
                y = jnp.maximum(_bdot(h2_s[...], w1[:, w1_cols]), 0.0)
                y_s[:, cols] = (y * y).astype(BF16)

            def ffn_down(j):
                cols = slice(j * piece, (j + 1) * piece)
                acc_s[:, cols] += _bdot(y_s[...], w2[c * FF_CHUNK:(c + 1) * FF_CHUNK, cols])

            sc0 = scores(0)
            sc1 = scores(1)
            ffn_up(0)
            sc2 = scores(2)
            sc3 = scores(3)
            ffn_up(1)
            apply(0, sc0)
            ffn_up(2)
            apply(1, sc1)
            ffn_up(3)
            apply(2, sc2)
            ffn_down(0)
            apply(3, sc3)
            ffn_down(1)
            ffn_down(2)
            ffn_down(3)

        for c in range(n_chunks):
            chunk_body(c)

        x2 = x1_s[...] + mod_ref[5] * acc_s[...]
        y_ref[...] = _rms(x2) * gf[...]

        hh_s[1 - slot] = _norm_mod(xn_ref[...], g1[...], mod_ref[0], mod_ref[1])

        u_b[HIST_BASE:HIST_BASE + tl, :] = _proj(hh_s[slot], wu)
        gate_a = _sigmoid(_proj(hh_s[slot], wga))
        gate_b = _sigmoid(_proj(hh_s[slot], wgb))
        d = _pool_delta(u_b, s2_b, s4_b, s8_b, tl, t * tl)
        aout = _pool_mix(d, wpool, pscale[...])
        u_b[16:HIST_BASE, :] = u_b[16 + tl:HIST_BASE + tl, :]
        merged = gate_a * _bdot(aout, wpa[...]) + gate_b * _bdot(_load_blocks(bo_s), wpb[...])
        x1 = x_ref[...] + mod_ref[2] * _bdot(merged.astype(BF16), wout[...])
        x1_s[...] = x1

    @pl.when(t == n_tiles - 1)
    def _():
        st_ref[...] = st_s[...]
        hist_ref[...] = u_b[16:HIST_BASE, :]

    @pl.when(t == n_tiles)
    def _():
        _ffn_tile(x1_s[...], mod_ref[3], mod_ref[4], mod_ref[5], g2, w1, w2, gf, y_ref)


def _sample_mixer_kernel(x_ref, mod_ref, s0_ref, cache_ref, vecs, win, walpha, wpool, wmix,
                         x1_ref, st_ref, hist_ref,
                         u_b, s2_b, s4_b, s8_b, h_s, q_s, k_s, v_s, sg_s, b_s, bo_s, u_s, ao_s):
    wu, wq, wk, wv, wg, walr, wga, wgb = _split_w_in(win)
    wpa, wpb, wout = _split_rows(wmix, MIX_SIZES)
    g1, _, _, balpha, pscale, gng = _vec_views(vecs)
    s = pl.program_id(0)
    seq = q_s.shape[1]
    group = s0_ref.shape[0]

    @pl.when(s == 0)
    def _():
        u_b[0:HIST_BASE, :] = jnp.zeros((HIST_BASE, POOL_WIDTH), F32)
        s2_b[0:16, :] = jnp.zeros((16, POOL_WIDTH), F32)
        _project_gla(x_ref[...], mod_ref, g1, wq, wk, wv, wg, walr, walpha, balpha, seq,
                     h_s, q_s, k_s, v_s, sg_s, b_s)
        _store_blocks(u_s, _proj(h_s[...], wu))

    scores = [[_gla_head_scores(s * group + i, h, q_s, k_s, b_s) for h in range(GLA_HEADS)]
              for i in range(group)]

    for i in range(group):
        blk = s * group + i
        u_b[HIST_BASE - POOL_HIST:HIST_BASE, :] = cache_ref[i]
        u_b[HIST_BASE:HIST_BASE + seq, :] = u_s[blk]
        d = _pool_delta(u_b, s2_b, s4_b, s8_b, seq, PAST_LEN)
        ao_s[blk] = _pool_mix(d, wpool, pscale[...])
        hist_ref[i] = u_b[HIST_BASE + seq - POOL_HIST:HIST_BASE + seq, :]

    for i in range(group):
        def get_state(h, i=i):
            return s0_ref[i, h]

        def put_state(h, s_new, i=i):
            st_ref[i, h] = s_new

        for h in range(GLA_HEADS):
            _gla_head_apply(s * group + i, h, scores[i][h], v_s, sg_s, gng[...], bo_s,
                            get_state, put_state)

    @pl.when(s == pl.num_programs(0) - 1)
    def _():
        merged_a = _gated_pool(h_s, _load_blocks(ao_s), wga, wpa)
        x1_ref[...] = _merge_out(x_ref[...], mod_ref[2], merged_a, _gate_b(h_s, wgb),
                                 _load_blocks(bo_s), wpb, wout)


def _ffn_kernel(x_ref, mod_ref, vecs, w1, w2, o_ref):
    _, g2, gf, _, _, _ = _vec_views(vecs)
    _ffn_tile(x_ref[...], mod_ref[0], mod_ref[1], mod_ref[2], g2, w1, w2, gf, o_ref)


def _const_spec(shape):
    nd = len(shape)
    return pl.BlockSpec(shape, lambda *_: (0,) * nd, pipeline_mode=pl.Buffered(1))


def _params(vmem_limit=None):
    return pltpu.CompilerParams(dimension_semantics=("arbitrary",),
                                vmem_limit_bytes=vmem_limit or VMEM_LIMIT)


def _mixer_scratch(nblk, rows, hist_rows, h_slots=None):
    h_shape = (nblk * rows, D_MODEL) if h_slots is None else (h_slots, nblk * rows, D_MODEL)
    return [
        pltpu.VMEM((hist_rows, POOL_WIDTH), F32),
        pltpu.VMEM((hist_rows, POOL_WIDTH), F32),
        pltpu.VMEM((hist_rows, POOL_WIDTH - POOL_GD), F32),
        pltpu.VMEM((hist_rows, POOL_WIDTH - 2 * POOL_GD), F32),
        pltpu.VMEM(h_shape, BF16),
        pltpu.VMEM((nblk, rows, GLA_DK), F32),
        pltpu.VMEM((nblk, rows, GLA_DK), F32),
        pltpu.VMEM((nblk, rows, GLA_DV), F32),
        pltpu.VMEM((nblk, rows, GLA_DV), F32),
        pltpu.VMEM((nblk, rows, GLA_DK), F32),
        pltpu.VMEM((nblk, rows, GLA_DV), BF16),
    ]


def _adaln(c_all, w_ada, b_ada):
    rows = c_all.shape[0]
    n = w_ada.shape[1]
    return pl.pallas_call(
        _adaln_kernel,
        grid=(n // ADA_BLOCK,),
        in_specs=[pl.BlockSpec((rows, D_MODEL), lambda j: (0, 0)),
                  pl.BlockSpec((D_MODEL, ADA_BLOCK), lambda j: (0, j)),
                  pl.BlockSpec((1, ADA_BLOCK), lambda j: (0, j))],
        out_specs=pl.BlockSpec((rows, ADA_BLOCK), lambda j: (0, j)),
        out_shape=jax.ShapeDtypeStruct((rows, n), F32),
        compiler_params=_params(),
        name="adaln_mod",
    )(c_all, w_ada, b_ada)


def _prompt_layer(x, weights, ffn_weights):
    n_tok = x.shape[0]
    tl = TL_MIX
    n_tiles = n_tok // tl
    consts = tuple(weights) + tuple(ffn_weights)
    return pl.pallas_call(
        _prompt_layer_kernel,
        grid=(n_tiles + 1,),
        in_specs=[pl.BlockSpec((tl, D_MODEL), lambda t: (jnp.minimum(t, n_tiles - 1), 0)),
                  pl.BlockSpec((tl, D_MODEL), lambda t: (jnp.minimum(t + 1, n_tiles - 1), 0))]
        + [_const_spec(w.shape) for w in consts],
        out_specs=[pl.BlockSpec((tl, D_MODEL), lambda t: (jnp.maximum(t - 1, 0), 0)),
                   pl.BlockSpec((GLA_HEADS, GLA_HK, GLA_HV), lambda t: (0, 0, 0)),
                   pl.BlockSpec((16, POOL_WIDTH), lambda t: (0, 0))],
        out_shape=[jax.ShapeDtypeStruct((n_tok, D_MODEL), F32),
                   jax.ShapeDtypeStruct((GLA_HEADS, GLA_HK, GLA_HV), F32),
                   jax.ShapeDtypeStruct((16, POOL_WIDTH), F32)],
        scratch_shapes=_mixer_scratch(tl // CHUNK, CHUNK, HIST_BASE + tl, h_slots=2) + [
            pltpu.VMEM((GLA_HEADS, GLA_HK, GLA_HV), F32),
            pltpu.VMEM((tl, D_MODEL), F32),
            pltpu.VMEM((tl, D_MODEL), BF16),
            pltpu.VMEM((tl, FF_CHUNK), BF16),
            pltpu.VMEM((tl, D_MODEL), F32),
        ],
        compiler_params=_params(VMEM_LIMIT_LAYER),
        name="prompt_layer",
    )(x, x, *consts)


def _sample_mixer(x, mod, s0, cache, weights):
    n_tok = x.shape[0]
    n_seq = s0.shape[0]
    seq = n_tok // n_seq
    group = SEQ_GROUP
    w_specs = [_const_spec(w.shape) for w in weights]
    st_spec = pl.BlockSpec((group, GLA_HEADS, GLA_HK, GLA_HV), lambda s: (s, 0, 0, 0))
    hist_spec = pl.BlockSpec((group, POOL_HIST, POOL_WIDTH), lambda s: (s, 0, 0))
    return pl.pallas_call(
        _sample_mixer_kernel,
        grid=(n_seq // group,),
        in_specs=[_const_spec(x.shape), _const_spec(mod.shape), st_spec, hist_spec] + w_specs,
        out_specs=[pl.BlockSpec((n_tok, D_MODEL), lambda s: (0, 0)), st_spec, hist_spec],
        out_shape=[jax.ShapeDtypeStruct((n_tok, D_MODEL), F32),
                   jax.ShapeDtypeStruct(s0.shape, F32),
                   jax.ShapeDtypeStruct(cache.shape, F32)],
        scratch_shapes=_mixer_scratch(n_seq, seq, HIST_BASE + seq) + [
            pltpu.VMEM((n_seq, seq, POOL_WIDTH), F32),
            pltpu.VMEM((n_seq, seq, POOL_WIDTH), BF16),
        ],
        compiler_params=_params(VMEM_LIMIT_LAYER),
        name="sample_mixer",
    )(x, mod, s0, cache, *weights)


def _ffn_final(x, mod, vecs, w1, w2):
    n_tok = x.shape[0]
    tl = min(TL_FFN, n_tok)
    assert mod.shape[1] == 1 or n_tok == tl
    return pl.pallas_call(
        _ffn_kernel,
        grid=(n_tok // tl,),
        in_specs=[pl.BlockSpec((tl, D_MODEL), lambda t: (t, 0)), _const_spec(mod.shape),
                  _const_spec(vecs.shape), _const_spec(w1.shape), _const_spec(w2.shape)],
        out_specs=pl.BlockSpec((tl, D_MODEL), lambda t: (t, 0)),
        out_shape=jax.ShapeDtypeStruct((n_tok, D_MODEL), F32),
        compiler_params=_params(),
        name="ffn_final",
    )(x, mod, vecs, w1, w2)


def kernel(x_prompt, x_sample, c_prompt, c_sample, state_gla, cache_pool, w_ada, b_ada, norm1_g,
           w_in, w_alpha, b_alpha, w_pool, pool_scale, gla_norm_g, w_pa, w_pb, w_out, norm2_g,
           w_ff1, w_ff2, final_g):
    n_batch, n_seq_p, _ = x_prompt.shape
    n_dec, n_seq_s, _ = x_sample.shape
    assert n_batch == 1 and w_ada.shape[0] == 1
    assert n_seq_p % TL_MIX == 0 and SLAB % n_seq_s == 0 and n_seq_s % SUB == 0
    assert n_dec % SEQ_GROUP == 0

    n_c = n_batch + n_dec
    pad = (-n_c) % 8
    c_all = jnp.concatenate([c_prompt, c_sample, jnp.zeros((pad, D_MODEL), F32)], axis=0)
    mod = _adaln(c_all, w_ada[0], b_ada)
    mod = mod.reshape(n_c + pad, 6, D_MODEL).transpose(1, 0, 2)
    mod_s = mod[:, n_batch:n_c]

    rows = [norm1_g[0], norm2_g[0], final_g, jnp.concatenate([b_alpha[0], pool_scale[0]]),
            jnp.pad(gla_norm_g[0], (0, D_MODEL - GLA_HV))] + [mod[i, 0] for i in range(6)]
    assert len(rows) == VEC_MOD + 6
    vecs = jnp.pad(jnp.stack(rows), ((0, VEC_ROWS - len(rows)), (0, 0)))

    w_in_t = jnp.swapaxes(w_in[0], 0, 1).astype(BF16)
    weights = (vecs, w_in_t, w_alpha[0].astype(BF16), w_pool[0].astype(BF16),
               jnp.concatenate([w_pa[0], w_pb[0], w_out[0]], axis=0).astype(BF16))

    w1 = w_ff1[0].astype(BF16)
    w2 = w_ff2[0].astype(BF16)

    y_p, st_p, hist_p = _prompt_layer(x_prompt[0], weights, (w1, w2))
    x1_s, st_s, hist_s = _sample_mixer(x_sample.reshape(n_dec * n_seq_s, D_MODEL), mod_s[0:3],
                                       state_gla[0], cache_pool[0], weights)
    y_s = _ffn_final(x1_s, mod_s[3:6], vecs, w1, w2)

    return (y_p[None], y_s.reshape(n_dec, n_seq_s, D_MODEL), st_p[None, None],
            hist_p[None, None, 1:], st_s[None], hist_s[None])
```
